```python
import math
import jax, jax.numpy as jnp
from jax import lax
import numpy as np

D_MODEL = 1024
BATCH = 4
SEQ = 4096
DEPTH = 4
DEC_BATCH = 16
DEC_SEQ = 16
PAST_LEN = 4096

CHUNK = 64
N_MIXERS = 3
N_A = len(range(0, DEPTH, N_MIXERS))
N_B = len(range(1, DEPTH, N_MIXERS))
N_C = len(range(2, DEPTH, N_MIXERS))
D_PLE = 256
D_FF = 4 * D_MODEL
DN_ALPHA = (2.0 * DEPTH) ** 0.25
DN_BETA = (8.0 * DEPTH) ** -0.25
LN_EPS = 1e-5
A_HEAD = 64
A_HEADS = D_MODEL // A_HEAD
A_DECAY_LORA = 64
A_AAA_LORA = 64
A_MV_LORA = 32
A_GATE_LORA = 160
A_GN_EPS = 64e-5
B_HEAD = 64
B_HEADS = D_MODEL // B_HEAD
Q_BLOCK = 128
C_HEADS = 8
C_QK = D_MODEL // C_HEADS
C_V = 2 * D_MODEL // C_HEADS
ROPE_BASE = 10000.0

kernel_name = "hybrid_rwkv7_fox_retnet_stream_step"


def layer_norm(x, g, b, eps=LN_EPS):
    xf = x.astype(jnp.float32)
    mu = xf.mean(-1, keepdims=True)
    var = jnp.mean(jnp.square(xf - mu), -1, keepdims=True)
    return ((xf - mu) * lax.rsqrt(var + eps) * g + b).astype(x.dtype)


def group_norm(x, g, b, n_groups, eps):
    *lead, d = x.shape
    xh = x.astype(jnp.float32).reshape(*lead, n_groups, d // n_groups)
    mu = xh.mean(-1, keepdims=True)
    var = jnp.mean(jnp.square(xh - mu), -1, keepdims=True)
    y = ((xh - mu) * lax.rsqrt(var + eps)).reshape(*lead, d)
    return (y * g + b).astype(x.dtype)


def rms_norm(x, g, eps=1e-6):
    xf = x.astype(jnp.float32)
    return (xf * lax.rsqrt(jnp.mean(xf * xf, -1, keepdims=True) + eps) * g).astype(x.dtype)


def rope(x, pos):
    d = x.shape[-1]
    inv = ROPE_BASE ** (-jnp.arange(0, d, 2, dtype=jnp.float32) / d)
    ang = pos[:, None].astype(jnp.float32) * inv[None]
    cos, sin = jnp.cos(ang)[None, :, None, :], jnp.sin(ang)[None, :, None, :]
    x1, x2 = x[..., : d // 2], x[..., d // 2:]
    return jnp.concatenate([x1 * cos - x2 * sin, x1 * sin + x2 * cos], axis=-1)


def squared_relu_ffn(x, w1, w2):
    h = jax.nn.relu(x @ w1)
    return (h * h) @ w2


def rwkv7_mixer(x, shift_prev, wkv_prev, v_first, vres, mu, w_rkv, w0, w1, w2,
                a0, a1, a2, g1, g2, k_k, k_a, r_k, gn_g, gn_b, w_o):
    B, T, D = x.shape
    H, N = A_HEADS, A_HEAD
    x_prev = jnp.concatenate([shift_prev[:, None, :].astype(x.dtype), x[:, :-1]], axis=1)
    xs = x[None] + (x_prev - x)[None] * mu[:, None, None, :]
    xr, xw, xk, xv, xa, xg = xs[0], xs[1], xs[2], xs[3], xs[4], xs[5]
    rkv = jnp.einsum('nbtd,nde->nbte', jnp.stack([xr, xk, xv]), w_rkv)
    r, k, v = rkv[0], rkv[1], rkv[2]
    w_log = -jax.nn.softplus(-(w0 + jnp.tanh(xw @ w1) @ w2)) - 0.5
    if vres is None:
        v_first = v
    else:
        v0, v1, v2 = vres
        v = v + (v_first - v) * jax.nn.sigmoid(v0 + (xv @ v1) @ v2)
    a = jax.nn.sigmoid(a0 + (xa @ a1) @ a2)
    g = jax.nn.sigmoid(xg @ g1) @ g2
    heads = lambda t: t.reshape(B, T, H, N)
    kk = heads(k * k_k).astype(jnp.float32)
    kk = kk / jnp.maximum(jnp.sqrt(jnp.sum(kk * kk, -1, keepdims=True)), 1e-12)
    k = k * (1 + (a - 1) * k_a)
    decay = jnp.exp(-jnp.exp(w_log.astype(jnp.float32)))
    seq_major = lambda t: jnp.moveaxis(heads(t).astype(jnp.float32), 1, 0)
    af = heads(a).astype(jnp.float32)
    scan_in = (seq_major(r), seq_major(decay), seq_major(k), seq_major(v),
               jnp.moveaxis(kk, 1, 0), jnp.moveaxis(kk * af, 1, 0))

    def step(S, inp):
        r_t, d_t, k_t, v_t, kk_t, b_t = inp
        sa = jnp.einsum('bhvk,bhk->bhv', S, kk_t)
        S = S * d_t[:, :, None, :] - sa[..., None] * b_t[:, :, None, :] + v_t[..., None] * k_t[:, :, None, :]
        return S, jnp.einsum('bhvk,bhk->bhv', S, r_t)

    S, y = lax.scan(step, wkv_prev.astype(jnp.float32), scan_in)
    y = jnp.moveaxis(y, 0, 1).reshape(B, T, D).astype(x.dtype)
    y = group_norm(y, gn_g, gn_b, H, A_GN_EPS)
    bonus = jnp.sum(heads(r) * heads(k) * r_k.reshape(H, N), -1, keepdims=True) * heads(v)
    y = y + bonus.reshape(B, T, D)
    out = (y * g) @ w_o
    return out, x[:, -1], S, v_first


def fox_attend(q, c_q, q_pos, k, v, c_k, k_pos):
    s = jnp.einsum('bqhd,bkhd->bhqk', q, k).astype(jnp.float32) * (B_HEAD ** -0.5)
    s = s + jnp.swapaxes(c_q, 1, 2)[..., :, None] - jnp.swapaxes(c_k, 1, 2)[..., None, :]
    s = jnp.where(k_pos[None, :] <= q_pos[:, None], s, -jnp.inf)
    p = jax.nn.softmax(s, axis=-1).astype(v.dtype)
    return jnp.einsum('bhqk,bkhd->bqhd', p, v)


def fox_mixer(x, cache, w_in, b_f, q_gain, k_gain, w_o):
    B, T, D = x.shape
    H, dh = B_HEADS, B_HEAD
    h = x @ w_in
    q, k, v, gate = h[..., :D], h[..., D:2 * D], h[..., 2 * D:3 * D], h[..., 3 * D:4 * D]
    logf = jax.nn.log_sigmoid((h[..., 4 * D:] + b_f).astype(jnp.float32))
    q = rms_norm(q.reshape(B, T, H, dh), q_gain)
    k = rms_norm(k.reshape(B, T, H, dh), k_gain)
    v = v.reshape(B, T, H, dh)
    c_new = jnp.cumsum(logf, axis=1)
    if cache is None:
        pos = jnp.arange(T)
        nb = T // Q_BLOCK
        blocks = (jnp.swapaxes(q.reshape(B, nb, Q_BLOCK, H, dh), 0, 1),
                  jnp.swapaxes(c_new.reshape(B, nb, Q_BLOCK, H), 0, 1),
                  pos.reshape(nb, Q_BLOCK))
        o = lax.map(lambda blk: fox_attend(blk[0], blk[1], blk[2], k, v, c_new, pos), blocks)
        o = jnp.swapaxes(o, 0, 1).reshape(B, T, D)
    else:
        ck, cv, clogf = cache
        P = ck.shape[1]
        c_cache = jnp.cumsum(clogf.astype(jnp.float32), axis=1)
        c_q = c_cache[:, -1:, :] + c_new
        k_all = jnp.concatenate([ck.astype(k.dtype), k], axis=1)
        v_all = jnp.concatenate([cv.astype(v.dtype), v], axis=1)
        c_all = jnp.concatenate([c_cache, c_q], axis=1)
        o = fox_attend(q, c_q, P + jnp.arange(T), k_all, v_all, c_all, jnp.arange(P + T)).reshape(B, T, D)
    out = (jax.nn.sigmoid(gate) * o) @ w_o
    return out, k, v, logf


def retention_mixer(x, R_prev, pos0, w_in, gn_g, gn_b, w_o):
    B, T, D = x.shape
    H, dk, dv = C_HEADS, C_QK, C_V
    h = x @ w_in
    pos = pos0 + jnp.arange(T)
    q = rope(h[..., :D].reshape(B, T, H, dk).astype(jnp.float32), pos)
    k = rope(h[..., D:2 * D].reshape(B, T, H, dk).astype(jnp.float32), pos) * (dk ** -0.5)
    v = h[..., 2 * D:4 * D].reshape(B, T, H, dv).astype(jnp.float32)
    g = h[..., 4 * D:]
    Lc = min(CHUNK, T)
    nc = T // Lc
    log_g = jnp.log1p(-jnp.exp2(-5.0 - jnp.arange(H, dtype=jnp.float32)))
    idx = jnp.arange(Lc, dtype=jnp.float32)
    rel = idx[:, None] - idx[None, :]
    intra_decay = jnp.where(rel[None] >= 0, jnp.exp(rel[None] * log_g[:, None, None]), 0.0)
    q_decay = jnp.exp((idx[:, None] + 1.0) * log_g[None])
    k_decay = jnp.exp((Lc - 1.0 - idx[:, None]) * log_g[None])
    chunk_decay = jnp.exp(Lc * log_g)
    to_chunks = lambda t: jnp.moveaxis(t.reshape(B, nc, Lc, H, t.shape[-1]), 1, 0)

    def step(R, inp):
        qc, kc, vc = inp
        s = jnp.einsum('bihd,bjhd->bhij', qc, kc) * intra_decay[None]
        o = jnp.einsum('bhij,bjhe->bihe', s, vc) + \
            jnp.einsum('bihd,bhde->bihe', qc, R) * q_decay[None, :, :, None]
        R = R * chunk_decay[None, :, None, None] + jnp.einsum('bjhd,jh,bjhe->bhde', kc, k_decay, vc)
        return R, o

    R, o = lax.scan(step, R_prev.astype(jnp.float32), (to_chunks(q), to_chunks(k), to_chunks(v)))
    o = jnp.moveaxis(o, 0, 1).reshape(B, T, 2 * D)
    o = group_norm(o, gn_g, gn_b, H, LN_EPS).astype(x.dtype)
    out = (jax.nn.silu(g) * o) @ w_o
    return out, R


def run_trunk(x, ple, pos0, a_shift, a_wkv, b_cache, c_state, wt):
    v_first = None
    na_shift, na_wkv, nb_k, nb_v, nb_logf, nc_state = [], [], [], [], [], []
    for i in range(DEPTH):
        kind, j = i % N_MIXERS, i // N_MIXERS
        if kind == 0:
            vres = None if j == 0 else (wt['a_v0'][j - 1], wt['a_v1'][j - 1], wt['a_v2'][j - 1])
            mix, sh, wkv, vf = rwkv7_mixer(
                x, a_shift[j], a_wkv[j], v_first, vres, wt['a_mu'][j], wt['a_w_rkv'][j],
                wt['a_w0'][j], wt['a_w1'][j], wt['a_w2'][j], wt['a_a0'][j], wt['a_a1'][j], wt['a_a2'][j],
                wt['a_g1'][j], wt['a_g2'][j], wt['a_k_k'][j], wt['a_k_a'][j], wt['a_r_k'][j],
                wt['a_gn_g'][j], wt['a_gn_b'][j], wt['a_w_o'][j])
            if j == 0:
                v_first = vf
            na_shift.append(sh)
            na_wkv.append(wkv)
        elif kind == 1:
            cache = None if b_cache is None else (b_cache[0][j], b_cache[1][j], b_cache[2][j])
            mix, k, v, logf = fox_mixer(x, cache, wt['b_w_in'][j], wt['b_b_f'][j],
                                        wt['b_q_gain'][j], wt['b_k_gain'][j], wt['b_w_o'][j])
            nb_k.append(k)
            nb_v.append(v)
            nb_logf.append(logf)
        else:
            mix, R = retention_mixer(x, c_state[j], pos0, wt['c_w_in'][j], wt['c_gn_g'][j],
                                     wt['c_gn_b'][j], wt['c_w_o'][j])
            nc_state.append(R)
        x = layer_norm(DN_ALPHA * x + mix, wt['ln_mix_g'][i], wt['ln_mix_b'][i])
        x = layer_norm(DN_ALPHA * x + squared_relu_ffn(x, wt['ffn_w1'][i], wt['ffn_w2'][i]),
                       wt['ln_ffn_g'][i], wt['ln_ffn_b'][i])
        x = x + jax.nn.sigmoid(x @ wt['ple_gate'][i]) * (ple[i] @ wt['ple_in'][i])
    return (x, jnp.stack(na_shift), jnp.stack(na_wkv), jnp.stack(nb_k), jnp.stack(nb_v),
            jnp.stack(nb_logf), jnp.stack(nc_state))


def setup_inputs(seed: int = 0) -> dict:
    key = jax.random.key(seed)
    ks = iter(jax.random.split(key, 64))
    f32 = jnp.float32
    nrm = lambda shape, scale=1.0: jax.random.normal(next(ks), shape, f32) * scale
    uni = lambda shape, lo, hi: jax.random.uniform(next(ks), shape, f32, lo, hi)
    D = D_MODEL
    return {
        "x_prompt": nrm((BATCH, SEQ, D)),
        "x_sample": nrm((DEC_BATCH, DEC_SEQ, D)),
        "p_prompt": nrm((DEPTH, BATCH, SEQ, D_PLE)),
        "p_sample": nrm((DEPTH, DEC_BATCH, DEC_SEQ, D_PLE)),
        "state_a_shift": nrm((N_A, DEC_BATCH, D)),
        "state_a_wkv": nrm((N_A, DEC_BATCH, A_HEADS, A_HEAD, A_HEAD), 0.5),
        "cache_b_k": nrm((N_B, DEC_BATCH, PAST_LEN, B_HEADS, B_HEAD)),
        "cache_b_v": nrm((N_B, DEC_BATCH, PAST_LEN, B_HEADS, B_HEAD)),
        "cache_b_logf": jax.nn.log_sigmoid(uni((N_B, DEC_BATCH, PAST_LEN, B_HEADS), 1.0, 4.0)
                                           + nrm((N_B, DEC_BATCH, PAST_LEN, B_HEADS), 0.5)),
        "state_c": nrm((N_C, DEC_BATCH, C_HEADS, C_QK, C_V)),
        "ln_mix_g": 1.0 + nrm((DEPTH, D), 0.02),
        "ln_mix_b": nrm((DEPTH, D), 0.02),
        "ln_ffn_g": 1.0 + nrm((DEPTH, D), 0.02),
        "ln_ffn_b": nrm((DEPTH, D), 0.02),
        "ffn_w1": nrm((DEPTH, D, D_FF), D ** -0.5),
        "ffn_w2": nrm((DEPTH, D_FF, D), D_FF ** -0.5 * DN_BETA),
        "ple_in": nrm((DEPTH, D_PLE, D), D_PLE ** -0.5 * 0.5),
        "ple_gate": nrm((DEPTH, D, D), D ** -0.5),
        "a_mu": uni((N_A, 6, D), 0.0, 1.0),
        "a_w_rkv": nrm((N_A, 3, D, D), D ** -0.5),
        "a_w0": uni((N_A, D), -5.0, 0.0),
        "a_w1": nrm((N_A, D, A_DECAY_LORA), D ** -0.5),
        "a_w2": nrm((N_A, A_DECAY_LORA, D), A_DECAY_LORA ** -0.5 * 0.1),
        "a_a0": nrm((N_A, D), 0.1),
        "a_a1": nrm((N_A, D, A_AAA_LORA), D ** -0.5),
        "a_a2": nrm((N_A, A_AAA_LORA, D), A_AAA_LORA ** -0.5 * 0.1),
        "a_v0": nrm((N_A - 1, D), 0.1),
        "a_v1": nrm((N_A - 1, D, A_MV_LORA), D ** -0.5),
        "a_v2": nrm((N_A - 1, A_MV_LORA, D), A_MV_LORA ** -0.5 * 0.1),
        "a_g1": nrm((N_A, D, A_GATE_LORA), D ** -0.5),
        "a_g2": nrm((N_A, A_GATE_LORA, D), A_GATE_LORA ** -0.5),
        "a_k_k": 0.85 + nrm((N_A, D), 0.05),
        "a_k_a": 1.0 + nrm((N_A, D), 0.05),
        "a_r_k": nrm((N_A, D), 0.1),
        "a_gn_g": 1.0 + nrm((N_A, D), 0.02),
        "a_gn_b": nrm((N_A, D), 0.02),
        "a_w_o": nrm((N_A, D, D), D ** -0.5 * DN_BETA),
        "b_w_in": nrm((N_B, D, 4 * D + B_HEADS), D ** -0.5),
        "b_b_f": uni((N_B, B_HEADS), 1.0, 4.0),
        "b_q_gain": 1.0 + nrm((N_B, B_HEAD), 0.02),
        "b_k_gain": 1.0 + nrm((N_B, B_HEAD), 0.02),
        "b_w_o": nrm((N_B, D, D), D ** -0.5 * DN_BETA),
        "c_w_in": nrm((N_C, D, 6 * D), D ** -0.5),
        "c_gn_g": 1.0 + nrm((N_C, 2 * D), 0.02),
        "c_gn_b": nrm((N_C, 2 * D), 0.02),
        "c_w_o": nrm((N_C, 2 * D, D), (2 * D) ** -0.5 * DN_BETA),
    }


def reference(x_prompt, x_sample, p_prompt, p_sample, state_a_shift, state_a_wkv, cache_b_k, cache_b_v,
              cache_b_logf, state_c, ln_mix_g, ln_mix_b, ln_ffn_g, ln_ffn_b, ffn_w1, ffn_w2, ple_in, ple_gate,
              a_mu, a_w_rkv, a_w0, a_w1, a_w2, a_a0, a_a1, a_a2, a_v0, a_v1, a_v2, a_g1, a_g2, a_k_k, a_k_a,
              a_r_k, a_gn_g, a_gn_b, a_w_o, b_w_in, b_b_f, b_q_gain, b_k_gain, b_w_o,
              c_w_in, c_gn_g, c_gn_b, c_w_o):
    wt = dict(ln_mix_g=ln_mix_g, ln_mix_b=ln_mix_b, ln_ffn_g=ln_ffn_g, ln_ffn_b=ln_ffn_b,
              ffn_w1=ffn_w1, ffn_w2=ffn_w2, ple_in=ple_in, ple_gate=ple_gate,
              a_mu=a_mu, a_w_rkv=a_w_rkv, a_w0=a_w0, a_w1=a_w1, a_w2=a_w2, a_a0=a_a0, a_a1=a_a1, a_a2=a_a2,
              a_v0=a_v0, a_v1=a_v1, a_v2=a_v2, a_g1=a_g1, a_g2=a_g2, a_k_k=a_k_k, a_k_a=a_k_a, a_r_k=a_r_k,
              a_gn_g=a_gn_g, a_gn_b=a_gn_b, a_w_o=a_w_o, b_w_in=b_w_in, b_b_f=b_b_f, b_q_gain=b_q_gain,
              b_k_gain=b_k_gain, b_w_o=b_w_o, c_w_in=c_w_in, c_gn_g=c_gn_g, c_gn_b=c_gn_b, c_w_o=c_w_o)
    nbat = x_prompt.shape[0]
    zero_shift = jnp.zeros((N_A, nbat, D_MODEL), x_prompt.dtype)
    zero_wkv = jnp.zeros((N_A, nbat, A_HEADS, A_HEAD, A_HEAD), jnp.float32)
    zero_ret = jnp.zeros((N_C, nbat, C_HEADS, C_QK, C_V), jnp.float32)
    y_prompt, pa_shift, pa_wkv, pb_k, pb_v, pb_logf, pc_state = run_trunk(
        x_prompt, p_prompt, 0, zero_shift, zero_wkv, None, zero_ret, wt)
    y_sample, sa_shift, sa_wkv, sb_k, sb_v, sb_logf, sc_state = run_trunk(
        x_sample, p_sample, PAST_LEN, state_a_shift, state_a_wkv, (cache_b_k, cache_b_v, cache_b_logf),
        state_c, wt)
    return (y_prompt, y_sample, pa_shift, pa_wkv, pb_k, pb_v, pb_logf, pc_state,
            sa_shift, sa_wkv, sb_k, sb_v, sb_logf, sc_state)
```

```python
import functools
import math

import jax
import jax.numpy as jnp
from jax import lax
from jax.experimental import pallas as pl
from jax.experimental.pallas import tpu as pltpu

F32 = jnp.float32
BF16 = jnp.bfloat16

D_MODEL = 1024
DEPTH = 4
N_MIXERS = 3
D_PLE = 256
D_FF = 4 * D_MODEL
DN_ALPHA = (2.0 * DEPTH) ** 0.25
LN_EPS = 1e-5
A_HEAD = 64
A_HEADS = D_MODEL // A_HEAD
A_GN_EPS = 64e-5
B_HEAD = 64
B_HEADS = D_MODEL // B_HEAD
C_HEADS = 8
C_QK = D_MODEL // C_HEADS
C_V = 2 * D_MODEL // C_HEADS
ROPE_BASE = 10000.0
RMS_EPS = 1e-6

LANES = 128
PAIRS = D_MODEL // LANES
VMEM_LIMIT = 56 * 1024 * 1024
ROW_TILE = 512
FFN_CHUNK = 1024
RWKV_CHUNK = 64
RET_CHUNK = 256
ATTN_TILE = 512
CACHE_TILE = 1024


def _cparams(semantics):
    return pltpu.CompilerParams(dimension_semantics=semantics, vmem_limit_bytes=VMEM_LIMIT)


def _resident(shape):
    nd = len(shape)
    return pl.BlockSpec(shape, lambda *_: (0,) * nd, pipeline_mode=pl.Buffered(1))


def _rows(tm, width):
    return pl.BlockSpec((tm, width), lambda i: (i, 0))


def _dot(a, b):
    return jnp.dot(a.astype(BF16), b.astype(BF16), preferred_element_type=F32)


def _dot_nt(a, b):
    return lax.dot_general(a.astype(BF16), b.astype(BF16), (((1,), (1,)), ((), ())),
                           preferred_element_type=F32)


def _dot_tn(a, b):
    return lax.dot_general(a.astype(BF16), b.astype(BF16), (((0,), (0,)), ((), ())),
                           preferred_element_type=F32)


def _sigmoid(x):
    return 1.0 / (1.0 + jnp.exp(-x))


def _softplus(x):
    return jnp.maximum(x, 0.0) + jnp.log(1.0 + jnp.exp(-jnp.abs(x)))


def _layer_norm(x, g, b):
    mu = jnp.mean(x, -1, keepdims=True)
    xc = x - mu
    var = jnp.mean(xc * xc, -1, keepdims=True)
    return xc * lax.rsqrt(var + LN_EPS) * g + b


def _pair_sum(x, first):
    s0 = jnp.sum(jnp.where(first, x, 0.0), -1, keepdims=True)
    s1 = jnp.sum(jnp.where(first, 0.0, x), -1, keepdims=True)
    return jnp.where(first, s0, s1)


def _post_kernel(z_ref, x_ref, p_ref, wo_ref, g1_ref, b1_ref, w1_ref, w2_ref, g2_ref, b2_ref,
                 wg_ref, wp_ref, o_ref):
    x = x_ref[...]
    x1 = _layer_norm(DN_ALPHA * x + _dot(z_ref[...], wo_ref[...]), g1_ref[...], b1_ref[...])
    x1b = x1.astype(BF16)
    acc = jnp.zeros_like(x1)
    for f in range(D_FF // FFN_CHUNK):
        cols = slice(f * FFN_CHUNK, (f + 1) * FFN_CHUNK)
        h = jnp.maximum(_dot(x1b, w1_ref[:, cols]), 0.0)
        acc = acc + _dot(h * h, w2_ref[cols, :])
    x2 = _layer_norm(DN_ALPHA * x1 + acc, g2_ref[...], b2_ref[...])
    gate = _sigmoid(_dot(x2, wg_ref[...]))
    o_ref[...] = x2 + gate * _dot(p_ref[...], wp_ref[...])


def _post_mixer(z, x, p, wo, g1, b1, w1, w2, g2, b2, wg, wp):
    m, dz = z.shape
    tm = min(ROW_TILE, m)
    d = D_MODEL
    return pl.pallas_call(
        _post_kernel,
        out_shape=jax.ShapeDtypeStruct((m, d), F32),
        grid=(m // tm,),
        in_specs=[_rows(tm, dz), _rows(tm, d), _rows(tm, D_PLE), _resident((dz, d)),
                  _resident((1, d)), _resident((1, d)), _resident((d, D_FF)), _resident((D_FF, d)),
                  _resident((1, d)), _resident((1, d)), _resident((d, d)), _resident((D_PLE, d))],
        out_specs=_rows(tm, d),
        compiler_params=_cparams(("parallel",)),
        name="post_mixer",
    )(z, x, p, wo, g1, b1, w1, w2, g2, b2, wg, wp)


def _rwkv_proj_kernel(has_vres, *refs):
    if has_vres:
        (x_ref, xp_ref, vf_ref, mu_ref, wrkv_ref, w0_ref, w1_ref, w2_ref, a0_ref, a1_ref, a2_ref,
         g1_ref, g2_ref, v0_ref, v1_ref, v2_ref, r_o, k_o, v_o, wl_o, a_o, g_o) = refs
    else:
        (x_ref, xp_ref, mu_ref, wrkv_ref, w0_ref, w1_ref, w2_ref, a0_ref, a1_ref, a2_ref,
         g1_ref, g2_ref, r_o, k_o, v_o, wl_o, a_o, g_o) = refs
    x = x_ref[...]
    dx = xp_ref[...] - x
    xr, xw, xk, xv, xa, xg = [(x + dx * mu_ref[i:i + 1, :]).astype(BF16) for i in range(6)]
    r_o[...] = _dot(xr, wrkv_ref[0])
    k_o[...] = _dot(xk, wrkv_ref[1])
    v = _dot(xv, wrkv_ref[2])
    if has_vres:
        mix = _sigmoid(v0_ref[...] + _dot(_dot(xv, v1_ref[...]), v2_ref[...]))
        v = v + (vf_ref[...] - v) * mix
    v_o[...] = v
    pre = w0_ref[...] + _dot(jnp.tanh(_dot(xw, w1_ref[...])), w2_ref[...])
    w_log = -_softplus(-pre) - 0.5
    wl_o[...] = -jnp.exp(w_log)
    a_o[...] = _sigmoid(a0_ref[...] + _dot(_dot(xa, a1_ref[...]), a2_ref[...]))
    g_o[...] = _dot(_sigmoid(_dot(xg, g1_ref[...])), g2_ref[...])


def _rwkv_proj(x, xprev, vfirst, w):
    m, d = x.shape
    tm = min(ROW_TILE, m)
    has_vres = vfirst is not None
    args = [x, xprev] + ([vfirst] if has_vres else []) + [
        w["mu"], w["w_rkv"], w["w0"], w["w1"], w["w2"], w["a0"], w["a1"], w["a2"], w["g1"], w["g2"]]
    if has_vres:
        args += [w["v0"], w["v1"], w["v2"]]
    n_rows = 3 if has_vres else 2
    in_specs = [_rows(tm, d)] * n_rows + [_resident(a.shape) for a in args[n_rows:]]
    return pl.pallas_call(
        functools.partial(_rwkv_proj_kernel, has_vres),
        out_shape=[jax.ShapeDtypeStruct((m, d), F32)] * 6,
        grid=(m // tm,),
        in_specs=in_specs,
        out_specs=[_rows(tm, d)] * 6,
        compiler_params=_cparams(("parallel",)),
        name="rwkv_proj",
    )(*args)


def _split3(x):
    hi = x.astype(BF16)
    r1 = x - hi.astype(F32)
    mid = r1.astype(BF16)
    lo = (r1 - mid.astype(F32)).astype(BF16)
    return hi, mid, lo


def _rwkv_chunk_kernel(L, r_ref, wl_ref, k_ref, v_ref, a_ref, g_ref, kk_ref, ka_ref, rk_ref,
                       gng_ref, gnb_ref, s0_ref, z_ref, sout_ref, s_scr):
    c = pl.program_id(1)

    @pl.when(c == 0)
    def _():
        s_scr[...] = s0_ref[0]

    L2 = 2 * L
    first = lax.broadcasted_iota(jnp.int32, (L, LANES), 1) < A_HEAD
    row = lax.broadcasted_iota(jnp.int32, (L2, L2), 0)
    col = lax.broadcasted_iota(jnp.int32, (L2, L2), 1)
    same_head = (row >= L) == (col >= L)
    strict = same_head & (row > col)
    incl = same_head & (row >= col)
    eye = jnp.where(row == col, 1.0, 0.0).astype(F32)
    tri = jnp.where(lax.broadcasted_iota(jnp.int32, (L, L), 0) >=
                    lax.broadcasted_iota(jnp.int32, (L, L), 1), 1.0, 0.0).astype(BF16)

    def stack(x):
        return jnp.concatenate([jnp.where(first, x, 0.0), jnp.where(first, 0.0, x)], axis=0).astype(BF16)

    for p in range(PAIRS):
        sl = slice(p * LANES, (p + 1) * LANES)
        r = r_ref[0, :, sl]
        wl = wl_ref[0, :, sl]
        k = k_ref[0, :, sl]
        v = v_ref[0, :, sl]
        a = a_ref[0, :, sl]
        kk = k * kk_ref[:, sl]
        kk = kk / jnp.maximum(jnp.sqrt(_pair_sum(kk * kk, first)), 1e-12)
        k = k * (1.0 + (a - 1.0) * ka_ref[:, sl])
        b = kk * a
        hi, mid, lo = _split3(wl)
        cum = (jnp.dot(tri, hi, preferred_element_type=F32) + jnp.dot(tri, mid, preferred_element_type=F32)
               + jnp.dot(tri, lo, preferred_element_type=F32))
        g_t = jnp.exp(cum)
        g_prev = jnp.exp(cum - wl)
        g_inv = jnp.exp(-cum)
        g_end = g_t[L - 1:L, :]
        kinv = k * g_inv
        binv = b * g_inv
        a_kk = stack(kk * g_prev)
        a_r = stack(r * g_t)
        s_kinv = stack(kinv)
        s_binv = stack(binv)
        s_v = stack(v)
        m_b = jnp.where(strict, _dot_nt(a_kk, s_binv), 0.0)
        m_k = jnp.where(strict, _dot_nt(a_kk, s_kinv), 0.0)
        n_k = jnp.where(incl, _dot_nt(a_r, s_kinv), 0.0)
        n_b = jnp.where(incl, _dot_nt(a_r, s_binv), 0.0)
        x_pow = -m_b
        t_inv = eye + x_pow
        n = 1
        while 2 * n < L:
            x_pow = _dot(x_pow, x_pow)
            t_inv = t_inv + _dot(t_inv, x_pow)
            n *= 2
        s_prev = s_scr[p]
        s_prev_b = s_prev.astype(BF16)
        u = _dot(t_inv, _dot_nt(a_kk, s_prev_b) + _dot(m_k, s_v))
        u_b = u.astype(BF16)
        y2 = _dot_nt(a_r, s_prev_b) + _dot(n_k, s_v) - _dot(n_b, u_b)
        y = y2[:L] + y2[L:]
        s_scr[p] = s_prev * g_end + _dot_tn(s_v, stack(kinv * g_end)) - _dot_tn(u_b, stack(binv * g_end))
        mu = _pair_sum(y, first) * (1.0 / A_HEAD)
        yc = y - mu
        var = _pair_sum(yc * yc, first) * (1.0 / A_HEAD)
        yn = yc * lax.rsqrt(var + A_GN_EPS) * gng_ref[:, sl] + gnb_ref[:, sl]
        bonus = _pair_sum(r * k * rk_ref[:, sl], first) * v
        z_ref[0, :, sl] = (yn + bonus) * g_ref[0, :, sl]

    @pl.when(c == pl.num_programs(1) - 1)
    def _():
        sout_ref[0] = s_scr[...]


def _rwkv_chunk(r, wl, k, v, a, g, w, s0_pairs):
    bsz, t, d = r.shape
    L = min(RWKV_CHUNK, t)
    seq = pl.BlockSpec((1, L, d), lambda b, c: (b, c, 0))
    vec = pl.BlockSpec((1, d), lambda b, c: (0, 0))
    st = pl.BlockSpec((1, PAIRS, LANES, LANES), lambda b, c: (b, 0, 0, 0))
    return pl.pallas_call(
        functools.partial(_rwkv_chunk_kernel, L),
        out_shape=[jax.ShapeDtypeStruct((bsz, t, d), F32),
                   jax.ShapeDtypeStruct((bsz, PAIRS, LANES, LANES), F32)],
        grid=(bsz, t // L),
        in_specs=[seq] * 6 + [vec] * 5 + [st],
        out_specs=[seq, st],
        scratch_shapes=[pltpu.VMEM((PAIRS, LANES, LANES), F32)],
        compiler_params=_cparams(("parallel", "arbitrary")),
        name="rwkv_chunk",
    )(r, wl, k, v, a, g, w["k_k"], w["k_a"], w["r_k"], w["gn_g"], w["gn_b"], s0_pairs)


def _state_to_pairs(s):
    bsz = s.shape[0]
    s = s.reshape(bsz, PAIRS, 2, A_HEAD, A_HEAD)
    z = jnp.zeros_like(s[:, :, 0])
    top = jnp.concatenate([s[:, :, 0], z], axis=-1)
    bot = jnp.concatenate([z, s[:, :, 1]], axis=-1)
    return jnp.concatenate([top, bot], axis=-2)


def _pairs_to_state(sp):
    bsz = sp.shape[0]
    s0 = sp[:, :, :A_HEAD, :A_HEAD]
    s1 = sp[:, :, A_HEAD:, A_HEAD:]
    return jnp.stack([s0, s1], axis=2).reshape(bsz, A_HEADS, A_HEAD, A_HEAD)


def _fox_proj_kernel(x_ref, w_ref, wf_ref, bf_ref, qg_ref, kg_ref,
                     qb_o, kf_o, kb_o, vf_o, vb_o, gate_o, logf_o):
    d = D_MODEL
    xb = x_ref[...].astype(BF16)
    first = lax.broadcasted_iota(jnp.int32, (xb.shape[0], LANES), 1) < B_HEAD

    def rms(t, gain_ref, blk):
        ms = _pair_sum(t * t, first) * (1.0 / B_HEAD)
        return t * lax.rsqrt(ms + RMS_EPS) * gain_ref[:, blk]

    for j in range(PAIRS):
        blk = slice(j * LANES, (j + 1) * LANES)
        q = rms(_dot(xb, w_ref[:, j * LANES:(j + 1) * LANES]), qg_ref, blk)
        qb_o[:, blk] = (q * (B_HEAD ** -0.5)).astype(BF16)
        k = rms(_dot(xb, w_ref[:, d + j * LANES:d + (j + 1) * LANES]), kg_ref, blk)
        kf_o[:, blk] = k
        kb_o[:, blk] = k.astype(BF16)
    v = _dot(xb, w_ref[:, 2 * d:3 * d])
    vf_o[...] = v
    vb_o[...] = v.astype(BF16)
    gate_o[...] = _dot(xb, w_ref[:, 3 * d:4 * d])
    logf_o[...] = -_softplus(-(_dot(xb, wf_ref[...]) + bf_ref[...]))


def _fox_proj(x, w):
    m, d = x.shape
    tm = min(ROW_TILE, m)
    outs = [jax.ShapeDtypeStruct((m, d), BF16), jax.ShapeDtypeStruct((m, d), F32),
            jax.ShapeDtypeStruct((m, d), BF16), jax.ShapeDtypeStruct((m, d), F32),
            jax.ShapeDtypeStruct((m, d), BF16), jax.ShapeDtypeStruct((m, d), F32),
            jax.ShapeDtypeStruct((m, LANES), F32)]
    return pl.pallas_call(
        _fox_proj_kernel,
        out_shape=outs,
        grid=(m // tm,),
        in_specs=[_rows(tm, d), _resident((d, 4 * d)), _resident((d, LANES)), _resident((1, LANES)),
                  _resident((1, d)), _resident((1, d))],
        out_specs=[_rows(tm, d)] * 6 + [_rows(tm, LANES)],
        compiler_params=_cparams(("parallel",)),
        name="fox_proj",
    )(x, w["w_in"], w["w_f"], w["b_f"], w["q_gain"], w["k_gain"])


def _fox_attn_kernel(tile, q_ref, k_ref, v_ref, bk_ref, gate_ref, z_ref):
    qi = pl.program_id(2)
    q = q_ref[0]
    first = lax.broadcasted_iota(jnp.int32, (tile, LANES), 1) < B_HEAD
    zero = jnp.zeros_like(q)
    q_heads = (jnp.where(first, q, zero), jnp.where(first, zero, q))
    causal = (lax.broadcasted_iota(jnp.int32, (tile, tile), 1) <=
              lax.broadcasted_iota(jnp.int32, (tile, tile), 0))

    def step(kb, carry, diagonal):
        m0, m1, l0, l1, acc = carry
        start = pl.multiple_of(kb * tile, tile)
        k = k_ref[0, pl.ds(start, tile), :]
        v = v_ref[0, pl.ds(start, tile), :]
        bias = bk_ref[0, 0, kb]
        new_m, new_l, alphas, pvs = [], [], [], []
        for h, (m_prev, l_prev) in enumerate(((m0, l0), (m1, l1))):
            s = _dot_nt(q_heads[h], k) + bias[h:h + 1, :]
            if diagonal:
                s = jnp.where(causal, s, -jnp.inf)
            m_new = jnp.maximum(m_prev, jnp.max(s, -1, keepdims=True))
            p = jnp.exp(s - m_new)
            alpha = jnp.exp(m_prev - m_new)
            new_m.append(m_new)
            new_l.append(alpha * l_prev + jnp.sum(p, -1, keepdims=True))
            alphas.append(alpha)
            pvs.append(_dot(p, v))
        acc = jnp.where(first, alphas[0], alphas[1]) * acc + jnp.where(first, pvs[0], pvs[1])
        return new_m[0], new_m[1], new_l[0], new_l[1], acc

    neg = jnp.full((tile, 1), -jnp.inf, F32)
    zcol = jnp.zeros((tile, 1), F32)
    carry = (neg, neg, zcol, zcol, jnp.zeros((tile, LANES), F32))
    carry = lax.fori_loop(0, qi, lambda kb, cr: step(kb, cr, False), carry)
    _, _, l0, l1, acc = step(qi, carry, True)
    o = acc / jnp.where(first, l0, l1)
    z_ref[0] = _sigmoid(gate_ref[0]) * o


def _fox_attn(q, k, v, bias_k, gate):
    bsz, t, d = q.shape
    tile = min(ATTN_TILE, t)
    nq = t // tile
    qspec = pl.BlockSpec((1, tile, LANES), lambda b, p, i: (b, i, p))
    kvspec = pl.BlockSpec((1, t, LANES), lambda b, p, i: (b, 0, p))
    bspec = pl.BlockSpec((1, 1, nq, 2, tile), lambda b, p, i: (b, p, 0, 0, 0))
    return pl.pallas_call(
        functools.partial(_fox_attn_kernel, tile),
        out_shape=jax.ShapeDtypeStruct((bsz, t, d), F32),
        grid=(bsz, PAIRS, nq),
        in_specs=[qspec, kvspec, kvspec, bspec, qspec],
        out_specs=qspec,
        compiler_params=_cparams(("parallel", "parallel", "arbitrary")),
        name="fox_attn",
    )(q, k, v, bias_k, gate)


def _fox_cache_kernel(nq, q_ref, kn_ref, vn_ref, ck_ref, cv_ref, bc_ref, bn_ref, gate_ref, z_ref,
                      m_scr, l_scr, acc_scr):
    kb = pl.program_id(1)
    d = D_MODEL
    rows = B_HEADS * nq
    q = q_ref[0]
    row_head = lax.broadcasted_iota(jnp.int32, (rows, d), 0) // nq
    col_head = lax.broadcasted_iota(jnp.int32, (rows, d), 1) // B_HEAD
    q_rep = jnp.concatenate([q] * B_HEADS, axis=0)
    q_bd = jnp.where(row_head == col_head, q_rep, jnp.zeros_like(q_rep))

    def expand(bias):
        n = bias.shape[-1]
        return jnp.broadcast_to(bias[:, None, :], (B_HEADS, nq, n)).reshape(rows, n)

    @pl.when(kb == 0)
    def _():
        m_scr[...] = jnp.full_like(m_scr, -jnp.inf)
        l_scr[...] = jnp.zeros_like(l_scr)
        acc_scr[...] = jnp.zeros_like(acc_scr)

    def update(s, v):
        m_prev = m_scr[...]
        m_new = jnp.maximum(m_prev, jnp.max(s, -1, keepdims=True))
        p = jnp.exp(s - m_new)
        alpha = jnp.exp(m_prev - m_new)
        l_scr[...] = alpha * l_scr[...] + jnp.sum(p, -1, keepdims=True)
        acc_scr[...] = alpha * acc_scr[...] + _dot(p, v)
        m_scr[...] = m_new

    update(_dot_nt(q_bd, ck_ref[0]) + expand(bc_ref[0]), cv_ref[0])

    @pl.when(kb == pl.num_programs(1) - 1)
    def _():
        s = _dot_nt(q_bd, kn_ref[0]) + expand(bn_ref[0])
        qpos = lax.broadcasted_iota(jnp.int32, (rows, nq), 0) % nq
        kpos = lax.broadcasted_iota(jnp.int32, (rows, nq), 1)
        update(jnp.where(kpos <= qpos, s, -jnp.inf), vn_ref[0])
        o_all = acc_scr[...] / l_scr[...]
        head_of_col = lax.broadcasted_iota(jnp.int32, (nq, d), 1) // B_HEAD
        o = jnp.zeros((nq, d), F32)
        for h in range(B_HEADS):
            o = o + jnp.where(head_of_col == h, o_all[h * nq:(h + 1) * nq, :], 0.0)
        z_ref[0] = _sigmoid(gate_ref[0]) * o


def _fox_cache_attn(q, kn, vn, ck, cv, bias_c, bias_n, gate):
    bsz, nq, d = q.shape
    plen = ck.shape[1]
    tk = min(CACHE_TILE, plen)
    rows = B_HEADS * nq
    new = pl.BlockSpec((1, nq, d), lambda b, j: (b, 0, 0))
    cache = pl.BlockSpec((1, tk, d), lambda b, j: (b, j, 0))
    return pl.pallas_call(
        functools.partial(_fox_cache_kernel, nq),
        out_shape=jax.ShapeDtypeStruct((bsz, nq, d), F32),
        grid=(bsz, plen // tk),
        in_specs=[new, new, new, cache, cache,
                  pl.BlockSpec((1, B_HEADS, tk), lambda b, j: (b, 0, j)),
                  pl.BlockSpec((1, B_HEADS, nq), lambda b, j: (b, 0, 0)), new],
        out_specs=new,
        scratch_shapes=[pltpu.VMEM((rows, 1), F32), pltpu.VMEM((rows, 1), F32),
                        pltpu.VMEM((rows, d), F32)],
        compiler_params=_cparams(("parallel", "arbitrary")),
        name="fox_cache_attn",
    )(q, kn, vn, ck, cv, bias_c, bias_n, gate)


def _ret_proj_kernel(x_ref, w_ref, cos_ref, sin_ref, q_o, k_o, v_o, g_o):
    d = D_MODEL
    xb = x_ref[...].astype(BF16)
    for h in range(C_HEADS):
        blk = slice(h * C_QK, (h + 1) * C_QK)
        cos = cos_ref[:, blk]
        sin = sin_ref[:, blk]
        q = _dot(xb, w_ref[:, h * C_QK:(h + 1) * C_QK])
        q_o[:, blk] = (q * cos + pltpu.roll(q, C_QK // 2, 1) * sin).astype(BF16)
        k = _dot(xb, w_ref[:, d + h * C_QK:d + (h + 1) * C_QK])
        k_o[:, blk] = ((k * cos + pltpu.roll(k, C_QK // 2, 1) * sin) * (C_QK ** -0.5)).astype(BF16)
    v_o[...] = _dot(xb, w_ref[:, 2 * d:4 * d]).astype(BF16)
    g_o[...] = _dot(xb, w_ref[:, 4 * d:6 * d])


def _ret_proj(x, w_in, cos, sin, table_blocks):
    m, d = x.shape
    tm = min(ROW_TILE, m)
    tab = pl.BlockSpec((tm, d), lambda i: (i % table_blocks, 0))
    return pl.pallas_call(
        _ret_proj_kernel,
        out_shape=[jax.ShapeDtypeStruct((m, d), BF16), jax.ShapeDtypeStruct((m, d), BF16),
                   jax.ShapeDtypeStruct((m, 2 * d), BF16), jax.ShapeDtypeStruct((m, 2 * d), F32)],
        grid=(m // tm,),
        in_specs=[_rows(tm, d), _resident((d, 6 * d)), tab, tab],
        out_specs=[_rows(tm, d), _rows(tm, d), _rows(tm, 2 * d), _rows(tm, 2 * d)],
        compiler_params=_cparams(("parallel",)),
        name="ret_proj",
    )(x, w_in, cos, sin)


def _ret_chunk_kernel(q_ref, k_ref, v_ref, g_ref, intra_ref, qd_ref, kd_ref, cd_ref, gng_ref, gnb_ref,
                      r0_ref, z_ref, rout_ref, r_scr):
    c = pl.program_id(2)

    @pl.when(c == 0)
    def _():
        r_scr[...] = r0_ref[0, 0]

    q = q_ref[0]
    k = k_ref[0]
    v = v_ref[0]
    state = r_scr[...]
    s = _dot_nt(q, k) * intra_ref[0]
    o = _dot(s, v) + _dot(q, state) * qd_ref[0]
    r_scr[...] = state * cd_ref[0] + _dot_tn(k.astype(F32) * kd_ref[0], v)
    mu = jnp.mean(o, -1, keepdims=True)
    oc = o - mu
    var = jnp.mean(oc * oc, -1, keepdims=True)
    on = oc * lax.rsqrt(var + LN_EPS) * gng_ref[...] + gnb_ref[...]
    g = g_ref[0]
    z_ref[0] = g * _sigmoid(g) * on

    @pl.when(c == pl.num_programs(2) - 1)
    def _():
        rout_ref[0, 0] = r_scr[...]


def _ret_chunk(q, k, v, g, gn_g, gn_b, r0):
    bsz, t, _ = q.shape
    L = min(RET_CHUNK, t)
    log_g = jnp.log1p(-jnp.exp2(-5.0 - jnp.arange(C_HEADS, dtype=F32)))
    idx = jnp.arange(L, dtype=F32)
    rel = idx[:, None] - idx[None, :]
    intra = jnp.where(rel[None] >= 0, jnp.exp(rel[None] * log_g[:, None, None]), 0.0)
    q_decay = jnp.exp((idx[None, :, None] + 1.0) * log_g[:, None, None])
    k_decay = jnp.exp((L - 1.0 - idx[None, :, None]) * log_g[:, None, None])
    chunk_decay = jnp.exp(L * log_g)[:, None, None]
    qk = pl.BlockSpec((1, L, C_QK), lambda b, h, c: (b, c, h))
    vg = pl.BlockSpec((1, L, C_V), lambda b, h, c: (b, c, h))
    per_head = lambda shape: pl.BlockSpec((1,) + shape, lambda b, h, c: (h, 0, 0))
    st = pl.BlockSpec((1, 1, C_QK, C_V), lambda b, h, c: (b, h, 0, 0))
    gn = pl.BlockSpec((1, C_V), lambda b, h, c: (0, h))
    return pl.pallas_call(
        _ret_chunk_kernel,
        out_shape=[jax.ShapeDtypeStruct((bsz, t, 2 * D_MODEL), F32),
                   jax.ShapeDtypeStruct((bsz, C_HEADS, C_QK, C_V), F32)],
        grid=(bsz, C_HEADS, t // L),
        in_specs=[qk, qk, vg, vg, per_head((L, L)), per_head((L, 1)), per_head((L, 1)),
                  per_head((1, 1)), gn, gn, st],
        out_specs=[vg, st],
        scratch_shapes=[pltpu.VMEM((C_QK, C_V), F32)],
        compiler_params=_cparams(("parallel", "parallel", "arbitrary")),
        name="ret_chunk",
    )(q, k, v, g, intra, q_decay, k_decay, chunk_decay, gn_g, gn_b, r0)


def _pad_cols(w, n):
    return jnp.pad(w, ((0, 0), (0, n - w.shape[1])))


def _pad_rows(w, n):
    return jnp.pad(w, ((0, n - w.shape[0]), (0, 0)))


def _row(vec):
    return vec.reshape(1, -1).astype(F32)


def _prep_rwkv(j, wt):
    def lora(w_in, w_out, width):
        return _pad_cols(w_in, width).astype(BF16), _pad_rows(w_out, width).astype(BF16)

    w1, w2 = lora(wt["a_w1"][j], wt["a_w2"][j], 128)
    a1, a2 = lora(wt["a_a1"][j], wt["a_a2"][j], 128)
    g1, g2 = lora(wt["a_g1"][j], wt["a_g2"][j], 256)
    out = dict(mu=wt["a_mu"][j], w_rkv=wt["a_w_rkv"][j].astype(BF16), w0=_row(wt["a_w0"][j]), w1=w1, w2=w2,
               a0=_row(wt["a_a0"][j]), a1=a1, a2=a2, g1=g1, g2=g2,
               k_k=_row(wt["a_k_k"][j]), k_a=_row(wt["a_k_a"][j]), r_k=_row(wt["a_r_k"][j]),
               gn_g=_row(wt["a_gn_g"][j]), gn_b=_row(wt["a_gn_b"][j]))
    if j > 0:
        v1, v2 = lora(wt["a_v1"][j - 1], wt["a_v2"][j - 1], 128)
        out.update(v0=_row(wt["a_v0"][j - 1]), v1=v1, v2=v2)
    return out


def _prep_fox(j, wt):
    d = D_MODEL
    w_in = wt["b_w_in"][j]
    return dict(w_in=w_in[:, :4 * d].astype(BF16), w_f=_pad_cols(w_in[:, 4 * d:], LANES).astype(BF16),
                b_f=_pad_cols(_row(wt["b_b_f"][j]), LANES),
                q_gain=_row(jnp.tile(wt["b_q_gain"][j], B_HEADS)),
                k_gain=_row(jnp.tile(wt["b_k_gain"][j], B_HEADS)))


def _rope_tables(pos):
    inv = ROPE_BASE ** (-jnp.arange(0, C_QK, 2, dtype=F32) / C_QK)
    ang = pos[:, None].astype(F32) * inv[None]
    cos, sin = jnp.cos(ang), jnp.sin(ang)
    cos_t = jnp.tile(jnp.concatenate([cos, cos], -1), (1, C_HEADS))
    sin_t = jnp.tile(jnp.concatenate([-sin, sin], -1), (1, C_HEADS))
    return cos_t, sin_t


def _run_trunk(x, ple, pos0, a_shift, a_wkv, b_cache, c_state, wt, prep):
    bsz, t, d = x.shape
    m = bsz * t
    v_first = None
    na_shift, na_wkv, nb_k, nb_v, nb_logf, nc_state = [], [], [], [], [], []
    for i in range(DEPTH):
        kind, j = i % N_MIXERS, i // N_MIXERS
        x2 = x.reshape(m, d)
        if kind == 0:
            w = prep["rwkv"][j]
            xprev = jnp.concatenate([a_shift[j][:, None, :], x[:, :-1]], axis=1).reshape(m, d)
            r, k, v, wl, a, g = _rwkv_proj(x2, xprev, v_first, w)
            if j == 0:
                v_first = v
            seq = lambda arr: arr.reshape(bsz, t, d)
            z, s_pairs = _rwkv_chunk(seq(r), seq(wl), seq(k), seq(v), seq(a), seq(g), w,
                                     _state_to_pairs(a_wkv[j].astype(F32)))
            na_shift.append(x[:, -1])
            na_wkv.append(_pairs_to_state(s_pairs))
            z = z.reshape(m, d)
            wo = prep["a_w_o"][j]
        elif kind == 1:
            w = prep["fox"][j]
            qb, kf, kb, vf, vb, gate, logf_pad = _fox_proj(x2, w)
            logf = logf_pad[:, :B_HEADS].reshape(bsz, t, B_HEADS)
            c_new = jnp.cumsum(logf, axis=1)
            seq = lambda arr: arr.reshape(bsz, t, d)
            if b_cache is None:
                tile = min(ATTN_TILE, t)
                bias_k = (-c_new).reshape(bsz, t // tile, tile, PAIRS, 2).transpose(0, 3, 1, 4, 2)
                z = _fox_attn(seq(qb), seq(kb), seq(vb), bias_k, seq(gate))
            else:
                ck, cv, clogf = b_cache[0][j], b_cache[1][j], b_cache[2][j]
                plen = ck.shape[1]
                c_cache = jnp.cumsum(clogf.astype(F32), axis=1)
                bias_c = jnp.swapaxes(c_cache[:, -1:, :] - c_cache, 1, 2)
                bias_n = jnp.swapaxes(-c_new, 1, 2)
                z = _fox_cache_attn(seq(qb), seq(kb), seq(vb), ck.reshape(bsz, plen, d),
                                    cv.reshape(bsz, plen, d), bias_c, bias_n, seq(gate))
            nb_k.append(kf.reshape(bsz, t, B_HEADS, B_HEAD))
            nb_v.append(vf.reshape(bsz, t, B_HEADS, B_HEAD))
            nb_logf.append(logf)
            z = z.reshape(m, d)
            wo = prep["b_w_o"][j]
        else:
            cos, sin = _rope_tables(pos0 + jnp.arange(t))
            tm = min(ROW_TILE, m)
            if t >= tm:
                table_blocks = t // tm
            else:
                cos, sin = jnp.tile(cos, (tm // t, 1)), jnp.tile(sin, (tm // t, 1))
                table_blocks = 1
            q, k, v, g = _ret_proj(x2, prep["c_w_in"][j], cos, sin, table_blocks)
            z, r_new = _ret_chunk(q.reshape(bsz, t, d), k.reshape(bsz, t, d), v.reshape(bsz, t, 2 * d),
                                  g.reshape(bsz, t, 2 * d), _row(wt["c_gn_g"][j]), _row(wt["c_gn_b"][j]),
                                  c_state[j].astype(F32))
            nc_state.append(r_new)
            z = z.reshape(m, 2 * d)
            wo = prep["c_w_o"][j]
        x = _post_mixer(z, x2, ple[i].reshape(m, D_PLE), wo,
                        _row(wt["ln_mix_g"][i]), _row(wt["ln_mix_b"][i]), prep["ffn_w1"][i], prep["ffn_w2"][i],
                        _row(wt["ln_ffn_g"][i]), _row(wt["ln_ffn_b"][i]), prep["ple_gate"][i],
                        prep["ple_in"][i]).reshape(bsz, t, d)
    return (x, jnp.stack(na_shift), jnp.stack(na_wkv), jnp.stack(nb_k), jnp.stack(nb_v),
            jnp.stack(nb_logf), jnp.stack(nc_state))


def kernel(x_prompt, x_sample, p_prompt, p_sample, state_a_shift, state_a_wkv, cache_b_k, cache_b_v, cache_b_logf, state_c, ln_mix_g, ln_mix_b, ln_ffn_g, ln_ffn_b, ffn_w1, ffn_w2, ple_in, ple_gate, a_mu, a_w_rkv, a_w0, a_w1, a_w2, a_a0, a_a1, a_a2, a_v0, a_v1, a_v2, a_g1, a_g2, a_k_k, a_k_a, a_r_k, a_gn_g, a_gn_b, a_w_o, b_w_in, b_b_f, b_q_gain, b_k_gain, b_w_o, c_w_in, c_gn_g, c_gn_b, c_w_o):
    wt = dict(ln_mix_g=ln_mix_g, ln_mix_b=ln_mix_b, ln_ffn_g=ln_ffn_g, ln_ffn_b=ln_ffn_b,
              a_mu=a_mu, a_w_rkv=a_w_rkv, a_w0=a_w0, a_w1=a_w1, a_w2=a_w2, a_a0=a_a0, a_a1=a_a1, a_a2=a_a2,
              a_v0=a_v0, a_v1=a_v1, a_v2=a_v2, a_g1=a_g1, a_g2=a_g2, a_k_k=a_k_k, a_k_a=a_k_a, a_r_k=a_r_k,
              a_gn_g=a_gn_g, a_gn_b=a_gn_b, b_w_in=b_w_in, b_b_f=b_b_f, b_q_gain=b_q_gain,
              b_k_gain=b_k_gain, c_gn_g=c_gn_g, c_gn_b=c_gn_b)
    n_a, n_b, n_c = a_mu.shape[0], b_w_in.shape[0], c_w_in.shape[0]
    prep = dict(
        rwkv=[_prep_rwkv(j, wt) for j in range(n_a)],
        fox=[_prep_fox(j, wt) for j in range(n_b)],
        a_w_o=a_w_o.astype(BF16), b_w_o=b_w_o.astype(BF16), c_w_o=c_w_o.astype(BF16),
        c_w_in=c_w_in.astype(BF16), ffn_w1=ffn_w1.astype(BF16), ffn_w2=ffn_w2.astype(BF16),
        ple_gate=ple_gate.astype(BF16), ple_in=ple_in.astype(BF16))
    nbat = x_prompt.shape[0]
    zero_shift = jnp.zeros((n_a, nbat, D_MODEL), x_prompt.dtype)
    zero_wkv = jnp.zeros((n_a, nbat, A_HEADS, A_HEAD, A_HEAD), F32)
    zero_ret = jnp.zeros((n_c, nbat, C_HEADS, C_QK, C_V), F32)
    past_len = cache_b_k.shape[2]
    y_prompt, pa_shift, pa_wkv, pb_k, pb_v, pb_logf, pc_state = _run_trunk(
        x_prompt, p_prompt, 0, zero_shift, zero_wkv, None, zero_ret, wt, prep)
    y_sample, sa_shift, sa_wkv, sb_k, sb_v, sb_logf, sc_state = _run_trunk(
        x_sample, p_sample, past_len, state_a_shift, state_a_wkv, (cache_b_k, cache_b_v, cache_b_logf),
        state_c, wt, prep)
    return (y_prompt, y_sample, pa_shift, pa_wkv, pb_k, pb_v, pb_logf, pc_state,
            sa_shift, sa_wkv, sb_k, sb_v, sb_logf, sc_state)
```

```python
import functools
import math

import jax
import jax.numpy as jnp
from jax import lax
from jax.experimental import pallas as pl
from jax.experimental.pallas import tpu as pltpu

F32 = jnp.float32
BF16 = jnp.bfloat16

D_MODEL = 1024
DEPTH = 4
N_MIXERS = 3
D_PLE = 256
D_FF = 4 * D_MODEL
DN_ALPHA = (2.0 * DEPTH) ** 0.25
LN_EPS = 1e-5
A_HEAD = 64
A_HEADS = D_MODEL // A_HEAD
A_GN_EPS = 64e-5
B_HEAD = 64
B_HEADS = D_MODEL // B_HEAD
C_HEADS = 8
C_QK = D_MODEL // C_HEADS
C_V = 2 * D_MODEL // C_HEADS
ROPE_BASE = 10000.0
RMS_EPS = 1e-6

LANES = 128
PAIRS = D_MODEL // LANES
VMEM_LIMIT = 56 * 1024 * 1024
ROW_TILE = 512
FFN_CHUNK = 1024
RWKV_CHUNK = 64
RET_CHUNK = 256
ATTN_TILE = 512
CACHE_TILE = 1024


def _cparams(semantics):
    return pltpu.CompilerParams(dimension_semantics=semantics, vmem_limit_bytes=VMEM_LIMIT)


def _resident(shape):
    nd = len(shape)
    return pl.BlockSpec(shape, lambda *_: (0,) * nd, pipeline_mode=pl.Buffered(1))


def _rows(tm, width):
    return pl.BlockSpec((tm, width), lambda i: (i, 0))


def _dot(a, b):
    return jnp.dot(a.astype(BF16), b.astype(BF16), preferred_element_type=F32)


def _dot_nt(a, b):
    return lax.dot_general(a.astype(BF16), b.astype(BF16), (((1,), (1,)), ((), ())),
                           preferred_element_type=F32)


def _dot_tn(a, b):
    return lax.dot_general(a.astype(BF16), b.astype(BF16), (((0,), (0,)), ((), ())),
                           preferred_element_type=F32)


def _sigmoid(x):
    return 1.0 / (1.0 + jnp.exp(-x))


def _softplus(x):
    return jnp.maximum(x, 0.0) + jnp.log(1.0 + jnp.exp(-jnp.abs(x)))


def _layer_norm(x, g, b):
    mu = jnp.mean(x, -1, keepdims=True)
    xc = x - mu
    var = jnp.mean(xc * xc, -1, keepdims=True)
    return xc * lax.rsqrt(var + LN_EPS) * g + b


def _pair_sum(x, first):
    s0 = jnp.sum(jnp.where(first, x, 0.0), -1, keepdims=True)
    s1 = jnp.sum(jnp.where(first, 0.0, x), -1, keepdims=True)
    return jnp.where(first, s0, s1)


def _post_kernel(z_ref, x_ref, p_ref, wo_ref, g1_ref, b1_ref, w1_ref, w2_ref, g2_ref, b2_ref,
                 wg_ref, wp_ref, o_ref):
    x = x_ref[...]
    x1 = _layer_norm(DN_ALPHA * x + _dot(z_ref[...], wo_ref[...]), g1_ref[...], b1_ref[...])
    x1b = x1.astype(BF16)
    acc = jnp.zeros_like(x1)
    for f in range(D_FF // FFN_CHUNK):
        cols = slice(f * FFN_CHUNK, (f + 1) * FFN_CHUNK)
        h = jnp.maximum(_dot(x1b, w1_ref[:, cols]), 0.0)
        acc = acc + _dot(h * h, w2_ref[cols, :])
    x2 = _layer_norm(DN_ALPHA * x1 + acc, g2_ref[...], b2_ref[...])
    gate = _sigmoid(_dot(x2, wg_ref[...]))
    o_ref[...] = x2 + gate * _dot(p_ref[...], wp_ref[...])


def _post_mixer(z, x, p, wo, g1, b1, w1, w2, g2, b2, wg, wp):
    m, dz = z.shape
    tm = min(ROW_TILE, m)
    d = D_MODEL
    return pl.pallas_call(
        _post_kernel,
        out_shape=jax.ShapeDtypeStruct((m, d), F32),
        grid=(m // tm,),
        in_specs=[_rows(tm, dz), _rows(tm, d), _rows(tm, D_PLE), _resident((dz, d)),
                  _resident((1, d)), _resident((1, d)), _resident((d, D_FF)), _resident((D_FF, d)),
                  _resident((1, d)), _resident((1, d)), _resident((d, d)), _resident((D_PLE, d))],
        out_specs=_rows(tm, d),
        compiler_params=_cparams(("parallel",)),
        name="post_mixer",
    )(z, x, p, wo, g1, b1, w1, w2, g2, b2, wg, wp)


def _rwkv_proj_kernel(has_vres, *refs):
    if has_vres:
        (x_ref, xp_ref, vf_ref, mu_ref, wrkv_ref, w0_ref, w1_ref, w2_ref, a0_ref, a1_ref, a2_ref,
         g1_ref, g2_ref, v0_ref, v1_ref, v2_ref, r_o, k_o, v_o, wl_o, a_o, g_o) = refs
    else:
        (x_ref, xp_ref, mu_ref, wrkv_ref, w0_ref, w1_ref, w2_ref, a0_ref, a1_ref, a2_ref,
         g1_ref, g2_ref, r_o, k_o, v_o, wl_o, a_o, g_o) = refs
    x = x_ref[...]
    dx = xp_ref[...] - x
    xr, xw, xk, xv, xa, xg = [(x + dx * mu_ref[i:i + 1, :]).astype(BF16) for i in range(6)]
    r_o[...] = _dot(xr, wrkv_ref[0])
    k_o[...] = _dot(xk, wrkv_ref[1])
    v = _dot(xv, wrkv_ref[2])
    if has_vres:
        mix = _sigmoid(v0_ref[...] + _dot(_dot(xv, v1_ref[...]), v2_ref[...]))
        v = v + (vf_ref[...] - v) * mix
    v_o[...] = v
    pre = w0_ref[...] + _dot(jnp.tanh(_dot(xw, w1_ref[...])), w2_ref[...])
    w_log = -_softplus(-pre) - 0.5
    wl_o[...] = -jnp.exp(w_log)
    a_o[...] = _sigmoid(a0_ref[...] + _dot(_dot(xa, a1_ref[...]), a2_ref[...]))
    g_o[...] = _dot(_sigmoid(_dot(xg, g1_ref[...])), g2_ref[...])


def _rwkv_proj(x, xprev, vfirst, w):
    m, d = x.shape
    tm = min(ROW_TILE, m)
    has_vres = vfirst is not None
    args = [x, xprev] + ([vfirst] if has_vres else []) + [
        w["mu"], w["w_rkv"], w["w0"], w["w1"], w["w2"], w["a0"], w["a1"], w["a2"], w["g1"], w["g2"]]
    if has_vres:
        args += [w["v0"], w["v1"], w["v2"]]
    n_rows = 3 if has_vres else 2
    in_specs = [_rows(tm, d)] * n_rows + [_resident(a.shape) for a in args[n_rows:]]
    return pl.pallas_call(
        functools.partial(_rwkv_proj_kernel, has_vres),
        out_shape=[jax.ShapeDtypeStruct((m, d), F32)] * 6,
        grid=(m // tm,),
        in_specs=in_specs,
        out_specs=[_rows(tm, d)] * 6,
        compiler_params=_cparams(("parallel",)),
        name="rwkv_proj",
    )(*args)


def _split3(x):
    hi = x.astype(BF16)
    r1 = x - hi.astype(F32)
    mid = r1.astype(BF16)
    lo = (r1 - mid.astype(F32)).astype(BF16)
    return hi, mid, lo


def _rwkv_chunk_kernel(L, r_ref, wl_ref, k_ref, v_ref, a_ref, g_ref, kk_ref, ka_ref, rk_ref,
                       gng_ref, gnb_ref, s0_ref, z_ref, sout_ref, s_scr):
    c = pl.program_id(1)

    @pl.when(c == 0)
    def _():
        s_scr[...] = s0_ref[0]

    L2 = 2 * L
    first = lax.broadcasted_iota(jnp.int32, (L, LANES), 1) < A_HEAD
    row = lax.broadcasted_iota(jnp.int32, (L2, L2), 0)
    col = lax.broadcasted_iota(jnp.int32, (L2, L2), 1)
    same_head = (row >= L) == (col >= L)
    strict = same_head & (row > col)
    incl = same_head & (row >= col)
    eye = jnp.where(row == col, 1.0, 0.0).astype(F32)
    tri = jnp.where(lax.broadcasted_iota(jnp.int32, (L, L), 0) >=
                    lax.broadcasted_iota(jnp.int32, (L, L), 1), 1.0, 0.0).astype(BF16)

    def stack(x):
        return jnp.concatenate([jnp.where(first, x, 0.0), jnp.where(first, 0.0, x)], axis=0).astype(BF16)

    pairs = range(PAIRS)
    sls = [slice(p * LANES, (p + 1) * LANES) for p in pairs]
    r = [r_ref[0, :, sl] for sl in sls]
    wl = [wl_ref[0, :, sl] for sl in sls]
    v = [v_ref[0, :, sl] for sl in sls]
    a = [a_ref[0, :, sl] for sl in sls]
    k_raw = [k_ref[0, :, sl] for sl in sls]
    kk = [k_raw[p] * kk_ref[:, sls[p]] for p in pairs]
    kk = [x / jnp.maximum(jnp.sqrt(_pair_sum(x * x, first)), 1e-12) for x in kk]
    k = [k_raw[p] * (1.0 + (a[p] - 1.0) * ka_ref[:, sls[p]]) for p in pairs]
    b = [kk[p] * a[p] for p in pairs]

    def cumsum(x):
        hi, mid, lo = _split3(x)
        return (jnp.dot(tri, hi, preferred_element_type=F32) + jnp.dot(tri, mid, preferred_element_type=F32)
                + jnp.dot(tri, lo, preferred_element_type=F32))

    cum = [cumsum(x) for x in wl]
    g_t = [jnp.exp(x) for x in cum]
    g_prev = [jnp.exp(cum[p] - wl[p]) for p in pairs]
    g_inv = [jnp.exp(-x) for x in cum]
    g_end = [x[L - 1:L, :] for x in g_t]
    kinv = [k[p] * g_inv[p] for p in pairs]
    binv = [b[p] * g_inv[p] for p in pairs]
    lhs = [jnp.concatenate([stack(kk[p] * g_prev[p]), stack(r[p] * g_t[p])], axis=0) for p in pairs]
    rhs = [jnp.concatenate([stack(kinv[p]), stack(binv[p])], axis=0) for p in pairs]
    s_v = [stack(x) for x in v]
    gram = [_dot_nt(lhs[p], rhs[p]) for p in pairs]
    m_k = [jnp.where(strict, x[:L2, :L2], 0.0).astype(BF16) for x in gram]
    n_k = [jnp.where(incl, x[L2:, :L2], 0.0).astype(BF16) for x in gram]
    n_b = [jnp.where(incl, x[L2:, L2:], 0.0) for x in gram]
    x_pow = [jnp.where(strict, -x[:L2, L2:], 0.0) for x in gram]
    t_inv = [eye + x for x in x_pow]
    n = 1
    while 2 * n < L:
        x_pow = [_dot(x, x) for x in x_pow]
        t_inv = [t_inv[p] + _dot(t_inv[p], x_pow[p]) for p in pairs]
        n *= 2
    s_prev = [s_scr[p] for p in pairs]
    from_state = [_dot_nt(lhs[p], s_prev[p]) for p in pairs]
    u = [_dot(t_inv[p], from_state[p][:L2] + _dot(m_k[p], s_v[p])).astype(BF16) for p in pairs]
    vu = [jnp.concatenate([s_v[p], u[p]], axis=0) for p in pairs]
    y2 = [from_state[p][L2:] + _dot(jnp.concatenate([n_k[p], (-n_b[p]).astype(BF16)], axis=1), vu[p])
          for p in pairs]
    y = [x[:L] + x[L:] for x in y2]
    ends = [jnp.concatenate([stack(kinv[p] * g_end[p]), stack(-(binv[p] * g_end[p]))], axis=0) for p in pairs]
    for p in pairs:
        s_scr[p] = s_prev[p] * g_end[p] + _dot_tn(vu[p], ends[p])
    for p in pairs:
        sl = sls[p]
        mu = _pair_sum(y[p], first) * (1.0 / A_HEAD)
        yc = y[p] - mu
        var = _pair_sum(yc * yc, first) * (1.0 / A_HEAD)
        yn = yc * lax.rsqrt(var + A_GN_EPS) * gng_ref[:, sl] + gnb_ref[:, sl]
        bonus = _pair_sum(r[p] * k[p] * rk_ref[:, sl], first) * v[p]
        z_ref[0, :, sl] = (yn + bonus) * g_ref[0, :, sl]

    @pl.when(c == pl.num_programs(1) - 1)
    def _():
        sout_ref[0] = s_scr[...]


def _rwkv_chunk(r, wl, k, v, a, g, w, s0_pairs):
    bsz, t, d = r.shape
    L = min(RWKV_CHUNK, t)
    seq = pl.BlockSpec((1, L, d), lambda b, c: (b, c, 0))
    vec = pl.BlockSpec((1, d), lambda b, c: (0, 0))
    st = pl.BlockSpec((1, PAIRS, LANES, LANES), lambda b, c: (b, 0, 0, 0))
    return pl.pallas_call(
        functools.partial(_rwkv_chunk_kernel, L),
        out_shape=[jax.ShapeDtypeStruct((bsz, t, d), F32),
                   jax.ShapeDtypeStruct((bsz, PAIRS, LANES, LANES), F32)],
        grid=(bsz, t // L),
        in_specs=[seq] * 6 + [vec] * 5 + [st],
        out_specs=[seq, st],
        scratch_shapes=[pltpu.VMEM((PAIRS, LANES, LANES), F32)],
        compiler_params=_cparams(("parallel", "arbitrary")),
        name="rwkv_chunk",
    )(r, wl, k, v, a, g, w["k_k"], w["k_a"], w["r_k"], w["gn_g"], w["gn_b"], s0_pairs)


def _state_to_pairs(s):
    bsz = s.shape[0]
    s = s.reshape(bsz, PAIRS, 2, A_HEAD, A_HEAD)
    z = jnp.zeros_like(s[:, :, 0])
    top = jnp.concatenate([s[:, :, 0], z], axis=-1)
    bot = jnp.concatenate([z, s[:, :, 1]], axis=-1)
    return jnp.concatenate([top, bot], axis=-2)


def _pairs_to_state(sp):
    bsz = sp.shape[0]
    s0 = sp[:, :, :A_HEAD, :A_HEAD]
    s1 = sp[:, :, A_HEAD:, A_HEAD:]
    return jnp.stack([s0, s1], axis=2).reshape(bsz, A_HEADS, A_HEAD, A_HEAD)


def _fox_proj_kernel(x_ref, w_ref, wf_ref, bf_ref, qg_ref, kg_ref,
                     qb_o, kf_o, kb_o, vf_o, vb_o, gate_o, logf_o):
    d = D_MODEL
    xb = x_ref[...].astype(BF16)
    first = lax.broadcasted_iota(jnp.int32, (xb.shape[0], LANES), 1) < B_HEAD

    def rms(t, gain_ref, blk):
        ms = _pair_sum(t * t, first) * (1.0 / B_HEAD)
        return t * lax.rsqrt(ms + RMS_EPS) * gain_ref[:, blk]

    for j in range(PAIRS):
        blk = slice(j * LANES, (j + 1) * LANES)
        q = rms(_dot(xb, w_ref[:, j * LANES:(j + 1) * LANES]), qg_ref, blk)
        qb_o[:, blk] = (q * (B_HEAD ** -0.5)).astype(BF16)
        k = rms(_dot(xb, w_ref[:, d + j * LANES:d + (j + 1) * LANES]), kg_ref, blk)
        kf_o[:, blk] = k
        kb_o[:, blk] = k.astype(BF16)
    v = _dot(xb, w_ref[:, 2 * d:3 * d])
    vf_o[...] = v
    vb_o[...] = v.astype(BF16)
    gate_o[...] = _dot(xb, w_ref[:, 3 * d:4 * d])
    logf_o[...] = -_softplus(-(_dot(xb, wf_ref[...]) + bf_ref[...]))


def _fox_proj(x, w):
    m, d = x.shape
    tm = min(ROW_TILE, m)
    outs = [jax.ShapeDtypeStruct((m, d), BF16), jax.ShapeDtypeStruct((m, d), F32),
            jax.ShapeDtypeStruct((m, d), BF16), jax.ShapeDtypeStruct((m, d), F32),
            jax.ShapeDtypeStruct((m, d), BF16), jax.ShapeDtypeStruct((m, d), F32),
            jax.ShapeDtypeStruct((m, LANES), F32)]
    return pl.pallas_call(
        _fox_proj_kernel,
        out_shape=outs,
        grid=(m // tm,),
        in_specs=[_rows(tm, d), _resident((d, 4 * d)), _resident((d, LANES)), _resident((1, LANES)),
                  _resident((1, d)), _resident((1, d))],
        out_specs=[_rows(tm, d)] * 6 + [_rows(tm, LANES)],
        compiler_params=_cparams(("parallel",)),
        name="fox_proj",
    )(x, w["w_in"], w["w_f"], w["b_f"], w["q_gain"], w["k_gain"])


def _fox_attn_kernel(tile, q_ref, k_ref, v_ref, bk_ref, gate_ref, z_ref):
    qi = pl.program_id(2)
    q = q_ref[0]
    first = lax.broadcasted_iota(jnp.int32, (tile, LANES), 1) < B_HEAD
    zero = jnp.zeros_like(q)
    q_heads = (jnp.where(first, q, zero), jnp.where(first, zero, q))
    causal = (lax.broadcasted_iota(jnp.int32, (tile, tile), 1) <=
              lax.broadcasted_iota(jnp.int32, (tile, tile), 0))

    def step(kb, carry, diagonal):
        m0, m1, l0, l1, acc = carry
        start = pl.multiple_of(kb * tile, tile)
        k = k_ref[0, pl.ds(start, tile), :]
        v = v_ref[0, pl.ds(start, tile), :]
        bias = bk_ref[0, 0, kb]
        new_m, new_l, alphas, pvs = [], [], [], []
        for h, (m_prev, l_prev) in enumerate(((m0, l0), (m1, l1))):
            s = _dot_nt(q_heads[h], k) + bias[h:h + 1, :]
            if diagonal:
                s = jnp.where(causal, s, -jnp.inf)
            m_new = jnp.maximum(m_prev, jnp.max(s, -1, keepdims=True))
            p = jnp.exp(s - m_new)
            alpha = jnp.exp(m_prev - m_new)
            new_m.append(m_new)
            new_l.append(alpha * l_prev + jnp.sum(p, -1, keepdims=True))
            alphas.append(alpha)
            pvs.append(_dot(p, v))
        acc = jnp.where(first, alphas[0], alphas[1]) * acc + jnp.where(first, pvs[0], pvs[1])
        return new_m[0], new_m[1], new_l[0], new_l[1], acc

    neg = jnp.full((tile, 1), -jnp.inf, F32)
    zcol = jnp.zeros((tile, 1), F32)
    carry = (neg, neg, zcol, zcol, jnp.zeros((tile, LANES), F32))
    carry = lax.fori_loop(0, qi, lambda kb, cr: step(kb, cr, False), carry)
    _, _, l0, l1, acc = step(qi, carry, True)
    o = acc / jnp.where(first, l0, l1)
    z_ref[0] = _sigmoid(gate_ref[0]) * o


def _fox_attn(q, k, v, bias_k, gate):
    bsz, t, d = q.shape
    tile = min(ATTN_TILE, t)
    nq = t // tile
    qspec = pl.BlockSpec((1, tile, LANES), lambda b, p, i: (b, i, p))
    kvspec = pl.BlockSpec((1, t, LANES), lambda b, p, i: (b, 0, p))
    bspec = pl.BlockSpec((1, 1, nq, 2, tile), lambda b, p, i: (b, p, 0, 0, 0))
    return pl.pallas_call(
        functools.partial(_fox_attn_kernel, tile),
        out_shape=jax.ShapeDtypeStruct((bsz, t, d), F32),
        grid=(bsz, PAIRS, nq),
        in_specs=[qspec, kvspec, kvspec, bspec, qspec],
        out_specs=qspec,
        compiler_params=_cparams(("parallel", "parallel", "arbitrary")),
        name="fox_attn",
    )(q, k, v, bias_k, gate)


def _fox_cache_kernel(nq, q_ref, kn_ref, vn_ref, ck_ref, cv_ref, lf_ref, cn_ref, gate_ref, z_ref,
                      m_scr, l_scr, acc_scr, carry_scr):
    kb = pl.program_id(1)
    d = D_MODEL
    rows = B_HEADS * nq
    q = q_ref[0]
    row_head = lax.broadcasted_iota(jnp.int32, (rows, d), 0) // nq
    col_head = lax.broadcasted_iota(jnp.int32, (rows, d), 1) // B_HEAD
    q_rep = jnp.concatenate([q] * B_HEADS, axis=0)
    q_bd = jnp.where(row_head == col_head, q_rep, jnp.zeros_like(q_rep))

    def expand(bias):
        n = bias.shape[-1]
        return jnp.broadcast_to(bias[:, None, :], (B_HEADS, nq, n)).reshape(rows, n)

    @pl.when(kb == 0)
    def _():
        m_scr[...] = jnp.full_like(m_scr, -jnp.inf)
        l_scr[...] = jnp.zeros_like(l_scr)
        acc_scr[...] = jnp.zeros_like(acc_scr)
        carry_scr[...] = jnp.zeros_like(carry_scr)

    def update(s, v):
        m_prev = m_scr[...]
        m_new = jnp.maximum(m_prev, jnp.max(s, -1, keepdims=True))
        p = jnp.exp(s - m_new)
        alpha = jnp.exp(m_prev - m_new)
        l_scr[...] = alpha * l_scr[...] + jnp.sum(p, -1, keepdims=True)
        acc_scr[...] = alpha * acc_scr[...] + _dot(p, v)
        m_scr[...] = m_new

    cum = lf_ref[0]
    tk = cum.shape[-1]
    key = lax.broadcasted_iota(jnp.int32, cum.shape, 1)
    shift = 1
    while shift < tk:
        cum = cum + jnp.where(key >= shift, pltpu.roll(cum, shift, 1), 0.0)
        shift *= 2
    cum = cum + carry_scr[...]
    carry_scr[...] = cum[:, tk - 1:tk]
    update(_dot_nt(q_bd, ck_ref[0]) - expand(cum), cv_ref[0])

    @pl.when(kb == pl.num_programs(1) - 1)
    def _():
        s = _dot_nt(q_bd, kn_ref[0]) - expand(carry_scr[...] + cn_ref[0])
        qpos = lax.broadcasted_iota(jnp.int32, (rows, nq), 0) % nq
        kpos = lax.broadcasted_iota(jnp.int32, (rows, nq), 1)
        update(jnp.where(kpos <= qpos, s, -jnp.inf), vn_ref[0])
        o_all = acc_scr[...] / l_scr[...]
        head_of_col = lax.broadcasted_iota(jnp.int32, (nq, d), 1) // B_HEAD
        o = jnp.zeros((nq, d), F32)
        for h in range(B_HEADS):
            o = o + jnp.where(head_of_col == h, o_all[h * nq:(h + 1) * nq, :], 0.0)
        z_ref[0] = _sigmoid(gate_ref[0]) * o


def _fox_cache_attn(q, kn, vn, ck, cv, logf_cache, c_new, gate):
    bsz, nq, d = q.shape
    plen = ck.shape[1]
    tk = min(CACHE_TILE, plen)
    rows = B_HEADS * nq
    new = pl.BlockSpec((1, nq, d), lambda b, j: (b, 0, 0))
    cache = pl.BlockSpec((1, tk, d), lambda b, j: (b, j, 0))
    return pl.pallas_call(
        functools.partial(_fox_cache_kernel, nq),
        out_shape=jax.ShapeDtypeStruct((bsz, nq, d), F32),
        grid=(bsz, plen // tk),
        in_specs=[new, new, new, cache, cache,
                  pl.BlockSpec((1, B_HEADS, tk), lambda b, j: (b, 0, j)),
                  pl.BlockSpec((1, B_HEADS, nq), lambda b, j: (b, 0, 0)), new],
        out_specs=new,
        scratch_shapes=[pltpu.VMEM((rows, 1), F32), pltpu.VMEM((rows, 1), F32),
                        pltpu.VMEM((rows, d), F32), pltpu.VMEM((B_HEADS, 1), F32)],
        compiler_params=_cparams(("parallel", "arbitrary")),
        name="fox_cache_attn",
    )(q, kn, vn, ck, cv, logf_cache, c_new, gate)


def _ret_proj_kernel(x_ref, w_ref, cos_ref, sin_ref, q_o, k_o, v_o, g_o):
    d = D_MODEL
    xb = x_ref[...].astype(BF16)
    for h in range(C_HEADS):
        blk = slice(h * C_QK, (h + 1) * C_QK)
        cos = cos_ref[:, blk]
        sin = sin_ref[:, blk]
        q = _dot(xb, w_ref[:, h * C_QK:(h + 1) * C_QK])
        q_o[:, blk] = (q * cos + pltpu.roll(q, C_QK // 2, 1) * sin).astype(BF16)
        k = _dot(xb, w_ref[:, d + h * C_QK:d + (h + 1) * C_QK])
        k_o[:, blk] = ((k * cos + pltpu.roll(k, C_QK // 2, 1) * sin) * (C_QK ** -0.5)).astype(BF16)
    v_o[...] = _dot(xb, w_ref[:, 2 * d:4 * d]).astype(BF16)
    g_o[...] = _dot(xb, w_ref[:, 4 * d:6 * d])


def _ret_proj(x, w_in, cos, sin, table_blocks):
    m, d = x.shape
    tm = min(ROW_TILE, m)
    tab = pl.BlockSpec((tm, d), lambda i: (i % table_blocks, 0))
    return pl.pallas_call(
        _ret_proj_kernel,
        out_shape=[jax.ShapeDtypeStruct((m, d), BF16), jax.ShapeDtypeStruct((m, d), BF16),
                   jax.ShapeDtypeStruct((m, 2 * d), BF16), jax.ShapeDtypeStruct((m, 2 * d), F32)],
        grid=(m // tm,),
        in_specs=[_rows(tm, d), _resident((d, 6 * d)), tab, tab],
        out_specs=[_rows(tm, d), _rows(tm, d), _rows(tm, 2 * d), _rows(tm, 2 * d)],
        compiler_params=_cparams(("parallel",)),
        name="ret_proj",
    )(x, w_in, cos, sin)


def _ret_chunk_kernel(q_ref, k_ref, v_ref, g_ref, intra_ref, qd_ref, kd_ref, cd_ref, gng_ref, gnb_ref,
                      r0_ref, z_ref, rout_ref, r_scr):
    c = pl.program_id(2)

    @pl.when(c == 0)
    def _():
        r_scr[...] = r0_ref[0, 0]

    q = q_ref[0]
    k = k_ref[0]
    v = v_ref[0]
    state = r_scr[...]
    s = _dot_nt(q, k) * intra_ref[0]
    o = _dot(s, v) + _dot(q, state) * qd_ref[0]
    r_scr[...] = state * cd_ref[0] + _dot_tn(k.astype(F32) * kd_ref[0], v)
    mu = jnp.mean(o, -1, keepdims=True)
    oc = o - mu
    var = jnp.mean(oc * oc, -1, keepdims=True)
    on = oc * lax.rsqrt(var + LN_EPS) * gng_ref[...] + gnb_ref[...]
    g = g_ref[0]
    z_ref[0] = g * _sigmoid(g) * on

    @pl.when(c == pl.num_programs(2) - 1)
    def _():
        rout_ref[0, 0] = r_scr[...]


def _ret_chunk(q, k, v, g, gn_g, gn_b, r0):
    bsz, t, _ = q.shape
    L = min(RET_CHUNK, t)
    log_g = jnp.log1p(-jnp.exp2(-5.0 - jnp.arange(C_HEADS, dtype=F32)))
    idx = jnp.arange(L, dtype=F32)
    rel = idx[:, None] - idx[None, :]
    intra = jnp.where(rel[None] >= 0, jnp.exp(rel[None] * log_g[:, None, None]), 0.0)
    q_decay = jnp.exp((idx[None, :, None] + 1.0) * log_g[:, None, None])
    k_decay = jnp.exp((L - 1.0 - idx[None, :, None]) * log_g[:, None, None])
    chunk_decay = jnp.exp(L * log_g)[:, None, None]
    qk = pl.BlockSpec((1, L, C_QK), lambda b, h, c: (b, c, h))
    vg = pl.BlockSpec((1, L, C_V), lambda b, h, c: (b, c, h))
    per_head = lambda shape: pl.BlockSpec((1,) + shape, lambda b, h, c: (h, 0, 0))
    st = pl.BlockSpec((1, 1, C_QK, C_V), lambda b, h, c: (b, h, 0, 0))
    gn = pl.BlockSpec((1, C_V), lambda b, h, c: (0, h))
    return pl.pallas_call(
        _ret_chunk_kernel,
        out_shape=[jax.ShapeDtypeStruct((bsz, t, 2 * D_MODEL), F32),
                   jax.ShapeDtypeStruct((bsz, C_HEADS, C_QK, C_V), F32)],
        grid=(bsz, C_HEADS, t // L),
        in_specs=[qk, qk, vg, vg, per_head((L, L)), per_head((L, 1)), per_head((L, 1)),
                  per_head((1, 1)), gn, gn, st],
        out_specs=[vg, st],
        scratch_shapes=[pltpu.VMEM((C_QK, C_V), F32)],
        compiler_params=_cparams(("parallel", "parallel", "arbitrary")),
        name="ret_chunk",
    )(q, k, v, g, intra, q_decay, k_decay, chunk_decay, gn_g, gn_b, r0)


def _pad_cols(w, n):
    return jnp.pad(w, ((0, 0), (0, n - w.shape[1])))


def _pad_rows(w, n):
    return jnp.pad(w, ((0, n - w.shape[0]), (0, 0)))


def _row(vec):
    return vec.reshape(1, -1).astype(F32)


def _prep_rwkv(j, wt):
    def lora(w_in, w_out, width):
        return _pad_cols(w_in, width).astype(BF16), _pad_rows(w_out, width).astype(BF16)

    w1, w2 = lora(wt["a_w1"][j], wt["a_w2"][j], 128)
    a1, a2 = lora(wt["a_a1"][j], wt["a_a2"][j], 128)
    g1, g2 = lora(wt["a_g1"][j], wt["a_g2"][j], 256)
    out = dict(mu=wt["a_mu"][j], w_rkv=wt["a_w_rkv"][j].astype(BF16), w0=_row(wt["a_w0"][j]), w1=w1, w2=w2,
               a0=_row(wt["a_a0"][j]), a1=a1, a2=a2, g1=g1, g2=g2,
               k_k=_row(wt["a_k_k"][j]), k_a=_row(wt["a_k_a"][j]), r_k=_row(wt["a_r_k"][j]),
               gn_g=_row(wt["a_gn_g"][j]), gn_b=_row(wt["a_gn_b"][j]))
    if j > 0:
        v1, v2 = lora(wt["a_v1"][j - 1], wt["a_v2"][j - 1], 128)
        out.update(v0=_row(wt["a_v0"][j - 1]), v1=v1, v2=v2)
    return out


def _prep_fox(j, wt):
    d = D_MODEL
    w_in = wt["b_w_in"][j]
    return dict(w_in=w_in[:, :4 * d].astype(BF16), w_f=_pad_cols(w_in[:, 4 * d:], LANES).astype(BF16),
                b_f=_pad_cols(_row(wt["b_b_f"][j]), LANES),
                q_gain=_row(jnp.tile(wt["b_q_gain"][j], B_HEADS)),
                k_gain=_row(jnp.tile(wt["b_k_gain"][j], B_HEADS)))


def _rope_tables(pos):
    inv = ROPE_BASE ** (-jnp.arange(0, C_QK, 2, dtype=F32) / C_QK)
    ang = pos[:, None].astype(F32) * inv[None]
    cos, sin = jnp.cos(ang), jnp.sin(ang)
    cos_t = jnp.tile(jnp.concatenate([cos, cos], -1), (1, C_HEADS))
    sin_t = jnp.tile(jnp.concatenate([-sin, sin], -1), (1, C_HEADS))
    return cos_t, sin_t


def _run_trunk(x, ple, pos0, a_shift, a_wkv, b_cache, c_state, wt, prep):
    bsz, t, d = x.shape
    m = bsz * t
    v_first = None
    na_shift, na_wkv, nb_k, nb_v, nb_logf, nc_state = [], [], [], [], [], []
    for i in range(DEPTH):
        kind, j = i % N_MIXERS, i // N_MIXERS
        x2 = x.reshape(m, d)
        if kind == 0:
            w = prep["rwkv"][j]
            xprev = jnp.concatenate([a_shift[j][:, None, :], x[:, :-1]], axis=1).reshape(m, d)
            r, k, v, wl, a, g = _rwkv_proj(x2, xprev, v_first, w)
            if j == 0:
                v_first = v
            seq = lambda arr: arr.reshape(bsz, t, d)
            z, s_pairs = _rwkv_chunk(seq(r), seq(wl), seq(k), seq(v), seq(a), seq(g), w,
                                     _state_to_pairs(a_wkv[j].astype(F32)))
            na_shift.append(x[:, -1])
            na_wkv.append(_pairs_to_state(s_pairs))
            z = z.reshape(m, d)
            wo = prep["a_w_o"][j]
        elif kind == 1:
            w = prep["fox"][j]
            qb, kf, kb, vf, vb, gate, logf_pad = _fox_proj(x2, w)
            logf = logf_pad[:, :B_HEADS].reshape(bsz, t, B_HEADS)
            c_new = jnp.cumsum(logf, axis=1)
            seq = lambda arr: arr.reshape(bsz, t, d)
            if b_cache is None:
                tile = min(ATTN_TILE, t)
                bias_k = (-c_new).reshape(bsz, t // tile, tile, PAIRS, 2).transpose(0, 3, 1, 4, 2)
                z = _fox_attn(seq(qb), seq(kb), seq(vb), bias_k, seq(gate))
            else:
                ck, cv, clogf = b_cache[0][j], b_cache[1][j], b_cache[2][j]
                plen = ck.shape[1]
                z = _fox_cache_attn(seq(qb), seq(kb), seq(vb), ck.reshape(bsz, plen, d),
                                    cv.reshape(bsz, plen, d), jnp.swapaxes(clogf.astype(F32), 1, 2),
                                    jnp.swapaxes(c_new, 1, 2), seq(gate))
            nb_k.append(kf.reshape(bsz, t, B_HEADS, B_HEAD))
            nb_v.append(vf.reshape(bsz, t, B_HEADS, B_HEAD))
            nb_logf.append(logf)
            z = z.reshape(m, d)
            wo = prep["b_w_o"][j]
        else:
            cos, sin = _rope_tables(pos0 + jnp.arange(t))
            tm = min(ROW_TILE, m)
            if t >= tm:
                table_blocks = t // tm
            else:
                cos, sin = jnp.tile(cos, (tm // t, 1)), jnp.tile(sin, (tm // t, 1))
                table_blocks = 1
            q, k, v, g = _ret_proj(x2, prep["c_w_in"][j], cos, sin, table_blocks)
            z, r_new = _ret_chunk(q.reshape(bsz, t, d), k.reshape(bsz, t, d), v.reshape(bsz, t, 2 * d),
                                  g.reshape(bsz, t, 2 * d), _row(wt["c_gn_g"][j]), _row(wt["c_gn_b"][j]),
                                  c_state[j].astype(F32))
            nc_state.append(r_new)
            z = z.reshape(m, 2 * d)
            wo = prep["c_w_o"][j]
        x = _post_mixer(z, x2, ple[i].reshape(m, D_PLE), wo,
                        _row(wt["ln_mix_g"][i]), _row(wt["ln_mix_b"][i]), prep["ffn_w1"][i], prep["ffn_w2"][i],
                        _row(wt["ln_ffn_g"][i]), _row(wt["ln_ffn_b"][i]), prep["ple_gate"][i],
                        prep["ple_in"][i]).reshape(bsz, t, d)
    return (x, jnp.stack(na_shift), jnp.stack(na_wkv), jnp.stack(nb_k), jnp.stack(nb_v),
            jnp.stack(nb_logf), jnp.stack(nc_state))


def kernel(x_prompt, x_sample, p_prompt, p_sample, state_a_shift, state_a_wkv, cache_b_k, cache_b_v, cache_b_logf, state_c, ln_mix_g, ln_mix_b, ln_ffn_g, ln_ffn_b, ffn_w1, ffn_w2, ple_in, ple_gate, a_mu, a_w_rkv, a_w0, a_w1, a_w2, a_a0, a_a1, a_a2, a_v0, a_v1, a_v2, a_g1, a_g2, a_k_k, a_k_a, a_r_k, a_gn_g, a_gn_b, a_w_o, b_w_in, b_b_f, b_q_gain, b_k_gain, b_w_o, c_w_in, c_gn_g, c_gn_b, c_w_o):
    wt = dict(ln_mix_g=ln_mix_g, ln_mix_b=ln_mix_b, ln_ffn_g=ln_ffn_g, ln_ffn_b=ln_ffn_b,
              a_mu=a_mu, a_w_rkv=a_w_rkv, a_w0=a_w0, a_w1=a_w1, a_w2=a_w2, a_a0=a_a0, a_a1=a_a1, a_a2=a_a2,
              a_v0=a_v0, a_v1=a_v1, a_v2=a_v2, a_g1=a_g1, a_g2=a_g2, a_k_k=a_k_k, a_k_a=a_k_a, a_r_k=a_r_k,
              a_gn_g=a_gn_g, a_gn_b=a_gn_b, b_w_in=b_w_in, b_b_f=b_b_f, b_q_gain=b_q_gain,
              b_k_gain=b_k_gain, c_gn_g=c_gn_g, c_gn_b=c_gn_b)
    n_a, n_b, n_c = a_mu.shape[0], b_w_in.shape[0], c_w_in.shape[0]
    prep = dict(
        rwkv=[_prep_rwkv(j, wt) for j in range(n_a)],
        fox=[_prep_fox(j, wt) for j in range(n_b)],
        a_w_o=a_w_o.astype(BF16), b_w_o=b_w_o.astype(BF16), c_w_o=c_w_o.astype(BF16),
        c_w_in=c_w_in.astype(BF16), ffn_w1=ffn_w1.astype(BF16), ffn_w2=ffn_w2.astype(BF16),
        ple_gate=ple_gate.astype(BF16), ple_in=ple_in.astype(BF16))
    nbat = x_prompt.shape[0]
    zero_shift = jnp.zeros((n_a, nbat, D_MODEL), x_prompt.dtype)
    zero_wkv = jnp.zeros((n_a, nbat, A_HEADS, A_HEAD, A_HEAD), F32)
    zero_ret = jnp.zeros((n_c, nbat, C_HEADS, C_QK, C_V), F32)
    past_len = cache_b_k.shape[2]
    y_prompt, pa_shift, pa_wkv, pb_k, pb_v, pb_logf, pc_state = _run_trunk(
        x_prompt, p_prompt, 0, zero_shift, zero_wkv, None, zero_ret, wt, prep)
    y_sample, sa_shift, sa_wkv, sb_k, sb_v, sb_logf, sc_state = _run_trunk(
        x_sample, p_sample, past_len, state_a_shift, state_a_wkv, (cache_b_k, cache_b_v, cache_b_logf),
        state_c, wt, prep)
    return (y_prompt, y_sample, pa_shift, pa_wkv, pb_k, pb_v, pb_logf, pc_state,
            sa_shift, sa_wkv, sb_k, sb_v, sb_logf, sc_state)
```

```python
import functools
import math

import jax
import jax.numpy as jnp
from jax import lax
from jax.experimental import pallas as pl
from jax.experimental.pallas import tpu as pltpu

F32 = jnp.float32
BF16 = jnp.bfloat16

D_MODEL = 1024
DEPTH = 4
N_MIXERS = 3
D_PLE = 256
D_FF = 4 * D_MODEL
DN_ALPHA = (2.0 * DEPTH) ** 0.25
LN_EPS = 1e-5
A_HEAD = 64
A_HEADS = D_MODEL // A_HEAD
A_GN_EPS = 64e-5
B_HEAD = 64
B_HEADS = D_MODEL // B_HEAD
C_HEADS = 8
C_QK = D_MODEL // C_HEADS
C_V = 2 * D_MODEL // C_HEADS
ROPE_BASE = 10000.0
RMS_EPS = 1e-6

LANES = 128
PAIRS = D_MODEL // LANES
VMEM_LIMIT = 56 * 1024 * 1024
ROW_TILE = 512
FFN_CHUNK = 1024
RWKV_CHUNK = 64
RET_CHUNK = 256
ATTN_TILE = 512
CACHE_TILE = 1024


def _cparams(semantics):
    return pltpu.CompilerParams(dimension_semantics=semantics, vmem_limit_bytes=VMEM_LIMIT)


def _resident(shape):
    nd = len(shape)
    return pl.BlockSpec(shape, lambda *_: (0,) * nd, pipeline_mode=pl.Buffered(1))


def _rows(tm, width):
    return pl.BlockSpec((tm, width), lambda i: (i, 0))


def _dot(a, b):
    return jnp.dot(a.astype(BF16), b.astype(BF16), preferred_element_type=F32)


def _dot_nt(a, b):
    return lax.dot_general(a.astype(BF16), b.astype(BF16), (((1,), (1,)), ((), ())),
                           preferred_element_type=F32)


def _dot_tn(a, b):
    return lax.dot_general(a.astype(BF16), b.astype(BF16), (((0,), (0,)), ((), ())),
                           preferred_element_type=F32)


def _sigmoid(x):
    return 1.0 / (1.0 + jnp.exp(-x))


def _softplus(x):
    return jnp.maximum(x, 0.0) + jnp.log(1.0 + jnp.exp(-jnp.abs(x)))


def _layer_norm(x, g, b):
    mu = jnp.mean(x, -1, keepdims=True)
    xc = x - mu
    var = jnp.mean(xc * xc, -1, keepdims=True)
    return xc * lax.rsqrt(var + LN_EPS) * g + b


def _pair_sum(x, first):
    s0 = jnp.sum(jnp.where(first, x, 0.0), -1, keepdims=True)
    s1 = jnp.sum(jnp.where(first, 0.0, x), -1, keepdims=True)
    return jnp.where(first, s0, s1)


def _post_kernel(z_ref, x_ref, p_ref, wo_ref, g1_ref, b1_ref, w1_ref, w2_ref, g2_ref, b2_ref,
                 wg_ref, wp_ref, o_ref):
    x = x_ref[...]
    x1 = _layer_norm(DN_ALPHA * x + _dot(z_ref[...], wo_ref[...]), g1_ref[...], b1_ref[...])
    x1b = x1.astype(BF16)
    acc = jnp.zeros_like(x1)
    for f in range(D_FF // FFN_CHUNK):
        cols = slice(f * FFN_CHUNK, (f + 1) * FFN_CHUNK)
        h = jnp.maximum(_dot(x1b, w1_ref[:, cols]), 0.0)
        acc = acc + _dot(h * h, w2_ref[cols, :])
    x2 = _layer_norm(DN_ALPHA * x1 + acc, g2_ref[...], b2_ref[...])
    gate = _sigmoid(_dot(x2, wg_ref[...]))
    o_ref[...] = x2 + gate * _dot(p_ref[...], wp_ref[...])


def _post_mixer(z, x, p, wo, g1, b1, w1, w2, g2, b2, wg, wp):
    m, dz = z.shape
    tm = min(ROW_TILE, m)
    d = D_MODEL
    return pl.pallas_call(
        _post_kernel,
        out_shape=jax.ShapeDtypeStruct((m, d), F32),
        grid=(m // tm,),
        in_specs=[_rows(tm, dz), _rows(tm, d), _rows(tm, D_PLE), _resident((dz, d)),
                  _resident((1, d)), _resident((1, d)), _resident((d, D_FF)), _resident((D_FF, d)),
                  _resident((1, d)), _resident((1, d)), _resident((d, d)), _resident((D_PLE, d))],
        out_specs=_rows(tm, d),
        compiler_params=_cparams(("parallel",)),
        name="post_mixer",
    )(z, x, p, wo, g1, b1, w1, w2, g2, b2, wg, wp)


def _rwkv_proj_kernel(has_vres, *refs):
    if has_vres:
        (x_ref, xp_ref, vf_ref, mu_ref, wrkv_ref, w0_ref, w1_ref, w2_ref, a0_ref, a1_ref, a2_ref,
         g1_ref, g2_ref, v0_ref, v1_ref, v2_ref, r_o, k_o, v_o, wl_o, a_o, g_o) = refs
    else:
        (x_ref, xp_ref, mu_ref, wrkv_ref, w0_ref, w1_ref, w2_ref, a0_ref, a1_ref, a2_ref,
         g1_ref, g2_ref, r_o, k_o, v_o, wl_o, a_o, g_o) = refs
    x = x_ref[...]
    dx = xp_ref[...] - x
    xr, xw, xk, xv, xa, xg = [(x + dx * mu_ref[i:i + 1, :]).astype(BF16) for i in range(6)]
    r_o[...] = _dot(xr, wrkv_ref[0])
    k_o[...] = _dot(xk, wrkv_ref[1])
    v = _dot(xv, wrkv_ref[2])
    if has_vres:
        mix = _sigmoid(v0_ref[...] + _dot(_dot(xv, v1_ref[...]), v2_ref[...]))
        v = v + (vf_ref[...] - v) * mix
    v_o[...] = v
    pre = w0_ref[...] + _dot(jnp.tanh(_dot(xw, w1_ref[...])), w2_ref[...])
    w_log = -_softplus(-pre) - 0.5
    wl_o[...] = -jnp.exp(w_log)
    a_o[...] = _sigmoid(a0_ref[...] + _dot(_dot(xa, a1_ref[...]), a2_ref[...]))
    g_o[...] = _dot(_sigmoid(_dot(xg, g1_ref[...])), g2_ref[...])


def _rwkv_proj(x, xprev, vfirst, w):
    m, d = x.shape
    tm = min(ROW_TILE, m)
    has_vres = vfirst is not None
    args = [x, xprev] + ([vfirst] if has_vres else []) + [
        w["mu"], w["w_rkv"], w["w0"], w["w1"], w["w2"], w["a0"], w["a1"], w["a2"], w["g1"], w["g2"]]
    if has_vres:
        args += [w["v0"], w["v1"], w["v2"]]
    n_rows = 3 if has_vres else 2
    in_specs = [_rows(tm, d)] * n_rows + [_resident(a.shape) for a in args[n_rows:]]
    return pl.pallas_call(
        functools.partial(_rwkv_proj_kernel, has_vres),
        out_shape=[jax.ShapeDtypeStruct((m, d), F32)] * 6,
        grid=(m // tm,),
        in_specs=in_specs,
        out_specs=[_rows(tm, d)] * 6,
        compiler_params=_cparams(("parallel",)),
        name="rwkv_proj",
    )(*args)


def _split3(x):
    hi = x.astype(BF16)
    r1 = x - hi.astype(F32)
    mid = r1.astype(BF16)
    lo = (r1 - mid.astype(F32)).astype(BF16)
    return hi, mid, lo


def _rwkv_chunk_kernel(L, r_ref, wl_ref, k_ref, v_ref, a_ref, g_ref, kk_ref, ka_ref, rk_ref,
                       gng_ref, gnb_ref, s0_ref, z_ref, sout_ref, s_scr):
    c = pl.program_id(1)

    @pl.when(c == 0)
    def _():
        s_scr[...] = s0_ref[0]

    L2 = 2 * L
    first = lax.broadcasted_iota(jnp.int32, (L, LANES), 1) < A_HEAD
    row = lax.broadcasted_iota(jnp.int32, (L2, L2), 0)
    col = lax.broadcasted_iota(jnp.int32, (L2, L2), 1)
    same_head = (row >= L) == (col >= L)
    strict = same_head & (row > col)
    incl = same_head & (row >= col)
    eye = jnp.where(row == col, 1.0, 0.0).astype(F32)
    tri = jnp.where(lax.broadcasted_iota(jnp.int32, (L, L), 0) >=
                    lax.broadcasted_iota(jnp.int32, (L, L), 1), 1.0, 0.0).astype(BF16)

    def stack(x):
        return jnp.concatenate([jnp.where(first, x, 0.0), jnp.where(first, 0.0, x)], axis=0).astype(BF16)

    pairs = range(PAIRS)
    sls = [slice(p * LANES, (p + 1) * LANES) for p in pairs]
    r = [r_ref[0, :, sl] for sl in sls]
    wl = [wl_ref[0, :, sl] for sl in sls]
    v = [v_ref[0, :, sl] for sl in sls]
    a = [a_ref[0, :, sl] for sl in sls]
    k_raw = [k_ref[0, :, sl] for sl in sls]
    kk = [k_raw[p] * kk_ref[:, sls[p]] for p in pairs]
    kk = [x / jnp.maximum(jnp.sqrt(_pair_sum(x * x, first)), 1e-12) for x in kk]
    k = [k_raw[p] * (1.0 + (a[p] - 1.0) * ka_ref[:, sls[p]]) for p in pairs]
    b = [kk[p] * a[p] for p in pairs]

    def cumsum(x):
        hi, mid, lo = _split3(x)
        return (jnp.dot(tri, hi, preferred_element_type=F32) + jnp.dot(tri, mid, preferred_element_type=F32)
                + jnp.dot(tri, lo, preferred_element_type=F32))

    cum = [cumsum(x) for x in wl]
    g_t = [jnp.exp(x) for x in cum]
    g_prev = [jnp.exp(cum[p] - wl[p]) for p in pairs]
    g_inv = [jnp.exp(-x) for x in cum]
    g_end = [x[L - 1:L, :] for x in g_t]
    kinv = [k[p] * g_inv[p] for p in pairs]
    binv = [b[p] * g_inv[p] for p in pairs]
    lhs = [jnp.concatenate([stack(kk[p] * g_prev[p]), stack(r[p] * g_t[p])], axis=0) for p in pairs]
    rhs = [jnp.concatenate([stack(kinv[p]), stack(binv[p])], axis=0) for p in pairs]
    s_v = [stack(x) for x in v]
    gram = [_dot_nt(lhs[p], rhs[p]) for p in pairs]
    m_k = [jnp.where(strict, x[:L2, :L2], 0.0).astype(BF16) for x in gram]
    n_k = [jnp.where(incl, x[L2:, :L2], 0.0).astype(BF16) for x in gram]
    n_b = [jnp.where(incl, x[L2:, L2:], 0.0) for x in gram]
    x_pow = [jnp.where(strict, -x[:L2, L2:], 0.0) for x in gram]
    t_inv = [eye + x for x in x_pow]
    n = 1
    while 2 * n < L:
        x_pow = [_dot(x, x) for x in x_pow]
        t_inv = [t_inv[p] + _dot(t_inv[p], x_pow[p]) for p in pairs]
        n *= 2
    s_prev = [s_scr[p] for p in pairs]
    from_state = [_dot_nt(lhs[p], s_prev[p]) for p in pairs]
    u = [_dot(t_inv[p], from_state[p][:L2] + _dot(m_k[p], s_v[p])).astype(BF16) for p in pairs]
    vu = [jnp.concatenate([s_v[p], u[p]], axis=0) for p in pairs]
    y2 = [from_state[p][L2:] + _dot(jnp.concatenate([n_k[p], (-n_b[p]).astype(BF16)], axis=1), vu[p])
          for p in pairs]
    y = [x[:L] + x[L:] for x in y2]
    ends = [jnp.concatenate([stack(kinv[p] * g_end[p]), stack(-(binv[p] * g_end[p]))], axis=0) for p in pairs]
    for p in pairs:
        s_scr[p] = s_prev[p] * g_end[p] + _dot_tn(vu[p], ends[p])
    for p in pairs:
        sl = sls[p]
        mu = _pair_sum(y[p], first) * (1.0 / A_HEAD)
        yc = y[p] - mu
        var = _pair_sum(yc * yc, first) * (1.0 / A_HEAD)
        yn = yc * lax.rsqrt(var + A_GN_EPS) * gng_ref[:, sl] + gnb_ref[:, sl]
        bonus = _pair_sum(r[p] * k[p] * rk_ref[:, sl], first) * v[p]
        z_ref[0, :, sl] = ((yn + bonus) * g_ref[0, :, sl]).astype(BF16)

    @pl.when(c == pl.num_programs(1) - 1)
    def _():
        sout_ref[0] = s_scr[...]


def _rwkv_chunk(r, wl, k, v, a, g, w, s0_pairs):
    bsz, t, d = r.shape
    L = min(RWKV_CHUNK, t)
    seq = pl.BlockSpec((1, L, d), lambda b, c: (b, c, 0))
    vec = pl.BlockSpec((1, d), lambda b, c: (0, 0))
    st = pl.BlockSpec((1, PAIRS, LANES, LANES), lambda b, c: (b, 0, 0, 0))
    return pl.pallas_call(
        functools.partial(_rwkv_chunk_kernel, L),
        out_shape=[jax.ShapeDtypeStruct((bsz, t, d), BF16),
                   jax.ShapeDtypeStruct((bsz, PAIRS, LANES, LANES), F32)],
        grid=(bsz, t // L),
        in_specs=[seq] * 6 + [vec] * 5 + [st],
        out_specs=[seq, st],
        scratch_shapes=[pltpu.VMEM((PAIRS, LANES, LANES), F32)],
        compiler_params=_cparams(("parallel", "arbitrary")),
        name="rwkv_chunk",
    )(r, wl, k, v, a, g, w["k_k"], w["k_a"], w["r_k"], w["gn_g"], w["gn_b"], s0_pairs)


def _state_to_pairs(s):
    bsz = s.shape[0]
    s = s.reshape(bsz, PAIRS, 2, A_HEAD, A_HEAD)
    z = jnp.zeros_like(s[:, :, 0])
    top = jnp.concatenate([s[:, :, 0], z], axis=-1)
    bot = jnp.concatenate([z, s[:, :, 1]], axis=-1)
    return jnp.concatenate([top, bot], axis=-2)


def _pairs_to_state(sp):
    bsz = sp.shape[0]
    s0 = sp[:, :, :A_HEAD, :A_HEAD]
    s1 = sp[:, :, A_HEAD:, A_HEAD:]
    return jnp.stack([s0, s1], axis=2).reshape(bsz, A_HEADS, A_HEAD, A_HEAD)


def _fox_proj_kernel(q_scale, x_ref, w_ref, wf_ref, bf_ref, qg_ref, kg_ref,
                     qb_o, kf_o, kb_o, vf_o, vb_o, gate_o, logf_o):
    d = D_MODEL
    xb = x_ref[...].astype(BF16)
    first = lax.broadcasted_iota(jnp.int32, (xb.shape[0], LANES), 1) < B_HEAD

    def rms(t, gain_ref, blk):
        ms = _pair_sum(t * t, first) * (1.0 / B_HEAD)
        return t * lax.rsqrt(ms + RMS_EPS) * gain_ref[:, blk]

    for j in range(PAIRS):
        blk = slice(j * LANES, (j + 1) * LANES)
        q = rms(_dot(xb, w_ref[:, j * LANES:(j + 1) * LANES]), qg_ref, blk)
        qb_o[:, blk] = (q * q_scale).astype(BF16)
        k = rms(_dot(xb, w_ref[:, d + j * LANES:d + (j + 1) * LANES]), kg_ref, blk)
        kf_o[:, blk] = k
        kb_o[:, blk] = k.astype(BF16)
    v = _dot(xb, w_ref[:, 2 * d:3 * d])
    vf_o[...] = v
    vb_o[...] = v.astype(BF16)
    gate_o[...] = _dot(xb, w_ref[:, 3 * d:4 * d])
    logf_o[...] = -_softplus(-(_dot(xb, wf_ref[...]) + bf_ref[...]))


def _fox_proj(x, w, q_scale):
    m, d = x.shape
    tm = min(ROW_TILE, m)
    outs = [jax.ShapeDtypeStruct((m, d), BF16), jax.ShapeDtypeStruct((m, d), F32),
            jax.ShapeDtypeStruct((m, d), BF16), jax.ShapeDtypeStruct((m, d), F32),
            jax.ShapeDtypeStruct((m, d), BF16), jax.ShapeDtypeStruct((m, d), F32),
            jax.ShapeDtypeStruct((m, LANES), F32)]
    return pl.pallas_call(
        functools.partial(_fox_proj_kernel, q_scale),
        out_shape=outs,
        grid=(m // tm,),
        in_specs=[_rows(tm, d), _resident((d, 4 * d)), _resident((d, LANES)), _resident((1, LANES)),
                  _resident((1, d)), _resident((1, d))],
        out_specs=[_rows(tm, d)] * 6 + [_rows(tm, LANES)],
        compiler_params=_cparams(("parallel",)),
        name="fox_proj",
    )(x, w["w_in"], w["w_f"], w["b_f"], w["q_gain"], w["k_gain"])


BIAS_PIECES = 3


def _fox_attn_kernel(tile, q_ref, k_ref, kb_ref, vt_ref, gate_ref, z_ref, m_scr, l_scr, acc_scr):
    qi = pl.program_id(2)
    lane = lax.broadcasted_iota(jnp.int32, (tile, LANES), 1)
    first = lane < B_HEAD
    ones = jnp.where(lane % B_HEAD < BIAS_PIECES, 1.0, 0.0).astype(BF16)
    q = q_ref[0]
    q_heads = (jnp.where(first, q, ones), jnp.where(first, ones, q))
    causal = (lax.broadcasted_iota(jnp.int32, (tile, tile), 0) <=
              lax.broadcasted_iota(jnp.int32, (tile, tile), 1))
    m_scr[...] = jnp.full_like(m_scr, -jnp.inf)
    l_scr[...] = jnp.zeros_like(l_scr)
    acc_scr[...] = jnp.zeros_like(acc_scr)

    def step(blocks):
        k_heads, vts = [], []
        for kb, _ in blocks:
            start = pl.multiple_of(kb * tile, tile)
            k = k_ref[0, pl.ds(start, tile), :]
            kbias = kb_ref[0, pl.ds(start, tile), :]
            k_heads.append((jnp.where(first, k, kbias), jnp.where(first, kbias, k)))
            vts.append(vt_ref[0, kb])
        m_all, l_all, acc_all = m_scr[...], l_scr[...], acc_scr[...]
        heads = range(2)
        rows = [slice(h * B_HEAD, (h + 1) * B_HEAD) for h in heads]
        s = [[_dot_nt(kh[h], q_heads[h]) for kh in k_heads] for h in heads]
        s = [[jnp.where(causal, x, -jnp.inf) if diag else x for x, (_, diag) in zip(s[h], blocks)]
             for h in heads]
        m_prev = [m_all[h:h + 1] for h in heads]
        m_new = list(m_prev)
        for h in heads:
            for x in s[h]:
                m_new[h] = jnp.maximum(m_new[h], jnp.max(x, 0, keepdims=True))
        p = [[jnp.exp2(x - m_new[h]) for x in s[h]] for h in heads]
        alpha = [jnp.exp2(m_prev[h] - m_new[h]) for h in heads]
        l_new = [alpha[h] * l_all[h:h + 1] for h in heads]
        acc_new = [alpha[h] * acc_all[rows[h]] for h in heads]
        for h in heads:
            for x, vt in zip(p[h], vts):
                l_new[h] = l_new[h] + jnp.sum(x, 0, keepdims=True)
                acc_new[h] = acc_new[h] + _dot(vt[rows[h]], x)
        m_scr[...] = jnp.concatenate(m_new, 0)
        l_scr[...] = jnp.concatenate(l_new, 0)
        acc_scr[...] = jnp.concatenate(acc_new, 0)

    @pl.loop(0, qi // 2)
    def _(j):
        step([(2 * j, False), (2 * j + 1, False)])

    @pl.when(qi % 2 == 1)
    def _():
        step([(qi - 1, False), (qi, True)])

    @pl.when(qi % 2 == 0)
    def _():
        step([(qi, True)])

    inv_l = 1.0 / l_scr[...]
    o_t = jnp.concatenate([acc_scr[:B_HEAD, :] * inv_l[0:1], acc_scr[B_HEAD:, :] * inv_l[1:2]], 0)
    z_ref[0] = (_sigmoid(gate_ref[0]) * o_t.T).astype(BF16)


def _fox_attn(q, k, k_bias, v_t, gate):
    bsz, t, d = q.shape
    tile = min(ATTN_TILE, t)
    nq = t // tile
    qspec = pl.BlockSpec((1, tile, LANES), lambda b, p, i: (b, i, p))
    kspec = pl.BlockSpec((1, t, LANES), lambda b, p, i: (b, 0, p))
    vspec = pl.BlockSpec((1, nq, LANES, tile), lambda b, p, i: (b, 0, p, 0))
    return pl.pallas_call(
        functools.partial(_fox_attn_kernel, tile),
        out_shape=jax.ShapeDtypeStruct((bsz, t, d), BF16),
        grid=(bsz, PAIRS, nq),
        in_specs=[qspec, kspec, kspec, vspec, qspec],
        out_specs=qspec,
        scratch_shapes=[pltpu.VMEM((2, tile), F32), pltpu.VMEM((2, tile), F32),
                        pltpu.VMEM((LANES, tile), F32)],
        compiler_params=_cparams(("parallel", "parallel", "arbitrary")),
        name="fox_attn",
    )(q, k, k_bias, v_t, gate)


def _fox_key_bias(c_new):
    bsz, t, h = c_new.shape
    rest = -c_new * math.log2(math.e)
    pieces = []
    for _ in range(BIAS_PIECES):
        top = lax.bitcast_convert_type(
            lax.bitcast_convert_type(rest, jnp.uint32) & jnp.uint32(0xFFFF0000), F32)
        pieces.append(top.astype(BF16))
        rest = rest - top
    packed = jnp.pad(jnp.stack(pieces, -1), ((0, 0), (0, 0), (0, 0), (0, B_HEAD - BIAS_PIECES)))
    packed = packed.reshape(bsz, t, h // 2, 2, B_HEAD)[:, :, :, ::-1, :]
    return packed.reshape(bsz, t, h * B_HEAD)


def _fox_cache_kernel(nq, q_ref, kn_ref, vn_ref, ck_ref, cv_ref, lf_ref, cn_ref, gate_ref, z_ref,
                      m_scr, l_scr, acc_scr, carry_scr):
    kb = pl.program_id(1)
    d = D_MODEL
    rows = B_HEADS * nq
    q = q_ref[0]
    row_head = lax.broadcasted_iota(jnp.int32, (rows, d), 0) // nq
    col_head = lax.broadcasted_iota(jnp.int32, (rows, d), 1) // B_HEAD
    q_rep = jnp.concatenate([q] * B_HEADS, axis=0)
    q_bd = jnp.where(row_head == col_head, q_rep, jnp.zeros_like(q_rep))

    def expand(bias):
        n = bias.shape[-1]
        return jnp.broadcast_to(bias[:, None, :], (B_HEADS, nq, n)).reshape(rows, n)

    @pl.when(kb == 0)
    def _():
        m_scr[...] = jnp.full_like(m_scr, -jnp.inf)
        l_scr[...] = jnp.zeros_like(l_scr)
        acc_scr[...] = jnp.zeros_like(acc_scr)
        carry_scr[...] = jnp.zeros_like(carry_scr)

    def update(s, v):
        m_prev = m_scr[...]
        m_new = jnp.maximum(m_prev, jnp.max(s, -1, keepdims=True))
        p = jnp.exp(s - m_new)
        alpha = jnp.exp(m_prev - m_new)
        l_scr[...] = alpha * l_scr[...] + jnp.sum(p, -1, keepdims=True)
        acc_scr[...] = alpha * acc_scr[...] + _dot(p, v)
        m_scr[...] = m_new

    cum = lf_ref[0]
    tk = cum.shape[-1]
    key = lax.broadcasted_iota(jnp.int32, cum.shape, 1)
    shift = 1
    while shift < tk:
        cum = cum + jnp.where(key >= shift, pltpu.roll(cum, shift, 1), 0.0)
        shift *= 2
    cum = cum + carry_scr[...]
    carry_scr[...] = cum[:, tk - 1:tk]
    update(_dot_nt(q_bd, ck_ref[0]) - expand(cum), cv_ref[0])

    @pl.when(kb == pl.num_programs(1) - 1)
    def _():
        s = _dot_nt(q_bd, kn_ref[0]) - expand(carry_scr[...] + cn_ref[0])
        qpos = lax.broadcasted_iota(jnp.int32, (rows, nq), 0) % nq
        kpos = lax.broadcasted_iota(jnp.int32, (rows, nq), 1)
        update(jnp.where(kpos <= qpos, s, -jnp.inf), vn_ref[0])
        o_all = acc_scr[...] / l_scr[...]
        head_of_col = lax.broadcasted_iota(jnp.int32, (nq, d), 1) // B_HEAD
        o = jnp.zeros((nq, d), F32)
        for h in range(B_HEADS):
            o = o + jnp.where(head_of_col == h, o_all[h * nq:(h + 1) * nq, :], 0.0)
        z_ref[0] = (_sigmoid(gate_ref[0]) * o).astype(BF16)


def _fox_cache_attn(q, kn, vn, ck, cv, logf_cache, c_new, gate):
    bsz, nq, d = q.shape
    plen = ck.shape[1]
    tk = min(CACHE_TILE, plen)
    rows = B_HEADS * nq
    new = pl.BlockSpec((1, nq, d), lambda b, j: (b, 0, 0))
    cache = pl.BlockSpec((1, tk, d), lambda b, j: (b, j, 0))
    return pl.pallas_call(
        functools.partial(_fox_cache_kernel, nq),
        out_shape=jax.ShapeDtypeStruct((bsz, nq, d), BF16),
        grid=(bsz, plen // tk),
        in_specs=[new, new, new, cache, cache,
                  pl.BlockSpec((1, B_HEADS, tk), lambda b, j: (b, 0, j)),
                  pl.BlockSpec((1, B_HEADS, nq), lambda b, j: (b, 0, 0)), new],
        out_specs=new,
        scratch_shapes=[pltpu.VMEM((rows, 1), F32), pltpu.VMEM((rows, 1), F32),
                        pltpu.VMEM((rows, d), F32), pltpu.VMEM((B_HEADS, 1), F32)],
        compiler_params=_cparams(("parallel", "arbitrary")),
        name="fox_cache_attn",
    )(q, kn, vn, ck, cv, logf_cache, c_new, gate)


def _ret_proj_kernel(x_ref, w_ref, cos_ref, sin_ref, q_o, k_o, v_o, g_o):
    d = D_MODEL
    xb = x_ref[...].astype(BF16)
    for h in range(C_HEADS):
        blk = slice(h * C_QK, (h + 1) * C_QK)
        cos = cos_ref[:, blk]
        sin = sin_ref[:, blk]
        q = _dot(xb, w_ref[:, h * C_QK:(h + 1) * C_QK])
        q_o[:, blk] = (q * cos + pltpu.roll(q, C_QK // 2, 1) * sin).astype(BF16)
        k = _dot(xb, w_ref[:, d + h * C_QK:d + (h + 1) * C_QK])
        k_o[:, blk] = ((k * cos + pltpu.roll(k, C_QK // 2, 1) * sin) * (C_QK ** -0.5)).astype(BF16)
    v_o[...] = _dot(xb, w_ref[:, 2 * d:4 * d]).astype(BF16)
    g_o[...] = _dot(xb, w_ref[:, 4 * d:6 * d])


def _ret_proj(x, w_in, cos, sin, table_blocks):
    m, d = x.shape
    tm = min(ROW_TILE, m)
    tab = pl.BlockSpec((tm, d), lambda i: (i % table_blocks, 0))
    return pl.pallas_call(
        _ret_proj_kernel,
        out_shape=[jax.ShapeDtypeStruct((m, d), BF16), jax.ShapeDtypeStruct((m, d), BF16),
                   jax.ShapeDtypeStruct((m, 2 * d), BF16), jax.ShapeDtypeStruct((m, 2 * d), F32)],
        grid=(m // tm,),
        in_specs=[_rows(tm, d), _resident((d, 6 * d)), tab, tab],
        out_specs=[_rows(tm, d), _rows(tm, d), _rows(tm, 2 * d), _rows(tm, 2 * d)],
        compiler_params=_cparams(("parallel",)),
        name="ret_proj",
    )(x, w_in, cos, sin)


def _ret_chunk_kernel(q_ref, k_ref, v_ref, g_ref, intra_ref, qd_ref, kd_ref, cd_ref, gng_ref, gnb_ref,
                      r0_ref, z_ref, rout_ref, r_scr):
    c = pl.program_id(1)

    @pl.when(c == 0)
    def _():
        r_scr[...] = r0_ref[0]

    heads = range(C_HEADS)
    qk = [slice(h * C_QK, (h + 1) * C_QK) for h in heads]
    vg = [slice(h * C_V, (h + 1) * C_V) for h in heads]
    q = [q_ref[0, :, sl] for sl in qk]
    k = [k_ref[0, :, sl] for sl in qk]
    v = [v_ref[0, :, sl] for sl in vg]
    state = [r_scr[h] for h in heads]
    s = [_dot_nt(q[h], k[h]) * intra_ref[h] for h in heads]
    o = [_dot(s[h], v[h]) + _dot(q[h], state[h]) * qd_ref[h] for h in heads]
    for h in heads:
        r_scr[h] = state[h] * cd_ref[h] + _dot_tn(k[h].astype(F32) * kd_ref[h], v[h])
    for h in heads:
        mu = jnp.mean(o[h], -1, keepdims=True)
        oc = o[h] - mu
        var = jnp.mean(oc * oc, -1, keepdims=True)
        on = oc * lax.rsqrt(var + LN_EPS) * gng_ref[:, vg[h]] + gnb_ref[:, vg[h]]
        g = g_ref[0, :, vg[h]]
        z_ref[0, :, vg[h]] = (g * _sigmoid(g) * on).astype(BF16)

    @pl.when(c == pl.num_programs(1) - 1)
    def _():
        rout_ref[0] = r_scr[...]


def _ret_chunk(q, k, v, g, gn_g, gn_b, r0):
    bsz, t, _ = q.shape
    L = min(RET_CHUNK, t)
    log_g = jnp.log1p(-jnp.exp2(-5.0 - jnp.arange(C_HEADS, dtype=F32)))
    idx = jnp.arange(L, dtype=F32)
    rel = idx[:, None] - idx[None, :]
    intra = jnp.where(rel[None] >= 0, jnp.exp(rel[None] * log_g[:, None, None]), 0.0)
    q_decay = jnp.exp((idx[None, :, None] + 1.0) * log_g[:, None, None])
    k_decay = jnp.exp((L - 1.0 - idx[None, :, None]) * log_g[:, None, None])
    chunk_decay = jnp.exp(L * log_g)[:, None, None]
    qk = pl.BlockSpec((1, L, D_MODEL), lambda b, c: (b, c, 0))
    vg = pl.BlockSpec((1, L, 2 * D_MODEL), lambda b, c: (b, c, 0))
    st = pl.BlockSpec((1, C_HEADS, C_QK, C_V), lambda b, c: (b, 0, 0, 0))
    return pl.pallas_call(
        _ret_chunk_kernel,
        out_shape=[jax.ShapeDtypeStruct((bsz, t, 2 * D_MODEL), BF16),
                   jax.ShapeDtypeStruct((bsz, C_HEADS, C_QK, C_V), F32)],
        grid=(bsz, t // L),
        in_specs=[qk, qk, vg, vg, _resident((C_HEADS, L, L)), _resident((C_HEADS, L, 1)),
                  _resident((C_HEADS, L, 1)), _resident((C_HEADS, 1, 1)), _resident((1, 2 * D_MODEL)),
                  _resident((1, 2 * D_MODEL)), st],
        out_specs=[vg, st],
        scratch_shapes=[pltpu.VMEM((C_HEADS, C_QK, C_V), F32)],
        compiler_params=_cparams(("parallel", "arbitrary")),
        name="ret_chunk",
    )(q, k, v, g, intra, q_decay, k_decay, chunk_decay, gn_g, gn_b, r0)


def _pad_cols(w, n):
    return jnp.pad(w, ((0, 0), (0, n - w.shape[1])))


def _pad_rows(w, n):
    return jnp.pad(w, ((0, n - w.shape[0]), (0, 0)))


def _row(vec):
    return vec.reshape(1, -1).astype(F32)


def _prep_rwkv(j, wt):
    def lora(w_in, w_out, width):
        return _pad_cols(w_in, width).astype(BF16), _pad_rows(w_out, width).astype(BF16)

    w1, w2 = lora(wt["a_w1"][j], wt["a_w2"][j], 128)
    a1, a2 = lora(wt["a_a1"][j], wt["a_a2"][j], 128)
    g1, g2 = lora(wt["a_g1"][j], wt["a_g2"][j], 256)
    out = dict(mu=wt["a_mu"][j], w_rkv=wt["a_w_rkv"][j].astype(BF16), w0=_row(wt["a_w0"][j]), w1=w1, w2=w2,
               a0=_row(wt["a_a0"][j]), a1=a1, a2=a2, g1=g1, g2=g2,
               k_k=_row(wt["a_k_k"][j]), k_a=_row(wt["a_k_a"][j]), r_k=_row(wt["a_r_k"][j]),
               gn_g=_row(wt["a_gn_g"][j]), gn_b=_row(wt["a_gn_b"][j]))
    if j > 0:
        v1, v2 = lora(wt["a_v1"][j - 1], wt["a_v2"][j - 1], 128)
        out.update(v0=_row(wt["a_v0"][j - 1]), v1=v1, v2=v2)
    return out


def _prep_fox(j, wt):
    d = D_MODEL
    w_in = wt["b_w_in"][j]
    return dict(w_in=w_in[:, :4 * d].astype(BF16), w_f=_pad_cols(w_in[:, 4 * d:], LANES).astype(BF16),
                b_f=_pad_cols(_row(wt["b_b_f"][j]), LANES),
                q_gain=_row(jnp.tile(wt["b_q_gain"][j], B_HEADS)),
                k_gain=_row(jnp.tile(wt["b_k_gain"][j], B_HEADS)))


def _rope_tables(pos):
    inv = ROPE_BASE ** (-jnp.arange(0, C_QK, 2, dtype=F32) / C_QK)
    ang = pos[:, None].astype(F32) * inv[None]
    cos, sin = jnp.cos(ang), jnp.sin(ang)
    cos_t = jnp.tile(jnp.concatenate([cos, cos], -1), (1, C_HEADS))
    sin_t = jnp.tile(jnp.concatenate([-sin, sin], -1), (1, C_HEADS))
    return cos_t, sin_t


def _run_trunk(x, ple, pos0, a_shift, a_wkv, b_cache, c_state, wt, prep):
    bsz, t, d = x.shape
    m = bsz * t
    v_first = None
    na_shift, na_wkv, nb_k, nb_v, nb_logf, nc_state = [], [], [], [], [], []
    for i in range(DEPTH):
        kind, j = i % N_MIXERS, i // N_MIXERS
        x2 = x.reshape(m, d)
        if kind == 0:
            w = prep["rwkv"][j]
            xprev = jnp.concatenate([a_shift[j][:, None, :], x[:, :-1]], axis=1).reshape(m, d)
            r, k, v, wl, a, g = _rwkv_proj(x2, xprev, v_first, w)
            if j == 0:
                v_first = v
            seq = lambda arr: arr.reshape(bsz, t, d)
            z, s_pairs = _rwkv_chunk(seq(r), seq(wl), seq(k), seq(v), seq(a), seq(g), w,
                                     _state_to_pairs(a_wkv[j].astype(F32)))
            na_shift.append(x[:, -1])
            na_wkv.append(_pairs_to_state(s_pairs))
            z = z.reshape(m, d)
            wo = prep["a_w_o"][j]
        elif kind == 1:
            w = prep["fox"][j]
            fresh = b_cache is None
            q_scale = B_HEAD ** -0.5 * (math.log2(math.e) if fresh else 1.0)
            qb, kf, kb, vf, vb, gate, logf_pad = _fox_proj(x2, w, q_scale)
            logf = logf_pad[:, :B_HEADS].reshape(bsz, t, B_HEADS)
            c_new = jnp.cumsum(logf, axis=1)
            seq = lambda arr: arr.reshape(bsz, t, d)
            if fresh:
                tile = min(ATTN_TILE, t)
                v_t = vb.reshape(bsz, t // tile, tile, d).transpose(0, 1, 3, 2)
                z = _fox_attn(seq(qb), seq(kb), _fox_key_bias(c_new), v_t, seq(gate))
            else:
                ck, cv, clogf = b_cache[0][j], b_cache[1][j], b_cache[2][j]
                plen = ck.shape[1]
                z = _fox_cache_attn(seq(qb), seq(kb), seq(vb), ck.reshape(bsz, plen, d),
                                    cv.reshape(bsz, plen, d), jnp.swapaxes(clogf.astype(F32), 1, 2),
                                    jnp.swapaxes(c_new, 1, 2), seq(gate))
            nb_k.append(kf.reshape(bsz, t, B_HEADS, B_HEAD))
            nb_v.append(vf.reshape(bsz, t, B_HEADS, B_HEAD))
            nb_logf.append(logf)
            z = z.reshape(m, d)
            wo = prep["b_w_o"][j]
        else:
            cos, sin = _rope_tables(pos0 + jnp.arange(t))
            tm = min(ROW_TILE, m)
            if t >= tm:
                table_blocks = t // tm
            else:
                cos, sin = jnp.tile(cos, (tm // t, 1)), jnp.tile(sin, (tm // t, 1))
                table_blocks = 1
            q, k, v, g = _ret_proj(x2, prep["c_w_in"][j], cos, sin, table_blocks)
            z, r_new = _ret_chunk(q.reshape(bsz, t, d), k.reshape(bsz, t, d), v.reshape(bsz, t, 2 * d),
                                  g.reshape(bsz, t, 2 * d), _row(wt["c_gn_g"][j]), _row(wt["c_gn_b"][j]),
                                  c_state[j].astype(F32))
            nc_state.append(r_new)
            z = z.reshape(m, 2 * d)
            wo = prep["c_w_o"][j]
        x = _post_mixer(z, x2, ple[i].reshape(m, D_PLE), wo,
                        _row(wt["ln_mix_g"][i]), _row(wt["ln_mix_b"][i]), prep["ffn_w1"][i], prep["ffn_w2"][i],
                        _row(wt["ln_ffn_g"][i]), _row(wt["ln_ffn_b"][i]), prep["ple_gate"][i],
                        prep["ple_in"][i]).reshape(bsz, t, d)
    return (x, jnp.stack(na_shift), jnp.stack(na_wkv), jnp.stack(nb_k), jnp.stack(nb_v),
            jnp.stack(nb_logf), jnp.stack(nc_state))


def kernel(x_prompt, x_sample, p_prompt, p_sample, state_a_shift, state_a_wkv, cache_b_k, cache_b_v, cache_b_logf, state_c, ln_mix_g, ln_mix_b, ln_ffn_g, ln_ffn_b, ffn_w1, ffn_w2, ple_in, ple_gate, a_mu, a_w_rkv, a_w0, a_w1, a_w2, a_a0, a_a1, a_a2, a_v0, a_v1, a_v2, a_g1, a_g2, a_k_k, a_k_a, a_r_k, a_gn_g, a_gn_b, a_w_o, b_w_in, b_b_f, b_q_gain, b_k_gain, b_w_o, c_w_in, c_gn_g, c_gn_b, c_w_o):
    wt = dict(ln_mix_g=ln_mix_g, ln_mix_b=ln_mix_b, ln_ffn_g=ln_ffn_g, ln_ffn_b=ln_ffn_b,
              a_mu=a_mu, a_w_rkv=a_w_rkv, a_w0=a_w0, a_w1=a_w1, a_w2=a_w2, a_a0=a_a0, a_a1=a_a1, a_a2=a_a2,
              a_v0=a_v0, a_v1=a_v1, a_v2=a_v2, a_g1=a_g1, a_g2=a_g2, a_k_k=a_k_k, a_k_a=a_k_a, a_r_k=a_r_k,
              a_gn_g=a_gn_g, a_gn_b=a_gn_b, b_w_in=b_w_in, b_b_f=b_b_f, b_q_gain=b_q_gain,
              b_k_gain=b_k_gain, c_gn_g=c_gn_g, c_gn_b=c_gn_b)
    n_a, n_b, n_c = a_mu.shape[0], b_w_in.shape[0], c_w_in.shape[0]
    prep = dict(
        rwkv=[_prep_rwkv(j, wt) for j in range(n_a)],
        fox=[_prep_fox(j, wt) for j in range(n_b)],
        a_w_o=a_w_o.astype(BF16), b_w_o=b_w_o.astype(BF16), c_w_o=c_w_o.astype(BF16),
        c_w_in=c_w_in.astype(BF16), ffn_w1=ffn_w1.astype(BF16), ffn_w2=ffn_w2.astype(BF16),
        ple_gate=ple_gate.astype(BF16), ple_in=ple_in.astype(BF16))
    nbat = x_prompt.shape[0]
    zero_shift = jnp.zeros((n_a, nbat, D_MODEL), x_prompt.dtype)
    zero_wkv = jnp.zeros((n_a, nbat, A_HEADS, A_HEAD, A_HEAD), F32)
    zero_ret = jnp.zeros((n_c, nbat, C_HEADS, C_QK, C_V), F32)
    past_len = cache_b_k.shape[2]
    y_prompt, pa_shift, pa_wkv, pb_k, pb_v, pb_logf, pc_state = _run_trunk(
        x_prompt, p_prompt, 0, zero_shift, zero_wkv, None, zero_ret, wt, prep)
    y_sample, sa_shift, sa_wkv, sb_k, sb_v, sb_logf, sc_state = _run_trunk(
        x_sample, p_sample, past_len, state_a_shift, state_a_wkv, (cache_b_k, cache_b_v, cache_b_logf),
        state_c, wt, prep)
    return (y_prompt, y_sample, pa_shift, pa_wkv, pb_k, pb_v, pb_logf, pc_state,
            sa_shift, sa_wkv, sb_k, sb_v, sb_logf, sc_state)
```

```python
import functools
import math

import jax
import jax.numpy as jnp
from jax import lax
from jax.experimental import pallas as pl
from jax.experimental.pallas import tpu as pltpu

F32 = jnp.float32
BF16 = jnp.bfloat16

D_MODEL = 1024
DEPTH = 4
N_MIXERS = 3
D_PLE = 256
D_FF = 4 * D_MODEL
DN_ALPHA = (2.0 * DEPTH) ** 0.25
LN_EPS = 1e-5
A_HEAD = 64
A_HEADS = D_MODEL // A_HEAD
A_GN_EPS = 64e-5
B_HEAD = 64
B_HEADS = D_MODEL // B_HEAD
C_HEADS = 8
C_QK = D_MODEL // C_HEADS
C_V = 2 * D_MODEL // C_HEADS
ROPE_BASE = 10000.0
RMS_EPS = 1e-6

LANES = 128
PAIRS = D_MODEL // LANES
VMEM_LIMIT = 56 * 1024 * 1024
ROW_TILE = 512
FFN_CHUNK = 1024
RWKV_CHUNK = 64
RET_CHUNK = 256
ATTN_TILE = 512
CACHE_TILE = 1024


def _cparams(semantics):
    return pltpu.CompilerParams(dimension_semantics=semantics, vmem_limit_bytes=VMEM_LIMIT)


def _resident(shape):
    nd = len(shape)
    return pl.BlockSpec(shape, lambda *_: (0,) * nd, pipeline_mode=pl.Buffered(1))


def _rows(tm, width):
    return pl.BlockSpec((tm, width), lambda i: (i, 0))


def _dot(a, b):
    return jnp.dot(a.astype(BF16), b.astype(BF16), preferred_element_type=F32)


def _dot_nt(a, b):
    return lax.dot_general(a.astype(BF16), b.astype(BF16), (((1,), (1,)), ((), ())),
                           preferred_element_type=F32)


def _dot_tn(a, b):
    return lax.dot_general(a.astype(BF16), b.astype(BF16), (((0,), (0,)), ((), ())),
                           preferred_element_type=F32)


def _sigmoid(x):
    return 1.0 / (1.0 + jnp.exp(-x))


def _softplus(x):
    return jnp.maximum(x, 0.0) + jnp.log(1.0 + jnp.exp(-jnp.abs(x)))


def _layer_norm(x, g, b):
    mu = jnp.mean(x, -1, keepdims=True)
    xc = x - mu
    var = jnp.mean(xc * xc, -1, keepdims=True)
    return xc * lax.rsqrt(var + LN_EPS) * g + b


def _pair_sum(x, first):
    s0 = jnp.sum(jnp.where(first, x, 0.0), -1, keepdims=True)
    s1 = jnp.sum(jnp.where(first, 0.0, x), -1, keepdims=True)
    return jnp.where(first, s0, s1)


def _post_kernel(z_ref, x_ref, p_ref, wo_ref, g1_ref, b1_ref, w1_ref, w2_ref, g2_ref, b2_ref,
                 wg_ref, wp_ref, o_ref):
    x = x_ref[...]
    x1 = _layer_norm(DN_ALPHA * x + _dot(z_ref[...], wo_ref[...]), g1_ref[...], b1_ref[...])
    x1b = x1.astype(BF16)
    acc = jnp.zeros_like(x1)
    for f in range(D_FF // FFN_CHUNK):
        cols = slice(f * FFN_CHUNK, (f + 1) * FFN_CHUNK)
        h = jnp.maximum(_dot(x1b, w1_ref[:, cols]), 0.0)
        acc = acc + _dot(h * h, w2_ref[cols, :])
    x2 = _layer_norm(DN_ALPHA * x1 + acc, g2_ref[...], b2_ref[...])
    gate = _sigmoid(_dot(x2, wg_ref[...]))
    o_ref[...] = x2 + gate * _dot(p_ref[...], wp_ref[...])


def _post_mixer(z, x, p, wo, g1, b1, w1, w2, g2, b2, wg, wp):
    m, dz = z.shape
    tm = min(ROW_TILE, m)
    d = D_MODEL
    return pl.pallas_call(
        _post_kernel,
        out_shape=jax.ShapeDtypeStruct((m, d), F32),
        grid=(m // tm,),
        in_specs=[_rows(tm, dz), _rows(tm, d), _rows(tm, D_PLE), _resident((dz, d)),
                  _resident((1, d)), _resident((1, d)), _resident((d, D_FF)), _resident((D_FF, d)),
                  _resident((1, d)), _resident((1, d)), _resident((d, d)), _resident((D_PLE, d))],
        out_specs=_rows(tm, d),
        compiler_params=_cparams(("parallel",)),
        name="post_mixer",
    )(z, x, p, wo, g1, b1, w1, w2, g2, b2, wg, wp)


def _rwkv_proj_kernel(has_vres, seg_rows, *refs):
    if has_vres:
        (x_ref, edge_ref, vf_ref, mu_ref, wrkv_ref, w0_ref, w1_ref, w2_ref, a0_ref, a1_ref, a2_ref,
         g1_ref, g2_ref, v0_ref, v1_ref, v2_ref, r_o, k_o, v_o, wl_o, a_o, g_o) = refs
    else:
        (x_ref, edge_ref, mu_ref, wrkv_ref, w0_ref, w1_ref, w2_ref, a0_ref, a1_ref, a2_ref,
         g1_ref, g2_ref, r_o, k_o, v_o, wl_o, a_o, g_o) = refs
    x = x_ref[...]
    row = lax.broadcasted_iota(jnp.int32, x.shape, 0)
    xprev = pltpu.roll(x, 1, 0)
    for s in range(x.shape[0] // seg_rows):
        xprev = jnp.where(row == s * seg_rows, edge_ref[0, s:s + 1, :], xprev)
    dx = xprev - x
    xr, xw, xk, xv, xa, xg = [(x + dx * mu_ref[i:i + 1, :]).astype(BF16) for i in range(6)]
    r_o[...] = _dot(xr, wrkv_ref[0])
    k_o[...] = _dot(xk, wrkv_ref[1])
    v = _dot(xv, wrkv_ref[2])
    if has_vres:
        mix = _sigmoid(v0_ref[...] + _dot(_dot(xv, v1_ref[...]), v2_ref[...]))
        v = v + (vf_ref[...] - v) * mix
    v_o[...] = v
    pre = w0_ref[...] + _dot(jnp.tanh(_dot(xw, w1_ref[...])), w2_ref[...])
    w_log = -_softplus(-pre) - 0.5
    wl_o[...] = -jnp.exp(w_log)
    a_o[...] = _sigmoid(a0_ref[...] + _dot(_dot(xa, a1_ref[...]), a2_ref[...]))
    g_o[...] = _dot(_sigmoid(_dot(xg, g1_ref[...])), g2_ref[...])


def _rwkv_proj(x, shift, vfirst, w):
    bsz, t, d = x.shape
    m = bsz * t
    tm = min(ROW_TILE, m)
    seg_rows = min(tm, t)
    seg_last = x.reshape(bsz, t // seg_rows, seg_rows, d)[:, :, -1]
    edge = jnp.concatenate([shift[:, None, :].astype(F32), seg_last[:, :-1]], axis=1)
    edge = edge.reshape(m // tm, tm // seg_rows, d)
    sublanes = 8
    if edge.shape[1] < sublanes:
        edge = jnp.pad(edge, ((0, 0), (0, sublanes - edge.shape[1]), (0, 0)))
    x = x.reshape(m, d)
    has_vres = vfirst is not None
    args = [x, edge] + ([vfirst] if has_vres else []) + [
        w["mu"], w["w_rkv"], w["w0"], w["w1"], w["w2"], w["a0"], w["a1"], w["a2"], w["g1"], w["g2"]]
    if has_vres:
        args += [w["v0"], w["v1"], w["v2"]]
    edge_spec = pl.BlockSpec((1,) + edge.shape[1:], lambda i: (i, 0, 0))
    n_rows = 3 if has_vres else 2
    in_specs = ([_rows(tm, d), edge_spec] + ([_rows(tm, d)] if has_vres else [])
                + [_resident(a.shape) for a in args[n_rows:]])
    return pl.pallas_call(
        functools.partial(_rwkv_proj_kernel, has_vres, seg_rows),
        out_shape=[jax.ShapeDtypeStruct((m, d), F32)] * 6,
        grid=(m // tm,),
        in_specs=in_specs,
        out_specs=[_rows(tm, d)] * 6,
        compiler_params=_cparams(("parallel",)),
        name="rwkv_proj",
    )(*args)


def _split3(x):
    hi = x.astype(BF16)
    r1 = x - hi.astype(F32)
    mid = r1.astype(BF16)
    lo = (r1 - mid.astype(F32)).astype(BF16)
    return hi, mid, lo


def _rwkv_chunk_kernel(L, r_ref, wl_ref, k_ref, v_ref, a_ref, g_ref, kk_ref, ka_ref, rk_ref,
                       gng_ref, gnb_ref, s0_ref, z_ref, sout_ref, s_scr):
    c = pl.program_id(1)

    @pl.when(c == 0)
    def _():
        s_scr[...] = s0_ref[0]

    L2 = 2 * L
    first = lax.broadcasted_iota(jnp.int32, (L, LANES), 1) < A_HEAD
    row = lax.broadcasted_iota(jnp.int32, (L, L2), 0)
    col = lax.broadcasted_iota(jnp.int32, (L, L2), 1)
    first_sq = col < L
    col = jnp.where(first_sq, col, col - L)
    strict = row > col
    incl = row >= col
    eye = jnp.where(row == col, 1.0, 0.0).astype(F32)
    tri = jnp.where(lax.broadcasted_iota(jnp.int32, (L, L), 0) >=
                    lax.broadcasted_iota(jnp.int32, (L, L), 1), 1.0, 0.0).astype(BF16)
    vrow = lax.broadcasted_iota(jnp.int32, (LANES, LANES), 0) < A_HEAD
    vcol = lax.broadcasted_iota(jnp.int32, (LANES, LANES), 1) < A_HEAD
    same_head = vrow == vcol

    def stack(x, mask=first):
        return jnp.concatenate([jnp.where(mask, x, 0.0), jnp.where(mask, 0.0, x)], axis=0).astype(BF16)

    pairs = range(PAIRS)
    sls = [slice(p * LANES, (p + 1) * LANES) for p in pairs]
    r = [r_ref[0, :, sl] for sl in sls]
    wl = [wl_ref[0, :, sl] for sl in sls]
    v = [v_ref[0, :, sl] for sl in sls]
    a = [a_ref[0, :, sl] for sl in sls]
    k_raw = [k_ref[0, :, sl] for sl in sls]
    kk = [k_raw[p] * kk_ref[:, sls[p]] for p in pairs]
    kk = [x / jnp.maximum(jnp.sqrt(_pair_sum(x * x, first)), 1e-12) for x in kk]
    k = [k_raw[p] * (1.0 + (a[p] - 1.0) * ka_ref[:, sls[p]]) for p in pairs]
    b = [kk[p] * a[p] for p in pairs]

    def cumsum(x):
        hi, mid, lo = _split3(x)
        return (jnp.dot(tri, hi, preferred_element_type=F32) + jnp.dot(tri, mid, preferred_element_type=F32)
                + jnp.dot(tri, lo, preferred_element_type=F32))

    cum = [cumsum(x) for x in wl]
    g_t = [jnp.exp(x) for x in cum]
    g_prev = [jnp.exp(cum[p] - wl[p]) for p in pairs]
    g_inv = [jnp.exp(-x) for x in cum]
    g_end = [x[L - 1:L, :] for x in g_t]
    kinv = [k[p] * g_inv[p] for p in pairs]
    binv = [b[p] * g_inv[p] for p in pairs]
    lhs = [jnp.concatenate([kk[p] * g_prev[p], r[p] * g_t[p]], axis=0).astype(BF16) for p in pairs]
    rhs = [jnp.concatenate([stack(kinv[p]), stack(binv[p])], axis=0) for p in pairs]
    s_v = [stack(x) for x in v]
    gram = [_dot_nt(lhs[p], rhs[p]) for p in pairs]
    m_k = [jnp.where(strict, x[:L, :L2], 0.0).astype(BF16) for x in gram]
    n_k = [jnp.where(incl, x[L:, :L2], 0.0).astype(BF16) for x in gram]
    n_b = [jnp.where(incl, -x[L:, L2:], 0.0).astype(BF16) for x in gram]
    x_pow = [jnp.where(strict, -x[:L, L2:], 0.0) for x in gram]
    t_inv = [eye + x for x in x_pow]
    x_bd = [stack(x, first_sq) for x in x_pow]
    n = 1
    while 2 * n < L:
        x_pow = [_dot(x_pow[p], x_bd[p]) for p in pairs]
        x_bd = [stack(x, first_sq) for x in x_pow]
        t_inv = [t_inv[p] + _dot(t_inv[p], x_bd[p]) for p in pairs]
        n *= 2
    s_prev = [s_scr[p] for p in pairs]
    from_state = [_dot_nt(lhs[p], s_prev[p]) for p in pairs]
    u = [_dot(t_inv[p], stack(from_state[p][:L] + _dot(m_k[p], s_v[p]))) for p in pairs]
    y = [from_state[p][L:] + _dot(jnp.concatenate([n_k[p], n_b[p]], axis=1),
                                  jnp.concatenate([s_v[p], stack(u[p])], axis=0)) for p in pairs]
    for p in pairs:
        outer = _dot_tn(jnp.concatenate([v[p], -u[p]], axis=0),
                        jnp.concatenate([kinv[p] * g_end[p], binv[p] * g_end[p]], axis=0))
        s_scr[p] = s_prev[p] * g_end[p] + jnp.where(same_head, outer, 0.0)
    for p in pairs:
        sl = sls[p]
        mu = _pair_sum(y[p], first) * (1.0 / A_HEAD)
        yc = y[p] - mu
        var = _pair_sum(yc * yc, first) * (1.0 / A_HEAD)
        yn = yc * lax.rsqrt(var + A_GN_EPS) * gng_ref[:, sl] + gnb_ref[:, sl]
        bonus = _pair_sum(r[p] * k[p] * rk_ref[:, sl], first) * v[p]
        z_ref[0, :, sl] = ((yn + bonus) * g_ref[0, :, sl]).astype(BF16)

    @pl.when(c == pl.num_programs(1) - 1)
    def _():
        sout_ref[0] = s_scr[...]


def _rwkv_chunk(r, wl, k, v, a, g, w, s0_pairs):
    bsz, t, d = r.shape
    L = min(RWKV_CHUNK, t)
    seq = pl.BlockSpec((1, L, d), lambda b, c: (b, c, 0))
    vec = pl.BlockSpec((1, d), lambda b, c: (0, 0))
    st = pl.BlockSpec((1, PAIRS, LANES, LANES), lambda b, c: (b, 0, 0, 0))
    return pl.pallas_call(
        functools.partial(_rwkv_chunk_kernel, L),
        out_shape=[jax.ShapeDtypeStruct((bsz, t, d), BF16),
                   jax.ShapeDtypeStruct((bsz, PAIRS, LANES, LANES), F32)],
        grid=(bsz, t // L),
        in_specs=[seq] * 6 + [vec] * 5 + [st],
        out_specs=[seq, st],
        scratch_shapes=[pltpu.VMEM((PAIRS, LANES, LANES), F32)],
        compiler_params=_cparams(("parallel", "arbitrary")),
        name="rwkv_chunk",
    )(r, wl, k, v, a, g, w["k_k"], w["k_a"], w["r_k"], w["gn_g"], w["gn_b"], s0_pairs)


def _state_to_pairs(s):
    bsz = s.shape[0]
    s = s.reshape(bsz, PAIRS, 2, A_HEAD, A_HEAD)
    z = jnp.zeros_like(s[:, :, 0])
    top = jnp.concatenate([s[:, :, 0], z], axis=-1)
    bot = jnp.concatenate([z, s[:, :, 1]], axis=-1)
    return jnp.concatenate([top, bot], axis=-2)


def _pairs_to_state(sp):
    bsz = sp.shape[0]
    s0 = sp[:, :, :A_HEAD, :A_HEAD]
    s1 = sp[:, :, A_HEAD:, A_HEAD:]
    return jnp.stack([s0, s1], axis=2).reshape(bsz, A_HEADS, A_HEAD, A_HEAD)


def _fox_proj_kernel(q_scale, x_ref, w_ref, wf_ref, bf_ref, qg_ref, kg_ref,
                     qb_o, kf_o, kb_o, vf_o, vb_o, gate_o, logf_o):
    d = D_MODEL
    xb = x_ref[...].astype(BF16)
    first = lax.broadcasted_iota(jnp.int32, (xb.shape[0], LANES), 1) < B_HEAD

    def rms(t, gain_ref, blk):
        ms = _pair_sum(t * t, first) * (1.0 / B_HEAD)
        return t * lax.rsqrt(ms + RMS_EPS) * gain_ref[:, blk]

    for j in range(PAIRS):
        blk = slice(j * LANES, (j + 1) * LANES)
        q = rms(_dot(xb, w_ref[:, j * LANES:(j + 1) * LANES]), qg_ref, blk)
        qb_o[:, blk] = (q * q_scale).astype(BF16)
        k = rms(_dot(xb, w_ref[:, d + j * LANES:d + (j + 1) * LANES]), kg_ref, blk)
        kf_o[:, blk] = k
        kb_o[:, blk] = k.astype(BF16)
    v = _dot(xb, w_ref[:, 2 * d:3 * d])
    vf_o[...] = v
    vb_o[...] = v.astype(BF16)
    gate_o[...] = _dot(xb, w_ref[:, 3 * d:4 * d])
    logf_o[...] = -_softplus(-(_dot(xb, wf_ref[...]) + bf_ref[...]))


def _fox_proj(x, w, q_scale):
    m, d = x.shape
    tm = min(ROW_TILE, m)
    outs = [jax.ShapeDtypeStruct((m, d), BF16), jax.ShapeDtypeStruct((m, d), F32),
            jax.ShapeDtypeStruct((m, d), BF16), jax.ShapeDtypeStruct((m, d), F32),
            jax.ShapeDtypeStruct((m, d), BF16), jax.ShapeDtypeStruct((m, d), F32),
            jax.ShapeDtypeStruct((m, LANES), F32)]
    return pl.pallas_call(
        functools.partial(_fox_proj_kernel, q_scale),
        out_shape=outs,
        grid=(m // tm,),
        in_specs=[_rows(tm, d), _resident((d, 4 * d)), _resident((d, LANES)), _resident((1, LANES)),
                  _resident((1, d)), _resident((1, d))],
        out_specs=[_rows(tm, d)] * 6 + [_rows(tm, LANES)],
        compiler_params=_cparams(("parallel",)),
        name="fox_proj",
    )(x, w["w_in"], w["w_f"], w["b_f"], w["q_gain"], w["k_gain"])


BIAS_PIECES = 3


def _fox_attn_kernel(tile, q_ref, k_ref, kb_ref, vt_ref, gate_ref, z_ref, m_scr, l_scr, acc_scr):
    qi = pl.program_id(2)
    lane = lax.broadcasted_iota(jnp.int32, (tile, LANES), 1)
    first = lane < B_HEAD
    ones = jnp.where(lane % B_HEAD < BIAS_PIECES, 1.0, 0.0).astype(BF16)
    q = q_ref[0]
    q_heads = (jnp.where(first, q, ones), jnp.where(first, ones, q))
    causal = (lax.broadcasted_iota(jnp.int32, (tile, tile), 0) <=
              lax.broadcasted_iota(jnp.int32, (tile, tile), 1))
    m_scr[...] = jnp.full_like(m_scr, -jnp.inf)
    l_scr[...] = jnp.zeros_like(l_scr)
    acc_scr[...] = jnp.zeros_like(acc_scr)

    def step(blocks):
        k_heads, vts = [], []
        for kb, _ in blocks:
            start = pl.multiple_of(kb * tile, tile)
            k = k_ref[0, pl.ds(start, tile), :]
            kbias = kb_ref[0, pl.ds(start, tile), :]
            k_heads.append((jnp.where(first, k, kbias), jnp.where(first, kbias, k)))
            vts.append(vt_ref[0, kb])
        m_all, l_all, acc_all = m_scr[...], l_scr[...], acc_scr[...]
        heads = range(2)
        rows = [slice(h * B_HEAD, (h + 1) * B_HEAD) for h in heads]
        s = [[_dot_nt(kh[h], q_heads[h]) for kh in k_heads] for h in heads]
        s = [[jnp.where(causal, x, -jnp.inf) if diag else x for x, (_, diag) in zip(s[h], blocks)]
             for h in heads]
        m_prev = [m_all[h:h + 1] for h in heads]
        m_new = list(m_prev)
        for h in heads:
            for x in s[h]:
                m_new[h] = jnp.maximum(m_new[h], jnp.max(x, 0, keepdims=True))
        p = [[jnp.exp2(x - m_new[h]) for x in s[h]] for h in heads]
        alpha = [jnp.exp2(m_prev[h] - m_new[h]) for h in heads]
        l_new = [alpha[h] * l_all[h:h + 1] for h in heads]
        acc_new = [alpha[h] * acc_all[rows[h]] for h in heads]
        for h in heads:
            for x, vt in zip(p[h], vts):
                l_new[h] = l_new[h] + jnp.sum(x, 0, keepdims=True)
                acc_new[h] = acc_new[h] + _dot(vt[rows[h]], x)
        m_scr[...] = jnp.concatenate(m_new, 0)
        l_scr[...] = jnp.concatenate(l_new, 0)
        acc_scr[...] = jnp.concatenate(acc_new, 0)

    @pl.loop(0, qi // 2)
    def _(j):
        step([(2 * j, False), (2 * j + 1, False)])

    @pl.when(qi % 2 == 1)
    def _():
        step([(qi - 1, False), (qi, True)])

    @pl.when(qi % 2 == 0)
    def _():
        step([(qi, True)])

    inv_l = 1.0 / l_scr[...]
    o_t = jnp.concatenate([acc_scr[:B_HEAD, :] * inv_l[0:1], acc_scr[B_HEAD:, :] * inv_l[1:2]], 0)
    z_ref[0] = (_sigmoid(gate_ref[0]) * o_t.T).astype(BF16)


def _fox_attn(q, k, k_bias, v_t, gate):
    bsz, t, d = q.shape
    tile = min(ATTN_TILE, t)
    nq = t // tile
    qspec = pl.BlockSpec((1, tile, LANES), lambda b, p, i: (b, i, p))
    kspec = pl.BlockSpec((1, t, LANES), lambda b, p, i: (b, 0, p))
    vspec = pl.BlockSpec((1, nq, LANES, tile), lambda b, p, i: (b, 0, p, 0))
    return pl.pallas_call(
        functools.partial(_fox_attn_kernel, tile),
        out_shape=jax.ShapeDtypeStruct((bsz, t, d), BF16),
        grid=(bsz, PAIRS, nq),
        in_specs=[qspec, kspec, kspec, vspec, qspec],
        out_specs=qspec,
        scratch_shapes=[pltpu.VMEM((2, tile), F32), pltpu.VMEM((2, tile), F32),
                        pltpu.VMEM((LANES, tile), F32)],
        compiler_params=_cparams(("parallel", "parallel", "arbitrary")),
        name="fox_attn",
    )(q, k, k_bias, v_t, gate)


def _fox_key_bias(c_new):
    bsz, t, h = c_new.shape
    rest = -c_new * math.log2(math.e)
    pieces = []
    for _ in range(BIAS_PIECES):
        top = lax.bitcast_convert_type(
            lax.bitcast_convert_type(rest, jnp.uint32) & jnp.uint32(0xFFFF0000), F32)
        pieces.append(top.astype(BF16))
        rest = rest - top
    packed = jnp.pad(jnp.stack(pieces, -1), ((0, 0), (0, 0), (0, 0), (0, B_HEAD - BIAS_PIECES)))
    packed = packed.reshape(bsz, t, h // 2, 2, B_HEAD)[:, :, :, ::-1, :]
    return packed.reshape(bsz, t, h * B_HEAD)


def _fox_cache_kernel(nq, q_ref, kn_ref, vn_ref, ck_ref, cv_ref, lf_ref, cn_ref, gate_ref, z_ref,
                      m_scr, l_scr, acc_scr, carry_scr):
    kb = pl.program_id(1)
    d = D_MODEL
    rows = B_HEADS * nq
    q = q_ref[0]
    row_head = lax.broadcasted_iota(jnp.int32, (rows, d), 0) // nq
    col_head = lax.broadcasted_iota(jnp.int32, (rows, d), 1) // B_HEAD
    q_rep = jnp.concatenate([q] * B_HEADS, axis=0)
    q_bd = jnp.where(row_head == col_head, q_rep, jnp.zeros_like(q_rep))

    def expand(bias):
        n = bias.shape[-1]
        return jnp.broadcast_to(bias[:, None, :], (B_HEADS, nq, n)).reshape(rows, n)

    @pl.when(kb == 0)
    def _():
        m_scr[...] = jnp.full_like(m_scr, -jnp.inf)
        l_scr[...] = jnp.zeros_like(l_scr)
        acc_scr[...] = jnp.zeros_like(acc_scr)
        carry_scr[...] = jnp.zeros_like(carry_scr)

    def update(s, v):
        m_prev = m_scr[...]
        m_new = jnp.maximum(m_prev, jnp.max(s, -1, keepdims=True))
        p = jnp.exp(s - m_new)
        alpha = jnp.exp(m_prev - m_new)
        l_scr[...] = alpha * l_scr[...] + jnp.sum(p, -1, keepdims=True)
        acc_scr[...] = alpha * acc_scr[...] + _dot(p, v)
        m_scr[...] = m_new

    cum = lf_ref[0]
    tk = cum.shape[-1]
    key = lax.broadcasted_iota(jnp.int32, cum.shape, 1)
    shift = 1
    while shift < tk:
        cum = cum + jnp.where(key >= shift, pltpu.roll(cum, shift, 1), 0.0)
        shift *= 2
    cum = cum + carry_scr[...]
    carry_scr[...] = cum[:, tk - 1:tk]
    update(_dot_nt(q_bd, ck_ref[0]) - expand(cum), cv_ref[0])

    @pl.when(kb == pl.num_programs(1) - 1)
    def _():
        s = _dot_nt(q_bd, kn_ref[0]) - expand(carry_scr[...] + cn_ref[0])
        qpos = lax.broadcasted_iota(jnp.int32, (rows, nq), 0) % nq
        kpos = lax.broadcasted_iota(jnp.int32, (rows, nq), 1)
        update(jnp.where(kpos <= qpos, s, -jnp.inf), vn_ref[0])
        o_all = acc_scr[...] / l_scr[...]
        head_of_col = lax.broadcasted_iota(jnp.int32, (nq, d), 1) // B_HEAD
        o = jnp.zeros((nq, d), F32)
        for h in range(B_HEADS):
            o = o + jnp.where(head_of_col == h, o_all[h * nq:(h + 1) * nq, :], 0.0)
        z_ref[0] = (_sigmoid(gate_ref[0]) * o).astype(BF16)


def _fox_cache_attn(q, kn, vn, ck, cv, logf_cache, c_new, gate):
    bsz, nq, d = q.shape
    plen = ck.shape[1]
    tk = min(CACHE_TILE, plen)
    rows = B_HEADS * nq
    new = pl.BlockSpec((1, nq, d), lambda b, j: (b, 0, 0))
    cache = pl.BlockSpec((1, tk, d), lambda b, j: (b, j, 0))
    return pl.pallas_call(
        functools.partial(_fox_cache_kernel, nq),
        out_shape=jax.ShapeDtypeStruct((bsz, nq, d), BF16),
        grid=(bsz, plen // tk),
        in_specs=[new, new, new, cache, cache,
                  pl.BlockSpec((1, B_HEADS, tk), lambda b, j: (b, 0, j)),
                  pl.BlockSpec((1, B_HEADS, nq), lambda b, j: (b, 0, 0)), new],
        out_specs=new,
        scratch_shapes=[pltpu.VMEM((rows, 1), F32), pltpu.VMEM((rows, 1), F32),
                        pltpu.VMEM((rows, d), F32), pltpu.VMEM((B_HEADS, 1), F32)],
        compiler_params=_cparams(("parallel", "arbitrary")),
        name="fox_cache_attn",
    )(q, kn, vn, ck, cv, logf_cache, c_new, gate)


def _ret_proj_kernel(x_ref, w_ref, cos_ref, sin_ref, q_o, k_o, v_o, g_o):
    d = D_MODEL
    xb = x_ref[...].astype(BF16)
    for h in range(C_HEADS):
        blk = slice(h * C_QK, (h + 1) * C_QK)
        cos = cos_ref[:, blk]
        sin = sin_ref[:, blk]
        q = _dot(xb, w_ref[:, h * C_QK:(h + 1) * C_QK])
        q_o[:, blk] = (q * cos + pltpu.roll(q, C_QK // 2, 1) * sin).astype(BF16)
        k = _dot(xb, w_ref[:, d + h * C_QK:d + (h + 1) * C_QK])
        k_o[:, blk] = ((k * cos + pltpu.roll(k, C_QK // 2, 1) * sin) * (C_QK ** -0.5)).astype(BF16)
    v_o[...] = _dot(xb, w_ref[:, 2 * d:4 * d]).astype(BF16)
    g_o[...] = _dot(xb, w_ref[:, 4 * d:6 * d])


def _ret_proj(x, w_in, cos, sin, table_blocks):
    m, d = x.shape
    tm = min(ROW_TILE, m)
    tab = pl.BlockSpec((tm, d), lambda i: (i % table_blocks, 0))
    return pl.pallas_call(
        _ret_proj_kernel,
        out_shape=[jax.ShapeDtypeStruct((m, d), BF16), jax.ShapeDtypeStruct((m, d), BF16),
                   jax.ShapeDtypeStruct((m, 2 * d), BF16), jax.ShapeDtypeStruct((m, 2 * d), F32)],
        grid=(m // tm,),
        in_specs=[_rows(tm, d), _resident((d, 6 * d)), tab, tab],
        out_specs=[_rows(tm, d), _rows(tm, d), _rows(tm, 2 * d), _rows(tm, 2 * d)],
        compiler_params=_cparams(("parallel",)),
        name="ret_proj",
    )(x, w_in, cos, sin)


def _ret_chunk_kernel(q_ref, k_ref, v_ref, g_ref, intra_ref, qd_ref, kd_ref, cd_ref, gng_ref, gnb_ref,
                      r0_ref, z_ref, rout_ref, r_scr):
    c = pl.program_id(1)

    @pl.when(c == 0)
    def _():
        r_scr[...] = r0_ref[0]

    heads = range(C_HEADS)
    qk = [slice(h * C_QK, (h + 1) * C_QK) for h in heads]
    vg = [slice(h * C_V, (h + 1) * C_V) for h in heads]
    q = [q_ref[0, :, sl] for sl in qk]
    k = [k_ref[0, :, sl] for sl in qk]
    v = [v_ref[0, :, sl] for sl in vg]
    state = [r_scr[h] for h in heads]
    s = [_dot_nt(q[h], k[h]) * intra_ref[h] for h in heads]
    o = [_dot(s[h], v[h]) + _dot(q[h], state[h]) * qd_ref[h] for h in heads]
    for h in heads:
        r_scr[h] = state[h] * cd_ref[h] + _dot_tn(k[h].astype(F32) * kd_ref[h], v[h])
    for h in heads:
        mu = jnp.mean(o[h], -1, keepdims=True)
        oc = o[h] - mu
        var = jnp.mean(oc * oc, -1, keepdims=True)
        on = oc * lax.rsqrt(var + LN_EPS) * gng_ref[:, vg[h]] + gnb_ref[:, vg[h]]
        g = g_ref[0, :, vg[h]]
        z_ref[0, :, vg[h]] = (g * _sigmoid(g) * on).astype(BF16)

    @pl.when(c == pl.num_programs(1) - 1)
    def _():
        rout_ref[0] = r_scr[...]


def _ret_chunk(q, k, v, g, gn_g, gn_b, r0):
    bsz, t, _ = q.shape
    L = min(RET_CHUNK, t)
    log_g = jnp.log1p(-jnp.exp2(-5.0 - jnp.arange(C_HEADS, dtype=F32)))
    idx = jnp.arange(L, dtype=F32)
    rel = idx[:, None] - idx[None, :]
    intra = jnp.where(rel[None] >= 0, jnp.exp(rel[None] * log_g[:, None, None]), 0.0)
    q_decay = jnp.exp((idx[None, :, None] + 1.0) * log_g[:, None, None])
    k_decay = jnp.exp((L - 1.0 - idx[None, :, None]) * log_g[:, None, None])
    chunk_decay = jnp.exp(L * log_g)[:, None, None]
    qk = pl.BlockSpec((1, L, D_MODEL), lambda b, c: (b, c, 0))
    vg = pl.BlockSpec((1, L, 2 * D_MODEL), lambda b, c: (b, c, 0))
    st = pl.BlockSpec((1, C_HEADS, C_QK, C_V), lambda b, c: (b, 0, 0, 0))
    return pl.pallas_call(
        _ret_chunk_kernel,
        out_shape=[jax.ShapeDtypeStruct((bsz, t, 2 * D_MODEL), BF16),
                   jax.ShapeDtypeStruct((bsz, C_HEADS, C_QK, C_V), F32)],
        grid=(bsz, t // L),
        in_specs=[qk, qk, vg, vg, _resident((C_HEADS, L, L)), _resident((C_HEADS, L, 1)),
                  _resident((C_HEADS, L, 1)), _resident((C_HEADS, 1, 1)), _resident((1, 2 * D_MODEL)),
                  _resident((1, 2 * D_MODEL)), st],
        out_specs=[vg, st],
        scratch_shapes=[pltpu.VMEM((C_HEADS, C_QK, C_V), F32)],
        compiler_params=_cparams(("parallel", "arbitrary")),
        name="ret_chunk",
    )(q, k, v, g, intra, q_decay, k_decay, chunk_decay, gn_g, gn_b, r0)


def _pad_cols(w, n):
    return jnp.pad(w, ((0, 0), (0, n - w.shape[1])))


def _pad_rows(w, n):
    return jnp.pad(w, ((0, n - w.shape[0]), (0, 0)))


def _row(vec):
    return vec.reshape(1, -1).astype(F32)


def _prep_rwkv(j, wt):
    def lora(w_in, w_out, width):
        return _pad_cols(w_in, width).astype(BF16), _pad_rows(w_out, width).astype(BF16)

    w1, w2 = lora(wt["a_w1"][j], wt["a_w2"][j], 128)
    a1, a2 = lora(wt["a_a1"][j], wt["a_a2"][j], 128)
    g1, g2 = lora(wt["a_g1"][j], wt["a_g2"][j], 256)
    out = dict(mu=wt["a_mu"][j], w_rkv=wt["a_w_rkv"][j].astype(BF16), w0=_row(wt["a_w0"][j]), w1=w1, w2=w2,
               a0=_row(wt["a_a0"][j]), a1=a1, a2=a2, g1=g1, g2=g2,
               k_k=_row(wt["a_k_k"][j]), k_a=_row(wt["a_k_a"][j]), r_k=_row(wt["a_r_k"][j]),
               gn_g=_row(wt["a_gn_g"][j]), gn_b=_row(wt["a_gn_b"][j]))
    if j > 0:
        v1, v2 = lora(wt["a_v1"][j - 1], wt["a_v2"][j - 1], 128)
        out.update(v0=_row(wt["a_v0"][j - 1]), v1=v1, v2=v2)
    return out


def _prep_fox(j, wt):
    d = D_MODEL
    w_in = wt["b_w_in"][j]
    return dict(w_in=w_in[:, :4 * d].astype(BF16), w_f=_pad_cols(w_in[:, 4 * d:], LANES).astype(BF16),
                b_f=_pad_cols(_row(wt["b_b_f"][j]), LANES),
                q_gain=_row(jnp.tile(wt["b_q_gain"][j], B_HEADS)),
                k_gain=_row(jnp.tile(wt["b_k_gain"][j], B_HEADS)))


def _rope_tables(pos):
    inv = ROPE_BASE ** (-jnp.arange(0, C_QK, 2, dtype=F32) / C_QK)
    ang = pos[:, None].astype(F32) * inv[None]
    cos, sin = jnp.cos(ang), jnp.sin(ang)
    cos_t = jnp.tile(jnp.concatenate([cos, cos], -1), (1, C_HEADS))
    sin_t = jnp.tile(jnp.concatenate([-sin, sin], -1), (1, C_HEADS))
    return cos_t, sin_t


def _run_trunk(x, ple, pos0, a_shift, a_wkv, b_cache, c_state, wt, prep):
    bsz, t, d = x.shape
    m = bsz * t
    v_first = None
    na_shift, na_wkv, nb_k, nb_v, nb_logf, nc_state = [], [], [], [], [], []
    for i in range(DEPTH):
        kind, j = i % N_MIXERS, i // N_MIXERS
        x2 = x.reshape(m, d)
        if kind == 0:
            w = prep["rwkv"][j]
            r, k, v, wl, a, g = _rwkv_proj(x, a_shift[j], v_first, w)
            if j == 0:
                v_first = v
            seq = lambda arr: arr.reshape(bsz, t, d)
            z, s_pairs = _rwkv_chunk(seq(r), seq(wl), seq(k), seq(v), seq(a), seq(g), w,
                                     _state_to_pairs(a_wkv[j].astype(F32)))
            na_shift.append(x[:, -1])
            na_wkv.append(_pairs_to_state(s_pairs))
            z = z.reshape(m, d)
            wo = prep["a_w_o"][j]
        elif kind == 1:
            w = prep["fox"][j]
            fresh = b_cache is None
            q_scale = B_HEAD ** -0.5 * (math.log2(math.e) if fresh else 1.0)
            qb, kf, kb, vf, vb, gate, logf_pad = _fox_proj(x2, w, q_scale)
            logf = logf_pad[:, :B_HEADS].reshape(bsz, t, B_HEADS)
            c_new = jnp.cumsum(logf, axis=1)
            seq = lambda arr: arr.reshape(bsz, t, d)
            if fresh:
                tile = min(ATTN_TILE, t)
                v_t = vb.reshape(bsz, t // tile, tile, d).transpose(0, 1, 3, 2)
                z = _fox_attn(seq(qb), seq(kb), _fox_key_bias(c_new), v_t, seq(gate))
            else:
                ck, cv, clogf = b_cache[0][j], b_cache[1][j], b_cache[2][j]
                plen = ck.shape[1]
                z = _fox_cache_attn(seq(qb), seq(kb), seq(vb), ck.astype(BF16).reshape(bsz, plen, d),
                                    cv.astype(BF16).reshape(bsz, plen, d),
                                    jnp.swapaxes(clogf.astype(F32), 1, 2),
                                    jnp.swapaxes(c_new, 1, 2), seq(gate))
            nb_k.append(kf.reshape(bsz, t, B_HEADS, B_HEAD))
            nb_v.append(vf.reshape(bsz, t, B_HEADS, B_HEAD))
            nb_logf.append(logf)
            z = z.reshape(m, d)
            wo = prep["b_w_o"][j]
        else:
            cos, sin = _rope_tables(pos0 + jnp.arange(t))
            tm = min(ROW_TILE, m)
            if t >= tm:
                table_blocks = t // tm
            else:
                cos, sin = jnp.tile(cos, (tm // t, 1)), jnp.tile(sin, (tm // t, 1))
                table_blocks = 1
            q, k, v, g = _ret_proj(x2, prep["c_w_in"][j], cos, sin, table_blocks)
            z, r_new = _ret_chunk(q.reshape(bsz, t, d), k.reshape(bsz, t, d), v.reshape(bsz, t, 2 * d),
                                  g.reshape(bsz, t, 2 * d), _row(wt["c_gn_g"][j]), _row(wt["c_gn_b"][j]),
                                  c_state[j].astype(F32))
            nc_state.append(r_new)
            z = z.reshape(m, 2 * d)
            wo = prep["c_w_o"][j]
        x = _post_mixer(z, x2, ple[i].reshape(m, D_PLE), wo,
                        _row(wt["ln_mix_g"][i]), _row(wt["ln_mix_b"][i]), prep["ffn_w1"][i], prep["ffn_w2"][i],
                        _row(wt["ln_ffn_g"][i]), _row(wt["ln_ffn_b"][i]), prep["ple_gate"][i],
                        prep["ple_in"][i]).reshape(bsz, t, d)
    return (x, jnp.stack(na_shift), jnp.stack(na_wkv), jnp.stack(nb_k), jnp.stack(nb_v),
            jnp.stack(nb_logf), jnp.stack(nc_state))


def kernel(x_prompt, x_sample, p_prompt, p_sample, state_a_shift, state_a_wkv, cache_b_k, cache_b_v, cache_b_logf, state_c, ln_mix_g, ln_mix_b, ln_ffn_g, ln_ffn_b, ffn_w1, ffn_w2, ple_in, ple_gate, a_mu, a_w_rkv, a_w0, a_w1, a_w2, a_a0, a_a1, a_a2, a_v0, a_v1, a_v2, a_g1, a_g2, a_k_k, a_k_a, a_r_k, a_gn_g, a_gn_b, a_w_o, b_w_in, b_b_f, b_q_gain, b_k_gain, b_w_o, c_w_in, c_gn_g, c_gn_b, c_w_o):
    wt = dict(ln_mix_g=ln_mix_g, ln_mix_b=ln_mix_b, ln_ffn_g=ln_ffn_g, ln_ffn_b=ln_ffn_b,
              a_mu=a_mu, a_w_rkv=a_w_rkv, a_w0=a_w0, a_w1=a_w1, a_w2=a_w2, a_a0=a_a0, a_a1=a_a1, a_a2=a_a2,
              a_v0=a_v0, a_v1=a_v1, a_v2=a_v2, a_g1=a_g1, a_g2=a_g2, a_k_k=a_k_k, a_k_a=a_k_a, a_r_k=a_r_k,
              a_gn_g=a_gn_g, a_gn_b=a_gn_b, b_w_in=b_w_in, b_b_f=b_b_f, b_q_gain=b_q_gain,
              b_k_gain=b_k_gain, c_gn_g=c_gn_g, c_gn_b=c_gn_b)
    n_a, n_b, n_c = a_mu.shape[0], b_w_in.shape[0], c_w_in.shape[0]
    prep = dict(
        rwkv=[_prep_rwkv(j, wt) for j in range(n_a)],
        fox=[_prep_fox(j, wt) for j in range(n_b)],
        a_w_o=a_w_o.astype(BF16), b_w_o=b_w_o.astype(BF16), c_w_o=c_w_o.astype(BF16),
        c_w_in=c_w_in.astype(BF16), ffn_w1=ffn_w1.astype(BF16), ffn_w2=ffn_w2.astype(BF16),
        ple_gate=ple_gate.astype(BF16), ple_in=ple_in.astype(BF16))
    nbat = x_prompt.shape[0]
    zero_shift = jnp.zeros((n_a, nbat, D_MODEL), x_prompt.dtype)
    zero_wkv = jnp.zeros((n_a, nbat, A_HEADS, A_HEAD, A_HEAD), F32)
    zero_ret = jnp.zeros((n_c, nbat, C_HEADS, C_QK, C_V), F32)
    past_len = cache_b_k.shape[2]
    y_prompt, pa_shift, pa_wkv, pb_k, pb_v, pb_logf, pc_state = _run_trunk(
        x_prompt, p_prompt, 0, zero_shift, zero_wkv, None, zero_ret, wt, prep)
    y_sample, sa_shift, sa_wkv, sb_k, sb_v, sb_logf, sc_state = _run_trunk(
        x_sample, p_sample, past_len, state_a_shift, state_a_wkv, (cache_b_k, cache_b_v, cache_b_logf),
        state_c, wt, prep)
    return (y_prompt, y_sample, pa_shift, pa_wkv, pb_k, pb_v, pb_logf, pc_state,
            sa_shift, sa_wkv, sb_k, sb_v, sb_logf, sc_state)
```

```python
import functools
import math

import jax
import jax.numpy as jnp
from jax import lax
from jax.experimental import pallas as pl
from jax.experimental.pallas import tpu as pltpu

F32 = jnp.float32
BF16 = jnp.bfloat16

D_MODEL = 1024
DEPTH = 4
N_MIXERS = 3
D_PLE = 256
D_FF = 4 * D_MODEL
DN_ALPHA = (2.0 * DEPTH) ** 0.25
LN_EPS = 1e-5
A_HEAD = 64
A_HEADS = D_MODEL // A_HEAD
A_GN_EPS = 64e-5
B_HEAD = 64
B_HEADS = D_MODEL // B_HEAD
C_HEADS = 8
C_QK = D_MODEL // C_HEADS
C_V = 2 * D_MODEL // C_HEADS
ROPE_BASE = 10000.0
RMS_EPS = 1e-6

LANES = 128
PAIRS = D_MODEL // LANES
VMEM_LIMIT = 56 * 1024 * 1024
ROW_TILE = 512
FFN_CHUNK = 1024
RWKV_CHUNK = 64
RWKV_SEQS = 4
RET_CHUNK = 256
ATTN_TILE = 512
CACHE_TILE = 1024


def _cparams(semantics):
    return pltpu.CompilerParams(dimension_semantics=semantics, vmem_limit_bytes=VMEM_LIMIT)


def _resident(shape):
    nd = len(shape)
    return pl.BlockSpec(shape, lambda *_: (0,) * nd, pipeline_mode=pl.Buffered(1))


def _rows(tm, width):
    return pl.BlockSpec((tm, width), lambda i: (i, 0))


def _dot(a, b):
    return jnp.dot(a.astype(BF16), b.astype(BF16), preferred_element_type=F32)


def _dot_nt(a, b):
    return lax.dot_general(a.astype(BF16), b.astype(BF16), (((1,), (1,)), ((), ())),
                           preferred_element_type=F32)


def _dot_tn(a, b):
    return lax.dot_general(a.astype(BF16), b.astype(BF16), (((0,), (0,)), ((), ())),
                           preferred_element_type=F32)


def _sigmoid(x):
    return 1.0 / (1.0 + jnp.exp(-x))


def _softplus(x):
    return jnp.maximum(x, 0.0) + jnp.log(1.0 + jnp.exp(-jnp.abs(x)))


def _layer_norm(x, g, b):
    mu = jnp.mean(x, -1, keepdims=True)
    xc = x - mu
    var = jnp.mean(xc * xc, -1, keepdims=True)
    return xc * lax.rsqrt(var + LN_EPS) * g + b


def _pair_sum(x, first):
    s0 = jnp.sum(jnp.where(first, x, 0.0), -1, keepdims=True)
    s1 = jnp.sum(jnp.where(first, 0.0, x), -1, keepdims=True)
    return jnp.where(first, s0, s1)


def _post_kernel(z_ref, x_ref, p_ref, wo_ref, g1_ref, b1_ref, w1_ref, w2_ref, g2_ref, b2_ref,
                 wg_ref, wp_ref, o_ref):
    x = x_ref[...]
    x1 = _layer_norm(DN_ALPHA * x + _dot(z_ref[...], wo_ref[...]), g1_ref[...], b1_ref[...])
    x1b = x1.astype(BF16)
    acc = jnp.zeros_like(x1)
    for f in range(D_FF // FFN_CHUNK):
        cols = slice(f * FFN_CHUNK, (f + 1) * FFN_CHUNK)
        h = jnp.maximum(_dot(x1b, w1_ref[:, cols]), 0.0)
        acc = acc + _dot(h * h, w2_ref[cols, :])
    x2 = _layer_norm(DN_ALPHA * x1 + acc, g2_ref[...], b2_ref[...])
    gate = _sigmoid(_dot(x2, wg_ref[...]))
    o_ref[...] = x2 + gate * _dot(p_ref[...], wp_ref[...])


def _post_mixer(z, x, p, wo, g1, b1, w1, w2, g2, b2, wg, wp):
    m, dz = z.shape
    tm = min(ROW_TILE, m)
    d = D_MODEL
    return pl.pallas_call(
        _post_kernel,
        out_shape=jax.ShapeDtypeStruct((m, d), F32),
        grid=(m // tm,),
        in_specs=[_rows(tm, dz), _rows(tm, d), _rows(tm, D_PLE), _resident((dz, d)),
                  _resident((1, d)), _resident((1, d)), _resident((d, D_FF)), _resident((D_FF, d)),
                  _resident((1, d)), _resident((1, d)), _resident((d, d)), _resident((D_PLE, d))],
        out_specs=_rows(tm, d),
        compiler_params=_cparams(("parallel",)),
        name="post_mixer",
    )(z, x, p, wo, g1, b1, w1, w2, g2, b2, wg, wp)


def _rwkv_proj_kernel(has_vres, seg_rows, *refs):
    if has_vres:
        (x_ref, edge_ref, vf_ref, mu_ref, wrkv_ref, w0_ref, w1_ref, w2_ref, a0_ref, a1_ref, a2_ref,
         g1_ref, g2_ref, v0_ref, v1_ref, v2_ref, r_o, k_o, v_o, wl_o, a_o, g_o) = refs
    else:
        (x_ref, edge_ref, mu_ref, wrkv_ref, w0_ref, w1_ref, w2_ref, a0_ref, a1_ref, a2_ref,
         g1_ref, g2_ref, r_o, k_o, v_o, wl_o, a_o, g_o) = refs
    x = x_ref[...]
    row = lax.broadcasted_iota(jnp.int32, x.shape, 0)
    xprev = pltpu.roll(x, 1, 0)
    for s in range(x.shape[0] // seg_rows):
        xprev = jnp.where(row == s * seg_rows, edge_ref[0, s:s + 1, :], xprev)
    dx = xprev - x
    xr, xw, xk, xv, xa, xg = [(x + dx * mu_ref[i:i + 1, :]).astype(BF16) for i in range(6)]
    r_o[...] = _dot(xr, wrkv_ref[0])
    k_o[...] = _dot(xk, wrkv_ref[1])
    v = _dot(xv, wrkv_ref[2])
    if has_vres:
        mix = _sigmoid(v0_ref[...] + _dot(_dot(xv, v1_ref[...]), v2_ref[...]))
        v = v + (vf_ref[...] - v) * mix
    v_o[...] = v
    pre = w0_ref[...] + _dot(jnp.tanh(_dot(xw, w1_ref[...])), w2_ref[...])
    w_log = -_softplus(-pre) - 0.5
    wl_o[...] = -jnp.exp(w_log)
    a_o[...] = _sigmoid(a0_ref[...] + _dot(_dot(xa, a1_ref[...]), a2_ref[...]))
    g_o[...] = _dot(_sigmoid(_dot(xg, g1_ref[...])), g2_ref[...])


def _rwkv_proj(x, shift, vfirst, w):
    bsz, t, d = x.shape
    m = bsz * t
    tm = min(ROW_TILE, m)
    seg_rows = min(tm, t)
    seg_last = x.reshape(bsz, t // seg_rows, seg_rows, d)[:, :, -1]
    edge = jnp.concatenate([shift[:, None, :].astype(F32), seg_last[:, :-1]], axis=1)
    edge = edge.reshape(m // tm, tm // seg_rows, d)
    sublanes = 8
    if edge.shape[1] < sublanes:
        edge = jnp.pad(edge, ((0, 0), (0, sublanes - edge.shape[1]), (0, 0)))
    x = x.reshape(m, d)
    has_vres = vfirst is not None
    args = [x, edge] + ([vfirst] if has_vres else []) + [
        w["mu"], w["w_rkv"], w["w0"], w["w1"], w["w2"], w["a0"], w["a1"], w["a2"], w["g1"], w["g2"]]
    if has_vres:
        args += [w["v0"], w["v1"], w["v2"]]
    edge_spec = pl.BlockSpec((1,) + edge.shape[1:], lambda i: (i, 0, 0))
    n_rows = 3 if has_vres else 2
    in_specs = ([_rows(tm, d), edge_spec] + ([_rows(tm, d)] if has_vres else [])
                + [_resident(a.shape) for a in args[n_rows:]])
    return pl.pallas_call(
        functools.partial(_rwkv_proj_kernel, has_vres, seg_rows),
        out_shape=[jax.ShapeDtypeStruct((m, d), F32)] * 6,
        grid=(m // tm,),
        in_specs=in_specs,
        out_specs=[_rows(tm, d)] * 6,
        compiler_params=_cparams(("parallel",)),
        name="rwkv_proj",
    )(*args)


def _split3(x):
    hi = x.astype(BF16)
    r1 = x - hi.astype(F32)
    mid = r1.astype(BF16)
    lo = (r1 - mid.astype(F32)).astype(BF16)
    return hi, mid, lo


def _rwkv_chunk_kernel(L, nb, r_ref, wl_ref, k_ref, v_ref, a_ref, g_ref, kk_ref, ka_ref, rk_ref,
                       gng_ref, gnb_ref, s0_ref, z_ref, sout_ref, s_scr):
    c = pl.program_id(1)

    @pl.when(c == 0)
    def _():
        s_scr[...] = s0_ref[...]

    L2 = 2 * L
    first = lax.broadcasted_iota(jnp.int32, (L, LANES), 1) < A_HEAD
    row = lax.broadcasted_iota(jnp.int32, (L, L2), 0)
    col = lax.broadcasted_iota(jnp.int32, (L, L2), 1)
    first_sq = col < L
    col = jnp.where(first_sq, col, col - L)
    strict = row > col
    incl = row >= col
    eye = jnp.where(row == col, 1.0, 0.0).astype(F32)
    tri = jnp.where(lax.broadcasted_iota(jnp.int32, (L, L), 0) >=
                    lax.broadcasted_iota(jnp.int32, (L, L), 1), 1.0, 0.0).astype(BF16)
    vrow = lax.broadcasted_iota(jnp.int32, (LANES, LANES), 0) < A_HEAD
    vcol = lax.broadcasted_iota(jnp.int32, (LANES, LANES), 1) < A_HEAD
    same_head = vrow == vcol

    def stack(x, mask=first):
        return jnp.concatenate([jnp.where(mask, x, 0.0), jnp.where(mask, 0.0, x)], axis=0).astype(BF16)

    units = [(bi, p) for bi in range(nb) for p in range(PAIRS)]
    pairs = range(len(units))
    sls = [slice(p * LANES, (p + 1) * LANES) for _, p in units]
    r = [r_ref[bi, :, sls[i]] for i, (bi, _) in enumerate(units)]
    wl = [wl_ref[bi, :, sls[i]] for i, (bi, _) in enumerate(units)]
    v = [v_ref[bi, :, sls[i]] for i, (bi, _) in enumerate(units)]
    a = [a_ref[bi, :, sls[i]] for i, (bi, _) in enumerate(units)]
    k_raw = [k_ref[bi, :, sls[i]] for i, (bi, _) in enumerate(units)]
    kk = [k_raw[p] * kk_ref[:, sls[p]] for p in pairs]
    kk = [x / jnp.maximum(jnp.sqrt(_pair_sum(x * x, first)), 1e-12) for x in kk]
    k = [k_raw[p] * (1.0 + (a[p] - 1.0) * ka_ref[:, sls[p]]) for p in pairs]
    b = [kk[p] * a[p] for p in pairs]

    def cumsum(x):
        hi, mid, lo = _split3(x)
        return (jnp.dot(tri, hi, preferred_element_type=F32) + jnp.dot(tri, mid, preferred_element_type=F32)
                + jnp.dot(tri, lo, preferred_element_type=F32))

    cum = [cumsum(x) for x in wl]
    g_t = [jnp.exp(x) for x in cum]
    g_prev = [jnp.exp(cum[p] - wl[p]) for p in pairs]
    g_inv = [jnp.exp(-x) for x in cum]
    g_end = [x[L - 1:L, :] for x in g_t]
    kinv = [k[p] * g_inv[p] for p in pairs]
    binv = [b[p] * g_inv[p] for p in pairs]
    lhs = [jnp.concatenate([kk[p] * g_prev[p], r[p] * g_t[p]], axis=0).astype(BF16) for p in pairs]
    rhs = [jnp.concatenate([stack(kinv[p]), stack(binv[p])], axis=0) for p in pairs]
    s_v = [stack(x) for x in v]
    gram = [_dot_nt(lhs[p], rhs[p]) for p in pairs]
    m_k = [jnp.where(strict, x[:L, :L2], 0.0).astype(BF16) for x in gram]
    n_k = [jnp.where(incl, x[L:, :L2], 0.0).astype(BF16) for x in gram]
    n_b = [jnp.where(incl, -x[L:, L2:], 0.0).astype(BF16) for x in gram]
    x_pow = [jnp.where(strict, -x[:L, L2:], 0.0) for x in gram]
    t_inv = [eye + x for x in x_pow]
    x_bd = [stack(x, first_sq) for x in x_pow]
    n = 1
    while 2 * n < L:
        x_pow = [_dot(x_pow[p], x_bd[p]) for p in pairs]
        x_bd = [stack(x, first_sq) for x in x_pow]
        t_inv = [t_inv[p] + _dot(t_inv[p], x_bd[p]) for p in pairs]
        n *= 2
    s_prev = [s_scr[bi, p] for bi, p in units]
    from_state = [_dot_nt(lhs[p], s_prev[p]) for p in pairs]
    y_part = [from_state[p][L:] + _dot(n_k[p], s_v[p]) for p in pairs]
    r_bd = [stack(from_state[p][:L] + _dot(m_k[p], s_v[p])) for p in pairs]
    nb_t = [_dot(n_b[p], stack(t_inv[p], first_sq)) for p in pairs]
    both = [_dot(jnp.concatenate([t_inv[p], nb_t[p]], axis=0), r_bd[p]) for p in pairs]
    u = [x[:L] for x in both]
    y = [y_part[p] + both[p][L:] for p in pairs]
    vu = [jnp.concatenate([v[p], -u[p]], axis=0).astype(BF16) for p in pairs]
    ends = [jnp.concatenate([kinv[p] * g_end[p], binv[p] * g_end[p]], axis=0).astype(BF16) for p in pairs]
    outer = [_dot_tn(vu[p], ends[p]) for p in pairs]
    s_new = [s_prev[p] * g_end[p] + jnp.where(same_head, outer[p], 0.0) for p in pairs]
    for i, (bi, p) in enumerate(units):
        s_scr[bi, p] = s_new[i]
    for i, (bi, _) in enumerate(units):
        sl = sls[i]
        mu = _pair_sum(y[i], first) * (1.0 / A_HEAD)
        yc = y[i] - mu
        var = _pair_sum(yc * yc, first) * (1.0 / A_HEAD)
        yn = yc * lax.rsqrt(var + A_GN_EPS) * gng_ref[:, sl] + gnb_ref[:, sl]
        bonus = _pair_sum(r[i] * k[i] * rk_ref[:, sl], first) * v[i]
        z_ref[bi, :, sl] = ((yn + bonus) * g_ref[bi, :, sl]).astype(BF16)

    @pl.when(c == pl.num_programs(1) - 1)
    def _():
        sout_ref[...] = s_scr[...]


def _rwkv_chunk(r, wl, k, v, a, g, w, s0_pairs):
    bsz, t, d = r.shape
    L = min(RWKV_CHUNK, t)
    nb = math.gcd(bsz, RWKV_SEQS)
    seq = pl.BlockSpec((nb, L, d), lambda b, c: (b, c, 0))
    vec = pl.BlockSpec((1, d), lambda b, c: (0, 0))
    st = pl.BlockSpec((nb, PAIRS, LANES, LANES), lambda b, c: (b, 0, 0, 0))
    return pl.pallas_call(
        functools.partial(_rwkv_chunk_kernel, L, nb),
        out_shape=[jax.ShapeDtypeStruct((bsz, t, d), BF16),
                   jax.ShapeDtypeStruct((bsz, PAIRS, LANES, LANES), F32)],
        grid=(bsz // nb, t // L),
        in_specs=[seq] * 6 + [vec] * 5 + [st],
        out_specs=[seq, st],
        scratch_shapes=[pltpu.VMEM((nb, PAIRS, LANES, LANES), F32)],
        compiler_params=_cparams(("parallel", "arbitrary")),
        name="rwkv_chunk",
    )(r, wl, k, v, a, g, w["k_k"], w["k_a"], w["r_k"], w["gn_g"], w["gn_b"], s0_pairs)


def _state_to_pairs(s):
    bsz = s.shape[0]
    s = s.reshape(bsz, PAIRS, 2, A_HEAD, A_HEAD)
    z = jnp.zeros_like(s[:, :, 0])
    top = jnp.concatenate([s[:, :, 0], z], axis=-1)
    bot = jnp.concatenate([z, s[:, :, 1]], axis=-1)
    return jnp.concatenate([top, bot], axis=-2)


def _pairs_to_state(sp):
    bsz = sp.shape[0]
    s0 = sp[:, :, :A_HEAD, :A_HEAD]
    s1 = sp[:, :, A_HEAD:, A_HEAD:]
    return jnp.stack([s0, s1], axis=2).reshape(bsz, A_HEADS, A_HEAD, A_HEAD)


def _fox_proj_kernel(q_scale, x_ref, w_ref, wf_ref, bf_ref, qg_ref, kg_ref,
                     qb_o, kf_o, kb_o, vf_o, vb_o, gate_o, logf_o):
    d = D_MODEL
    xb = x_ref[...].astype(BF16)
    first = lax.broadcasted_iota(jnp.int32, (xb.shape[0], LANES), 1) < B_HEAD

    def rms(t, gain_ref, blk):
        ms = _pair_sum(t * t, first) * (1.0 / B_HEAD)
        return t * lax.rsqrt(ms + RMS_EPS) * gain_ref[:, blk]

    for j in range(PAIRS):
        blk = slice(j * LANES, (j + 1) * LANES)
        q = rms(_dot(xb, w_ref[:, j * LANES:(j + 1) * LANES]), qg_ref, blk)
        qb_o[:, blk] = (q * q_scale).astype(BF16)
        k = rms(_dot(xb, w_ref[:, d + j * LANES:d + (j + 1) * LANES]), kg_ref, blk)
        kf_o[:, blk] = k
        kb_o[:, blk] = k.astype(BF16)
    v = _dot(xb, w_ref[:, 2 * d:3 * d])
    vf_o[...] = v
    vb_o[...] = v.astype(BF16)
    gate_o[...] = _dot(xb, w_ref[:, 3 * d:4 * d])
    logf_o[...] = -_softplus(-(_dot(xb, wf_ref[...]) + bf_ref[...]))


def _fox_proj(x, w, q_scale):
    m, d = x.shape
    tm = min(ROW_TILE, m)
    outs = [jax.ShapeDtypeStruct((m, d), BF16), jax.ShapeDtypeStruct((m, d), F32),
            jax.ShapeDtypeStruct((m, d), BF16), jax.ShapeDtypeStruct((m, d), F32),
            jax.ShapeDtypeStruct((m, d), BF16), jax.ShapeDtypeStruct((m, d), F32),
            jax.ShapeDtypeStruct((m, LANES), F32)]
    return pl.pallas_call(
        functools.partial(_fox_proj_kernel, q_scale),
        out_shape=outs,
        grid=(m // tm,),
        in_specs=[_rows(tm, d), _resident((d, 4 * d)), _resident((d, LANES)), _resident((1, LANES)),
                  _resident((1, d)), _resident((1, d))],
        out_specs=[_rows(tm, d)] * 6 + [_rows(tm, LANES)],
        compiler_params=_cparams(("parallel",)),
        name="fox_proj",
    )(x, w["w_in"], w["w_f"], w["b_f"], w["q_gain"], w["k_gain"])


BIAS_PIECES = 3


def _fox_attn_kernel(tile, q_ref, k_ref, kb_ref, vt_ref, gate_ref, z_ref, m_scr, l_scr, acc_scr):
    qi = pl.program_id(2)
    lane = lax.broadcasted_iota(jnp.int32, (tile, LANES), 1)
    first = lane < B_HEAD
    ones = jnp.where(lane % B_HEAD < BIAS_PIECES, 1.0, 0.0).astype(BF16)
    q = q_ref[0]
    q_heads = (jnp.where(first, q, ones), jnp.where(first, ones, q))
    causal = (lax.broadcasted_iota(jnp.int32, (tile, tile), 0) <=
              lax.broadcasted_iota(jnp.int32, (tile, tile), 1))
    m_scr[...] = jnp.full_like(m_scr, -jnp.inf)
    l_scr[...] = jnp.zeros_like(l_scr)
    acc_scr[...] = jnp.zeros_like(acc_scr)

    def step(blocks):
        k_heads, vts = [], []
        for kb, _ in blocks:
            start = pl.multiple_of(kb * tile, tile)
            k = k_ref[0, pl.ds(start, tile), :]
            kbias = kb_ref[0, pl.ds(start, tile), :]
            k_heads.append((jnp.where(first, k, kbias), jnp.where(first, kbias, k)))
            vts.append(vt_ref[0, kb])
        m_all, l_all, acc_all = m_scr[...], l_scr[...], acc_scr[...]
        heads = range(2)
        rows = [slice(h * B_HEAD, (h + 1) * B_HEAD) for h in heads]
        s = [[_dot_nt(kh[h], q_heads[h]) for kh in k_heads] for h in heads]
        s = [[jnp.where(causal, x, -jnp.inf) if diag else x for x, (_, diag) in zip(s[h], blocks)]
             for h in heads]
        m_prev = [m_all[h:h + 1] for h in heads]
        m_new = list(m_prev)
        for h in heads:
            for x in s[h]:
                m_new[h] = jnp.maximum(m_new[h], jnp.max(x, 0, keepdims=True))
        p = [[jnp.exp2(x - m_new[h]) for x in s[h]] for h in heads]
        alpha = [jnp.exp2(m_prev[h] - m_new[h]) for h in heads]
        l_new = [alpha[h] * l_all[h:h + 1] for h in heads]
        acc_new = [alpha[h] * acc_all[rows[h]] for h in heads]
        for h in heads:
            for x, vt in zip(p[h], vts):
                l_new[h] = l_new[h] + jnp.sum(x, 0, keepdims=True)
                acc_new[h] = acc_new[h] + _dot(vt[rows[h]], x)
        m_scr[...] = jnp.concatenate(m_new, 0)
        l_scr[...] = jnp.concatenate(l_new, 0)
        acc_scr[...] = jnp.concatenate(acc_new, 0)

    @pl.loop(0, qi // 2)
    def _(j):
        step([(2 * j, False), (2 * j + 1, False)])

    @pl.when(qi % 2 == 1)
    def _():
        step([(qi - 1, False), (qi, True)])

    @pl.when(qi % 2 == 0)
    def _():
        step([(qi, True)])

    inv_l = 1.0 / l_scr[...]
    o_t = jnp.concatenate([acc_scr[:B_HEAD, :] * inv_l[0:1], acc_scr[B_HEAD:, :] * inv_l[1:2]], 0)
    z_ref[0] = (_sigmoid(gate_ref[0]) * o_t.T).astype(BF16)


def _fox_attn(q, k, k_bias, v_t, gate):
    bsz, t, d = q.shape
    tile = min(ATTN_TILE, t)
    nq = t // tile
    qspec = pl.BlockSpec((1, tile, LANES), lambda b, p, i: (b, i, p))
    kspec = pl.BlockSpec((1, t, LANES), lambda b, p, i: (b, 0, p))
    vspec = pl.BlockSpec((1, nq, LANES, tile), lambda b, p, i: (b, 0, p, 0))
    return pl.pallas_call(
        functools.partial(_fox_attn_kernel, tile),
        out_shape=jax.ShapeDtypeStruct((bsz, t, d), BF16),
        grid=(bsz, PAIRS, nq),
        in_specs=[qspec, kspec, kspec, vspec, qspec],
        out_specs=qspec,
        scratch_shapes=[pltpu.VMEM((2, tile), F32), pltpu.VMEM((2, tile), F32),
                        pltpu.VMEM((LANES, tile), F32)],
        compiler_params=_cparams(("parallel", "parallel", "arbitrary")),
        name="fox_attn",
    )(q, k, k_bias, v_t, gate)


def _fox_key_bias(c_new):
    bsz, t, h = c_new.shape
    rest = -c_new * math.log2(math.e)
    pieces = []
    for _ in range(BIAS_PIECES):
        top = lax.bitcast_convert_type(
            lax.bitcast_convert_type(rest, jnp.uint32) & jnp.uint32(0xFFFF0000), F32)
        pieces.append(top.astype(BF16))
        rest = rest - top
    packed = jnp.pad(jnp.stack(pieces, -1), ((0, 0), (0, 0), (0, 0), (0, B_HEAD - BIAS_PIECES)))
    packed = packed.reshape(bsz, t, h // 2, 2, B_HEAD)[:, :, :, ::-1, :]
    return packed.reshape(bsz, t, h * B_HEAD)


def _fox_cache_kernel(nq, q_ref, kn_ref, vn_ref, ck_ref, cv_ref, lf_ref, cn_ref, gate_ref, z_ref,
                      m_scr, l_scr, acc_scr, carry_scr):
    kb = pl.program_id(1)
    d = D_MODEL
    rows = B_HEADS * nq
    q = q_ref[0]
    row_head = lax.broadcasted_iota(jnp.int32, (rows, d), 0) // nq
    col_head = lax.broadcasted_iota(jnp.int32, (rows, d), 1) // B_HEAD
    q_rep = jnp.concatenate([q] * B_HEADS, axis=0)
    q_bd = jnp.where(row_head == col_head, q_rep, jnp.zeros_like(q_rep))

    def expand(bias):
        n = bias.shape[-1]
        return jnp.broadcast_to(bias[:, None, :], (B_HEADS, nq, n)).reshape(rows, n)

    @pl.when(kb == 0)
    def _():
        m_scr[...] = jnp.full_like(m_scr, -jnp.inf)
        l_scr[...] = jnp.zeros_like(l_scr)
        acc_scr[...] = jnp.zeros_like(acc_scr)
        carry_scr[...] = jnp.zeros_like(carry_scr)

    def update(s, v):
        m_prev = m_scr[...]
        m_new = jnp.maximum(m_prev, jnp.max(s, -1, keepdims=True))
        p = jnp.exp(s - m_new)
        alpha = jnp.exp(m_prev - m_new)
        l_scr[...] = alpha * l_scr[...] + jnp.sum(p, -1, keepdims=True)
        acc_scr[...] = alpha * acc_scr[...] + _dot(p, v)
        m_scr[...] = m_new

    cum = lf_ref[0]
    tk = cum.shape[-1]
    key = lax.broadcasted_iota(jnp.int32, cum.shape, 1)
    shift = 1
    while shift < tk:
        cum = cum + jnp.where(key >= shift, pltpu.roll(cum, shift, 1), 0.0)
        shift *= 2
    cum = cum + carry_scr[...]
    carry_scr[...] = cum[:, tk - 1:tk]
    update(_dot_nt(q_bd, ck_ref[0]) - expand(cum), cv_ref[0])

    @pl.when(kb == pl.num_programs(1) - 1)
    def _():
        s = _dot_nt(q_bd, kn_ref[0]) - expand(carry_scr[...] + cn_ref[0])
        qpos = lax.broadcasted_iota(jnp.int32, (rows, nq), 0) % nq
        kpos = lax.broadcasted_iota(jnp.int32, (rows, nq), 1)
        update(jnp.where(kpos <= qpos, s, -jnp.inf), vn_ref[0])
        o_all = acc_scr[...] / l_scr[...]
        head_of_col = lax.broadcasted_iota(jnp.int32, (nq, d), 1) // B_HEAD
        o = jnp.zeros((nq, d), F32)
        for h in range(B_HEADS):
            o = o + jnp.where(head_of_col == h, o_all[h * nq:(h + 1) * nq, :], 0.0)
        z_ref[0] = (_sigmoid(gate_ref[0]) * o).astype(BF16)


def _fox_cache_attn(q, kn, vn, ck, cv, logf_cache, c_new, gate):
    bsz, nq, d = q.shape
    plen = ck.shape[1]
    tk = min(CACHE_TILE, plen)
    rows = B_HEADS * nq
    new = pl.BlockSpec((1, nq, d), lambda b, j: (b, 0, 0))
    cache = pl.BlockSpec((1, tk, d), lambda b, j: (b, j, 0))
    return pl.pallas_call(
        functools.partial(_fox_cache_kernel, nq),
        out_shape=jax.ShapeDtypeStruct((bsz, nq, d), BF16),
        grid=(bsz, plen // tk),
        in_specs=[new, new, new, cache, cache,
                  pl.BlockSpec((1, B_HEADS, tk), lambda b, j: (b, 0, j)),
                  pl.BlockSpec((1, B_HEADS, nq), lambda b, j: (b, 0, 0)), new],
        out_specs=new,
        scratch_shapes=[pltpu.VMEM((rows, 1), F32), pltpu.VMEM((rows, 1), F32),
                        pltpu.VMEM((rows, d), F32), pltpu.VMEM((B_HEADS, 1), F32)],
        compiler_params=_cparams(("parallel", "arbitrary")),
        name="fox_cache_attn",
    )(q, kn, vn, ck, cv, logf_cache, c_new, gate)


def _ret_proj_kernel(x_ref, w_ref, cos_ref, sin_ref, q_o, k_o, v_o, g_o):
    d = D_MODEL
    xb = x_ref[...].astype(BF16)
    for h in range(C_HEADS):
        blk = slice(h * C_QK, (h + 1) * C_QK)
        cos = cos_ref[:, blk]
        sin = sin_ref[:, blk]
        q = _dot(xb, w_ref[:, h * C_QK:(h + 1) * C_QK])
        q_o[:, blk] = (q * cos + pltpu.roll(q, C_QK // 2, 1) * sin).astype(BF16)
        k = _dot(xb, w_ref[:, d + h * C_QK:d + (h + 1) * C_QK])
        k_o[:, blk] = ((k * cos + pltpu.roll(k, C_QK // 2, 1) * sin) * (C_QK ** -0.5)).astype(BF16)
    v_o[...] = _dot(xb, w_ref[:, 2 * d:4 * d]).astype(BF16)
    g_o[...] = _dot(xb, w_ref[:, 4 * d:6 * d])


def _ret_proj(x, w_in, cos, sin, table_blocks):
    m, d = x.shape
    tm = min(ROW_TILE, m)
    tab = pl.BlockSpec((tm, d), lambda i: (i % table_blocks, 0))
    return pl.pallas_call(
        _ret_proj_kernel,
        out_shape=[jax.ShapeDtypeStruct((m, d), BF16), jax.ShapeDtypeStruct((m, d), BF16),
                   jax.ShapeDtypeStruct((m, 2 * d), BF16), jax.ShapeDtypeStruct((m, 2 * d), F32)],
        grid=(m // tm,),
        in_specs=[_rows(tm, d), _resident((d, 6 * d)), tab, tab],
        out_specs=[_rows(tm, d), _rows(tm, d), _rows(tm, 2 * d), _rows(tm, 2 * d)],
        compiler_params=_cparams(("parallel",)),
        name="ret_proj",
    )(x, w_in, cos, sin)


def _ret_chunk_kernel(q_ref, k_ref, v_ref, g_ref, intra_ref, qd_ref, kd_ref, cd_ref, gng_ref, gnb_ref,
                      r0_ref, z_ref, rout_ref, r_scr):
    c = pl.program_id(1)

    @pl.when(c == 0)
    def _():
        r_scr[...] = r0_ref[0]

    heads = range(C_HEADS)
    qk = [slice(h * C_QK, (h + 1) * C_QK) for h in heads]
    vg = [slice(h * C_V, (h + 1) * C_V) for h in heads]
    q = [q_ref[0, :, sl] for sl in qk]
    k = [k_ref[0, :, sl] for sl in qk]
    v = [v_ref[0, :, sl] for sl in vg]
    state = [r_scr[h] for h in heads]
    s = [_dot_nt(q[h], k[h]) * intra_ref[h] for h in heads]
    o = [_dot(s[h], v[h]) + _dot(q[h], state[h]) * qd_ref[h] for h in heads]
    for h in heads:
        r_scr[h] = state[h] * cd_ref[h] + _dot_tn(k[h].astype(F32) * kd_ref[h], v[h])
    for h in heads:
        mu = jnp.mean(o[h], -1, keepdims=True)
        oc = o[h] - mu
        var = jnp.mean(oc * oc, -1, keepdims=True)
        on = oc * lax.rsqrt(var + LN_EPS) * gng_ref[:, vg[h]] + gnb_ref[:, vg[h]]
        g = g_ref[0, :, vg[h]]
        z_ref[0, :, vg[h]] = (g * _sigmoid(g) * on).astype(BF16)

    @pl.when(c == pl.num_programs(1) - 1)
    def _():
        rout_ref[0] = r_scr[...]


def _ret_chunk(q, k, v, g, gn_g, gn_b, r0):
    bsz, t, _ = q.shape
    L = min(RET_CHUNK, t)
    log_g = jnp.log1p(-jnp.exp2(-5.0 - jnp.arange(C_HEADS, dtype=F32)))
    idx = jnp.arange(L, dtype=F32)
    rel = idx[:, None] - idx[None, :]
    intra = jnp.where(rel[None] >= 0, jnp.exp(rel[None] * log_g[:, None, None]), 0.0)
    q_decay = jnp.exp((idx[None, :, None] + 1.0) * log_g[:, None, None])
    k_decay = jnp.exp((L - 1.0 - idx[None, :, None]) * log_g[:, None, None])
    chunk_decay = jnp.exp(L * log_g)[:, None, None]
    qk = pl.BlockSpec((1, L, D_MODEL), lambda b, c: (b, c, 0))
    vg = pl.BlockSpec((1, L, 2 * D_MODEL), lambda b, c: (b, c, 0))
    st = pl.BlockSpec((1, C_HEADS, C_QK, C_V), lambda b, c: (b, 0, 0, 0))
    return pl.pallas_call(
        _ret_chunk_kernel,
        out_shape=[jax.ShapeDtypeStruct((bsz, t, 2 * D_MODEL), BF16),
                   jax.ShapeDtypeStruct((bsz, C_HEADS, C_QK, C_V), F32)],
        grid=(bsz, t // L),
        in_specs=[qk, qk, vg, vg, _resident((C_HEADS, L, L)), _resident((C_HEADS, L, 1)),
                  _resident((C_HEADS, L, 1)), _resident((C_HEADS, 1, 1)), _resident((1, 2 * D_MODEL)),
                  _resident((1, 2 * D_MODEL)), st],
        out_specs=[vg, st],
        scratch_shapes=[pltpu.VMEM((C_HEADS, C_QK, C_V), F32)],
        compiler_params=_cparams(("parallel", "arbitrary")),
        name="ret_chunk",
    )(q, k, v, g, intra, q_decay, k_decay, chunk_decay, gn_g, gn_b, r0)


def _pad_cols(w, n):
    return jnp.pad(w, ((0, 0), (0, n - w.shape[1])))


def _pad_rows(w, n):
    return jnp.pad(w, ((0, n - w.shape[0]), (0, 0)))


def _row(vec):
    return vec.reshape(1, -1).astype(F32)


def _prep_rwkv(j, wt):
    def lora(w_in, w_out, width):
        return _pad_cols(w_in, width).astype(BF16), _pad_rows(w_out, width).astype(BF16)

    w1, w2 = lora(wt["a_w1"][j], wt["a_w2"][j], 128)
    a1, a2 = lora(wt["a_a1"][j], wt["a_a2"][j], 128)
    g1, g2 = lora(wt["a_g1"][j], wt["a_g2"][j], 256)
    out = dict(mu=wt["a_mu"][j], w_rkv=wt["a_w_rkv"][j].astype(BF16), w0=_row(wt["a_w0"][j]), w1=w1, w2=w2,
               a0=_row(wt["a_a0"][j]), a1=a1, a2=a2, g1=g1, g2=g2,
               k_k=_row(wt["a_k_k"][j]), k_a=_row(wt["a_k_a"][j]), r_k=_row(wt["a_r_k"][j]),
               gn_g=_row(wt["a_gn_g"][j]), gn_b=_row(wt["a_gn_b"][j]))
    if j > 0:
        v1, v2 = lora(wt["a_v1"][j - 1], wt["a_v2"][j - 1], 128)
        out.update(v0=_row(wt["a_v0"][j - 1]), v1=v1, v2=v2)
    return out


def _prep_fox(j, wt):
    d = D_MODEL
    w_in = wt["b_w_in"][j]
    return dict(w_in=w_in[:, :4 * d].astype(BF16), w_f=_pad_cols(w_in[:, 4 * d:], LANES).astype(BF16),
                b_f=_pad_cols(_row(wt["b_b_f"][j]), LANES),
                q_gain=_row(jnp.tile(wt["b_q_gain"][j], B_HEADS)),
                k_gain=_row(jnp.tile(wt["b_k_gain"][j], B_HEADS)))


def _rope_tables(pos):
    inv = ROPE_BASE ** (-jnp.arange(0, C_QK, 2, dtype=F32) / C_QK)
    ang = pos[:, None].astype(F32) * inv[None]
    cos, sin = jnp.cos(ang), jnp.sin(ang)
    cos_t = jnp.tile(jnp.concatenate([cos, cos], -1), (1, C_HEADS))
    sin_t = jnp.tile(jnp.concatenate([-sin, sin], -1), (1, C_HEADS))
    return cos_t, sin_t


def _run_trunk(x, ple, pos0, a_shift, a_wkv, b_cache, c_state, wt, prep):
    bsz, t, d = x.shape
    m = bsz * t
    v_first = None
    na_shift, na_wkv, nb_k, nb_v, nb_logf, nc_state = [], [], [], [], [], []
    for i in range(DEPTH):
        kind, j = i % N_MIXERS, i // N_MIXERS
        x2 = x.reshape(m, d)
        if kind == 0:
            w = prep["rwkv"][j]
            r, k, v, wl, a, g = _rwkv_proj(x, a_shift[j], v_first, w)
            if j == 0:
                v_first = v
            seq = lambda arr: arr.reshape(bsz, t, d)
            z, s_pairs = _rwkv_chunk(seq(r), seq(wl), seq(k), seq(v), seq(a), seq(g), w,
                                     _state_to_pairs(a_wkv[j].astype(F32)))
            na_shift.append(x[:, -1])
            na_wkv.append(_pairs_to_state(s_pairs))
            z = z.reshape(m, d)
            wo = prep["a_w_o"][j]
        elif kind == 1:
            w = prep["fox"][j]
            fresh = b_cache is None
            q_scale = B_HEAD ** -0.5 * (math.log2(math.e) if fresh else 1.0)
            qb, kf, kb, vf, vb, gate, logf_pad = _fox_proj(x2, w, q_scale)
            logf = logf_pad[:, :B_HEADS].reshape(bsz, t, B_HEADS)
            c_new = jnp.cumsum(logf, axis=1)
            seq = lambda arr: arr.reshape(bsz, t, d)
            if fresh:
                tile = min(ATTN_TILE, t)
                v_t = vb.reshape(bsz, t // tile, tile, d).transpose(0, 1, 3, 2)
                z = _fox_attn(seq(qb), seq(kb), _fox_key_bias(c_new), v_t, seq(gate))
            else:
                ck, cv, clogf = b_cache[0][j], b_cache[1][j], b_cache[2][j]
                plen = ck.shape[1]
                z = _fox_cache_attn(seq(qb), seq(kb), seq(vb), ck.reshape(bsz, plen, d),
                                    cv.reshape(bsz, plen, d), jnp.swapaxes(clogf.astype(F32), 1, 2),
                                    jnp.swapaxes(c_new, 1, 2), seq(gate))
            nb_k.append(kf.reshape(bsz, t, B_HEADS, B_HEAD))
            nb_v.append(vf.reshape(bsz, t, B_HEADS, B_HEAD))
            nb_logf.append(logf)
            z = z.reshape(m, d)
            wo = prep["b_w_o"][j]
        else:
            cos, sin = _rope_tables(pos0 + jnp.arange(t))
            tm = min(ROW_TILE, m)
            if t >= tm:
                table_blocks = t // tm
            else:
                cos, sin = jnp.tile(cos, (tm // t, 1)), jnp.tile(sin, (tm // t, 1))
                table_blocks = 1
            q, k, v, g = _ret_proj(x2, prep["c_w_in"][j], cos, sin, table_blocks)
            z, r_new = _ret_chunk(q.reshape(bsz, t, d), k.reshape(bsz, t, d), v.reshape(bsz, t, 2 * d),
                                  g.reshape(bsz, t, 2 * d), _row(wt["c_gn_g"][j]), _row(wt["c_gn_b"][j]),
                                  c_state[j].astype(F32))
            nc_state.append(r_new)
            z = z.reshape(m, 2 * d)
            wo = prep["c_w_o"][j]
        x = _post_mixer(z, x2, ple[i].reshape(m, D_PLE), wo,
                        _row(wt["ln_mix_g"][i]), _row(wt["ln_mix_b"][i]), prep["ffn_w1"][i], prep["ffn_w2"][i],
                        _row(wt["ln_ffn_g"][i]), _row(wt["ln_ffn_b"][i]), prep["ple_gate"][i],
                        prep["ple_in"][i]).reshape(bsz, t, d)
    return (x, jnp.stack(na_shift), jnp.stack(na_wkv), jnp.stack(nb_k), jnp.stack(nb_v),
            jnp.stack(nb_logf), jnp.stack(nc_state))


def kernel(x_prompt, x_sample, p_prompt, p_sample, state_a_shift, state_a_wkv, cache_b_k, cache_b_v, cache_b_logf, state_c, ln_mix_g, ln_mix_b, ln_ffn_g, ln_ffn_b, ffn_w1, ffn_w2, ple_in, ple_gate, a_mu, a_w_rkv, a_w0, a_w1, a_w2, a_a0, a_a1, a_a2, a_v0, a_v1, a_v2, a_g1, a_g2, a_k_k, a_k_a, a_r_k, a_gn_g, a_gn_b, a_w_o, b_w_in, b_b_f, b_q_gain, b_k_gain, b_w_o, c_w_in, c_gn_g, c_gn_b, c_w_o):
    wt = dict(ln_mix_g=ln_mix_g, ln_mix_b=ln_mix_b, ln_ffn_g=ln_ffn_g, ln_ffn_b=ln_ffn_b,
              a_mu=a_mu, a_w_rkv=a_w_rkv, a_w0=a_w0, a_w1=a_w1, a_w2=a_w2, a_a0=a_a0, a_a1=a_a1, a_a2=a_a2,
              a_v0=a_v0, a_v1=a_v1, a_v2=a_v2, a_g1=a_g1, a_g2=a_g2, a_k_k=a_k_k, a_k_a=a_k_a, a_r_k=a_r_k,
              a_gn_g=a_gn_g, a_gn_b=a_gn_b, b_w_in=b_w_in, b_b_f=b_b_f, b_q_gain=b_q_gain,
              b_k_gain=b_k_gain, c_gn_g=c_gn_g, c_gn_b=c_gn_b)
    n_a, n_b, n_c = a_mu.shape[0], b_w_in.shape[0], c_w_in.shape[0]
    prep = dict(
        rwkv=[_prep_rwkv(j, wt) for j in range(n_a)],
        fox=[_prep_fox(j, wt) for j in range(n_b)],
        a_w_o=a_w_o.astype(BF16), b_w_o=b_w_o.astype(BF16), c_w_o=c_w_o.astype(BF16),
        c_w_in=c_w_in.astype(BF16), ffn_w1=ffn_w1.astype(BF16), ffn_w2=ffn_w2.astype(BF16),
        ple_gate=ple_gate.astype(BF16), ple_in=ple_in.astype(BF16))
    nbat = x_prompt.shape[0]
    zero_shift = jnp.zeros((n_a, nbat, D_MODEL), x_prompt.dtype)
    zero_wkv = jnp.zeros((n_a, nbat, A_HEADS, A_HEAD, A_HEAD), F32)
    zero_ret = jnp.zeros((n_c, nbat, C_HEADS, C_QK, C_V), F32)
    past_len = cache_b_k.shape[2]
    y_prompt, pa_shift, pa_wkv, pb_k, pb_v, pb_logf, pc_state = _run_trunk(
        x_prompt, p_prompt, 0, zero_shift, zero_wkv, None, zero_ret, wt, prep)
    y_sample, sa_shift, sa_wkv, sb_k, sb_v, sb_logf, sc_state = _run_trunk(
        x_sample, p_sample, past_len, state_a_shift, state_a_wkv, (cache_b_k, cache_b_v, cache_b_logf),
        state_c, wt, prep)
    return (y_prompt, y_sample, pa_shift, pa_wkv, pb_k, pb_v, pb_logf, pc_state,
            sa_shift, sa_wkv, sb_k, sb_v, sb_logf, sc_state)
```

```python
import functools
import math

import jax
import jax.numpy as jnp
from jax import lax
from jax.experimental import pallas as pl
from jax.experimental.pallas import tpu as pltpu

F32 = jnp.float32
BF16 = jnp.bfloat16

D_MODEL = 1024
DEPTH = 4
N_MIXERS = 3
D_PLE = 256
D_FF = 4 * D_MODEL
DN_ALPHA = (2.0 * DEPTH) ** 0.25
LN_EPS = 1e-5
A_HEAD = 64
A_HEADS = D_MODEL // A_HEAD
A_GN_EPS = 64e-5
B_HEAD = 64
B_HEADS = D_MODEL // B_HEAD
C_HEADS = 8
C_QK = D_MODEL // C_HEADS
C_V = 2 * D_MODEL // C_HEADS
ROPE_BASE = 10000.0
RMS_EPS = 1e-6

LANES = 128
PAIRS = D_MODEL // LANES
VMEM_LIMIT = 56 * 1024 * 1024
ROW_TILE = 512
FFN_CHUNK = 1024
RWKV_CHUNK = 64
RWKV_SEQS = 4
RET_CHUNK = 256
ATTN_TILE = 512
CACHE_TILE = 1024


def _cparams(semantics):
    return pltpu.CompilerParams(dimension_semantics=semantics, vmem_limit_bytes=VMEM_LIMIT)


def _resident(shape):
    nd = len(shape)
    return pl.BlockSpec(shape, lambda *_: (0,) * nd, pipeline_mode=pl.Buffered(1))


def _weight(x):
    if isinstance(x, tuple):
        arr, layer = x
        nd = arr.ndim - 1
        spec = pl.BlockSpec((None,) + arr.shape[1:], lambda *_: (layer,) + (0,) * nd,
                            pipeline_mode=pl.Buffered(1))
        return spec, arr
    return _resident(x.shape), x


def _weights(xs):
    specs, args = zip(*[_weight(x) for x in xs])
    return list(specs), list(args)


def _rows(tm, width):
    return pl.BlockSpec((tm, width), lambda i: (i, 0))


def _dot(a, b):
    return jnp.dot(a.astype(BF16), b.astype(BF16), preferred_element_type=F32)


def _dot_nt(a, b):
    return lax.dot_general(a.astype(BF16), b.astype(BF16), (((1,), (1,)), ((), ())),
                           preferred_element_type=F32)


def _dot_tn(a, b):
    return lax.dot_general(a.astype(BF16), b.astype(BF16), (((0,), (0,)), ((), ())),
                           preferred_element_type=F32)


def _sigmoid(x):
    return 1.0 / (1.0 + jnp.exp(-x))


def _softplus(x):
    return jnp.maximum(x, 0.0) + jnp.log(1.0 + jnp.exp(-jnp.abs(x)))


def _layer_norm(x, g, b):
    mu = jnp.mean(x, -1, keepdims=True)
    xc = x - mu
    var = jnp.mean(xc * xc, -1, keepdims=True)
    return xc * lax.rsqrt(var + LN_EPS) * g + b


def _pair_sum(x, first):
    s0 = jnp.sum(jnp.where(first, x, 0.0), -1, keepdims=True)
    s1 = jnp.sum(jnp.where(first, 0.0, x), -1, keepdims=True)
    return jnp.where(first, s0, s1)


def _post_kernel(z_ref, x_ref, p_ref, wo_ref, g1_ref, b1_ref, w1_ref, w2_ref, g2_ref, b2_ref,
                 wg_ref, wp_ref, o_ref):
    x = x_ref[...]
    x1 = _layer_norm(DN_ALPHA * x + _dot(z_ref[...], wo_ref[...]), g1_ref[...], b1_ref[...])
    x1b = x1.astype(BF16)
    acc = jnp.zeros_like(x1)
    for f in range(D_FF // FFN_CHUNK):
        cols = slice(f * FFN_CHUNK, (f + 1) * FFN_CHUNK)
        h = jnp.maximum(_dot(x1b, w1_ref[:, cols]), 0.0)
        acc = acc + _dot(h * h, w2_ref[cols, :])
    x2 = _layer_norm(DN_ALPHA * x1 + acc, g2_ref[...], b2_ref[...])
    gate = _sigmoid(_dot(x2, wg_ref[...]))
    o_ref[...] = x2 + gate * _dot(p_ref[...], wp_ref[...])


def _post_mixer(z, x, p, params):
    m, dz = z.shape
    tm = min(ROW_TILE, m)
    d = D_MODEL
    w_specs, w_args = _weights(params)
    return pl.pallas_call(
        _post_kernel,
        out_shape=jax.ShapeDtypeStruct((m, d), F32),
        grid=(m // tm,),
        in_specs=[_rows(tm, dz), _rows(tm, d), _rows(tm, D_PLE)] + w_specs,
        out_specs=_rows(tm, d),
        compiler_params=_cparams(("parallel",)),
        name="post_mixer",
    )(z, x, p, *w_args)


def _rwkv_proj_kernel(has_vres, seg_rows, *refs):
    if has_vres:
        (x_ref, edge_ref, vf_ref, mu_ref, wrkv_ref, w0_ref, w1_ref, w2_ref, a0_ref, a1_ref, a2_ref,
         g1_ref, g2_ref, v0_ref, v1_ref, v2_ref, r_o, k_o, v_o, wl_o, a_o, g_o) = refs
    else:
        (x_ref, edge_ref, mu_ref, wrkv_ref, w0_ref, w1_ref, w2_ref, a0_ref, a1_ref, a2_ref,
         g1_ref, g2_ref, r_o, k_o, v_o, wl_o, a_o, g_o) = refs
    x = x_ref[...]
    row = lax.broadcasted_iota(jnp.int32, x.shape, 0)
    xprev = pltpu.roll(x, 1, 0)
    for s in range(x.shape[0] // seg_rows):
        xprev = jnp.where(row == s * seg_rows, edge_ref[0, s:s + 1, :], xprev)
    dx = xprev - x
    xr, xw, xk, xv, xa, xg = [(x + dx * mu_ref[i:i + 1, :]).astype(BF16) for i in range(6)]
    r_o[...] = _dot(xr, wrkv_ref[0])
    k_o[...] = _dot(xk, wrkv_ref[1])
    v = _dot(xv, wrkv_ref[2])
    if has_vres:
        mix = _sigmoid(v0_ref[...] + _dot(_dot(xv, v1_ref[...]), v2_ref[...]))
        v = v + (vf_ref[...] - v) * mix
    v_o[...] = v
    pre = w0_ref[...] + _dot(jnp.tanh(_dot(xw, w1_ref[...])), w2_ref[...])
    w_log = -_softplus(-pre) - 0.5
    wl_o[...] = -jnp.exp(w_log)
    a_o[...] = _sigmoid(a0_ref[...] + _dot(_dot(xa, a1_ref[...]), a2_ref[...]))
    g_o[...] = _dot(_sigmoid(_dot(xg, g1_ref[...])), g2_ref[...])


def _rwkv_proj(x, shift, vfirst, w):
    bsz, t, d = x.shape
    m = bsz * t
    tm = min(ROW_TILE, m)
    seg_rows = min(tm, t)
    seg_last = x.reshape(bsz, t // seg_rows, seg_rows, d)[:, :, -1]
    edge = jnp.concatenate([shift[:, None, :].astype(F32), seg_last[:, :-1]], axis=1)
    edge = edge.reshape(m // tm, tm // seg_rows, d)
    sublanes = 8
    if edge.shape[1] < sublanes:
        edge = jnp.pad(edge, ((0, 0), (0, sublanes - edge.shape[1]), (0, 0)))
    x = x.reshape(m, d)
    has_vres = vfirst is not None
    params = [w["mu"], w["w_rkv"], w["w0"], w["w1"], w["w2"], w["a0"], w["a1"], w["a2"], w["g1"], w["g2"]]
    if has_vres:
        params += [w["v0"], w["v1"], w["v2"]]
    w_specs, w_args = _weights(params)
    edge_spec = pl.BlockSpec((1,) + edge.shape[1:], lambda i: (i, 0, 0))
    return pl.pallas_call(
        functools.partial(_rwkv_proj_kernel, has_vres, seg_rows),
        out_shape=[jax.ShapeDtypeStruct((m, d), F32)] * 6,
        grid=(m // tm,),
        in_specs=[_rows(tm, d), edge_spec] + ([_rows(tm, d)] if has_vres else []) + w_specs,
        out_specs=[_rows(tm, d)] * 6,
        compiler_params=_cparams(("parallel",)),
        name="rwkv_proj",
    )(x, edge, *([vfirst] if has_vres else []), *w_args)


def _split3(x):
    hi = x.astype(BF16)
    r1 = x - hi.astype(F32)
    mid = r1.astype(BF16)
    lo = (r1 - mid.astype(F32)).astype(BF16)
    return hi, mid, lo


def _rwkv_chunk_kernel(L, nb, r_ref, wl_ref, k_ref, v_ref, a_ref, g_ref, kk_ref, ka_ref, rk_ref,
                       gng_ref, gnb_ref, s0_ref, z_ref, sout_ref, s_scr):
    c = pl.program_id(1)

    @pl.when(c == 0)
    def _():
        s_scr[...] = s0_ref[...]

    L2 = 2 * L
    first = lax.broadcasted_iota(jnp.int32, (L, LANES), 1) < A_HEAD
    row = lax.broadcasted_iota(jnp.int32, (L, L2), 0)
    col = lax.broadcasted_iota(jnp.int32, (L, L2), 1)
    first_sq = col < L
    col = jnp.where(first_sq, col, col - L)
    strict = row > col
    incl = row >= col
    eye = jnp.where(row == col, 1.0, 0.0).astype(F32)
    tri = jnp.where(lax.broadcasted_iota(jnp.int32, (L, L), 0) >=
                    lax.broadcasted_iota(jnp.int32, (L, L), 1), 1.0, 0.0).astype(BF16)
    vrow = lax.broadcasted_iota(jnp.int32, (LANES, LANES), 0) < A_HEAD
    vcol = lax.broadcasted_iota(jnp.int32, (LANES, LANES), 1) < A_HEAD
    same_head = vrow == vcol

    def stack(x, mask=first):
        return jnp.concatenate([jnp.where(mask, x, 0.0), jnp.where(mask, 0.0, x)], axis=0).astype(BF16)

    units = [(bi, p) for bi in range(nb) for p in range(PAIRS)]
    pairs = range(len(units))
    sls = [slice(p * LANES, (p + 1) * LANES) for _, p in units]
    r = [r_ref[bi, :, sls[i]] for i, (bi, _) in enumerate(units)]
    wl = [wl_ref[bi, :, sls[i]] for i, (bi, _) in enumerate(units)]
    v = [v_ref[bi, :, sls[i]] for i, (bi, _) in enumerate(units)]
    a = [a_ref[bi, :, sls[i]] for i, (bi, _) in enumerate(units)]
    k_raw = [k_ref[bi, :, sls[i]] for i, (bi, _) in enumerate(units)]
    kk = [k_raw[p] * kk_ref[:, sls[p]] for p in pairs]
    kk = [x / jnp.maximum(jnp.sqrt(_pair_sum(x * x, first)), 1e-12) for x in kk]
    k = [k_raw[p] * (1.0 + (a[p] - 1.0) * ka_ref[:, sls[p]]) for p in pairs]
    b = [kk[p] * a[p] for p in pairs]

    def cumsum(x):
        hi, mid, lo = _split3(x)
        return (jnp.dot(tri, hi, preferred_element_type=F32) + jnp.dot(tri, mid, preferred_element_type=F32)
                + jnp.dot(tri, lo, preferred_element_type=F32))

    cum = [cumsum(x) for x in wl]
    g_t = [jnp.exp(x) for x in cum]
    g_prev = [jnp.exp(cum[p] - wl[p]) for p in pairs]
    g_inv = [jnp.exp(-x) for x in cum]
    g_end = [x[L - 1:L, :] for x in g_t]
    kinv = [k[p] * g_inv[p] for p in pairs]
    binv = [b[p] * g_inv[p] for p in pairs]
    lhs = [jnp.concatenate([kk[p] * g_prev[p], r[p] * g_t[p]], axis=0).astype(BF16) for p in pairs]
    rhs = [jnp.concatenate([stack(kinv[p]), stack(binv[p])], axis=0) for p in pairs]
    s_v = [stack(x) for x in v]
    gram = [_dot_nt(lhs[p], rhs[p]) for p in pairs]
    m_k = [jnp.where(strict, x[:L, :L2], 0.0).astype(BF16) for x in gram]
    n_k = [jnp.where(incl, x[L:, :L2], 0.0).astype(BF16) for x in gram]
    n_b = [jnp.where(incl, -x[L:, L2:], 0.0).astype(BF16) for x in gram]
    x_pow = [jnp.where(strict, -x[:L, L2:], 0.0) for x in gram]
    t_inv = [eye + x for x in x_pow]
    x_bd = [stack(x, first_sq) for x in x_pow]
    n = 1
    while 2 * n < L:
        x_pow = [_dot(x_pow[p], x_bd[p]) for p in pairs]
        x_bd = [stack(x, first_sq) for x in x_pow]
        t_inv = [t_inv[p] + _dot(t_inv[p], x_bd[p]) for p in pairs]
        n *= 2
    s_prev = [s_scr[bi, p] for bi, p in units]
    from_state = [_dot_nt(lhs[p], s_prev[p]) for p in pairs]
    y_part = [from_state[p][L:] + _dot(n_k[p], s_v[p]) for p in pairs]
    r_bd = [stack(from_state[p][:L] + _dot(m_k[p], s_v[p])) for p in pairs]
    nb_t = [_dot(n_b[p], stack(t_inv[p], first_sq)) for p in pairs]
    both = [_dot(jnp.concatenate([t_inv[p], nb_t[p]], axis=0), r_bd[p]) for p in pairs]
    u = [x[:L] for x in both]
    y = [y_part[p] + both[p][L:] for p in pairs]
    vu = [jnp.concatenate([v[p], -u[p]], axis=0).astype(BF16) for p in pairs]
    ends = [jnp.concatenate([kinv[p] * g_end[p], binv[p] * g_end[p]], axis=0).astype(BF16) for p in pairs]
    outer = [_dot_tn(vu[p], ends[p]) for p in pairs]
    s_new = [s_prev[p] * g_end[p] + jnp.where(same_head, outer[p], 0.0) for p in pairs]
    for i, (bi, p) in enumerate(units):
        s_scr[bi, p] = s_new[i]
    for i, (bi, _) in enumerate(units):
        sl = sls[i]
        mu = _pair_sum(y[i], first) * (1.0 / A_HEAD)
        yc = y[i] - mu
        var = _pair_sum(yc * yc, first) * (1.0 / A_HEAD)
        yn = yc * lax.rsqrt(var + A_GN_EPS) * gng_ref[:, sl] + gnb_ref[:, sl]
        bonus = _pair_sum(r[i] * k[i] * rk_ref[:, sl], first) * v[i]
        z_ref[bi, :, sl] = ((yn + bonus) * g_ref[bi, :, sl]).astype(BF16)

    @pl.when(c == pl.num_programs(1) - 1)
    def _():
        sout_ref[...] = s_scr[...]


def _rwkv_chunk(r, wl, k, v, a, g, w, s0_pairs):
    bsz, t, d = r.shape
    L = min(RWKV_CHUNK, t)
    nb = math.gcd(bsz, RWKV_SEQS)
    seq = pl.BlockSpec((nb, L, d), lambda b, c: (b, c, 0))
    st = pl.BlockSpec((nb, PAIRS, LANES, LANES), lambda b, c: (b, 0, 0, 0))
    w_specs, w_args = _weights([w["k_k"], w["k_a"], w["r_k"], w["gn_g"], w["gn_b"]])
    return pl.pallas_call(
        functools.partial(_rwkv_chunk_kernel, L, nb),
        out_shape=[jax.ShapeDtypeStruct((bsz, t, d), BF16),
                   jax.ShapeDtypeStruct((bsz, PAIRS, LANES, LANES), F32)],
        grid=(bsz // nb, t // L),
        in_specs=[seq] * 6 + w_specs + [st],
        out_specs=[seq, st],
        scratch_shapes=[pltpu.VMEM((nb, PAIRS, LANES, LANES), F32)],
        compiler_params=_cparams(("parallel", "arbitrary")),
        name="rwkv_chunk",
    )(r, wl, k, v, a, g, *w_args, s0_pairs)


def _state_to_pairs(s):
    bsz = s.shape[0]
    s = s.reshape(bsz, PAIRS, 2, A_HEAD, A_HEAD)
    z = jnp.zeros_like(s[:, :, 0])
    top = jnp.concatenate([s[:, :, 0], z], axis=-1)
    bot = jnp.concatenate([z, s[:, :, 1]], axis=-1)
    return jnp.concatenate([top, bot], axis=-2)


def _pairs_to_state(sp):
    bsz = sp.shape[0]
    s0 = sp[:, :, :A_HEAD, :A_HEAD]
    s1 = sp[:, :, A_HEAD:, A_HEAD:]
    return jnp.stack([s0, s1], axis=2).reshape(bsz, A_HEADS, A_HEAD, A_HEAD)


def _fox_proj_kernel(q_scale, x_ref, w_ref, wf_ref, bf_ref, qg_ref, kg_ref,
                     qb_o, kf_o, kb_o, vf_o, vb_o, gate_o, logf_o):
    d = D_MODEL
    xb = x_ref[...].astype(BF16)
    first = lax.broadcasted_iota(jnp.int32, (xb.shape[0], LANES), 1) < B_HEAD

    def rms(t, gain_ref, blk):
        ms = _pair_sum(t * t, first) * (1.0 / B_HEAD)
        return t * lax.rsqrt(ms + RMS_EPS) * gain_ref[:, blk]

    for j in range(PAIRS):
        blk = slice(j * LANES, (j + 1) * LANES)
        q = rms(_dot(xb, w_ref[:, j * LANES:(j + 1) * LANES]), qg_ref, blk)
        qb_o[:, blk] = (q * q_scale).astype(BF16)
        k = rms(_dot(xb, w_ref[:, d + j * LANES:d + (j + 1) * LANES]), kg_ref, blk)
        kf_o[:, blk] = k
        kb_o[:, blk] = k.astype(BF16)
    v = _dot(xb, w_ref[:, 2 * d:3 * d])
    vf_o[...] = v
    vb_o[...] = v.astype(BF16)
    gate_o[...] = _dot(xb, w_ref[:, 3 * d:4 * d])
    logf_o[...] = -_softplus(-(_dot(xb, wf_ref[...]) + bf_ref[...]))


def _fox_proj(x, w, q_scale):
    m, d = x.shape
    tm = min(ROW_TILE, m)
    outs = [jax.ShapeDtypeStruct((m, d), BF16), jax.ShapeDtypeStruct((m, d), F32),
            jax.ShapeDtypeStruct((m, d), BF16), jax.ShapeDtypeStruct((m, d), F32),
            jax.ShapeDtypeStruct((m, d), BF16), jax.ShapeDtypeStruct((m, d), F32),
            jax.ShapeDtypeStruct((m, LANES), F32)]
    w_specs, w_args = _weights([w["w_in"], w["w_f"], w["b_f"], w["q_gain"], w["k_gain"]])
    return pl.pallas_call(
        functools.partial(_fox_proj_kernel, q_scale),
        out_shape=outs,
        grid=(m // tm,),
        in_specs=[_rows(tm, d)] + w_specs,
        out_specs=[_rows(tm, d)] * 6 + [_rows(tm, LANES)],
        compiler_params=_cparams(("parallel",)),
        name="fox_proj",
    )(x, *w_args)


BIAS_PIECES = 3


def _fox_attn_kernel(tile, q_ref, k_ref, kb_ref, vt_ref, gate_ref, z_ref, m_scr, l_scr, acc_scr):
    qi = pl.program_id(2)
    lane = lax.broadcasted_iota(jnp.int32, (tile, LANES), 1)
    first = lane < B_HEAD
    ones = jnp.where(lane % B_HEAD < BIAS_PIECES, 1.0, 0.0).astype(BF16)
    q = q_ref[0]
    q_heads = (jnp.where(first, q, ones), jnp.where(first, ones, q))
    causal = (lax.broadcasted_iota(jnp.int32, (tile, tile), 0) <=
              lax.broadcasted_iota(jnp.int32, (tile, tile), 1))
    m_scr[...] = jnp.full_like(m_scr, -jnp.inf)
    l_scr[...] = jnp.zeros_like(l_scr)
    acc_scr[...] = jnp.zeros_like(acc_scr)

    def step(blocks):
        k_heads, vts = [], []
        for kb, _ in blocks:
            start = pl.multiple_of(kb * tile, tile)
            k = k_ref[0, pl.ds(start, tile), :]
            kbias = kb_ref[0, pl.ds(start, tile), :]
            k_heads.append((jnp.where(first, k, kbias), jnp.where(first, kbias, k)))
            vts.append(vt_ref[0, kb])
        m_all, l_all, acc_all = m_scr[...], l_scr[...], acc_scr[...]
        heads = range(2)
        rows = [slice(h * B_HEAD, (h + 1) * B_HEAD) for h in heads]
        s = [[_dot_nt(kh[h], q_heads[h]) for kh in k_heads] for h in heads]
        s = [[jnp.where(causal, x, -jnp.inf) if diag else x for x, (_, diag) in zip(s[h], blocks)]
             for h in heads]
        m_prev = [m_all[h:h + 1] for h in heads]
        m_new = list(m_prev)
        for h in heads:
            for x in s[h]:
                m_new[h] = jnp.maximum(m_new[h], jnp.max(x, 0, keepdims=True))
        p = [[jnp.exp2(x - m_new[h]) for x in s[h]] for h in heads]
        alpha = [jnp.exp2(m_prev[h] - m_new[h]) for h in heads]
        l_new = [alpha[h] * l_all[h:h + 1] for h in heads]
        acc_new = [alpha[h] * acc_all[rows[h]] for h in heads]
        for h in heads:
            for x, vt in zip(p[h], vts):
                l_new[h] = l_new[h] + jnp.sum(x, 0, keepdims=True)
                acc_new[h] = acc_new[h] + _dot(vt[rows[h]], x)
        m_scr[...] = jnp.concatenate(m_new, 0)
        l_scr[...] = jnp.concatenate(l_new, 0)
        acc_scr[...] = jnp.concatenate(acc_new, 0)

    @pl.loop(0, qi // 2)
    def _(j):
        step([(2 * j, False), (2 * j + 1, False)])

    @pl.when(qi % 2 == 1)
    def _():
        step([(qi - 1, False), (qi, True)])

    @pl.when(qi % 2 == 0)
    def _():
        step([(qi, True)])

    inv_l = 1.0 / l_scr[...]
    o_t = jnp.concatenate([acc_scr[:B_HEAD, :] * inv_l[0:1], acc_scr[B_HEAD:, :] * inv_l[1:2]], 0)
    z_ref[0] = (_sigmoid(gate_ref[0]) * o_t.T).astype(BF16)


def _fox_attn(q, k, k_bias, v_t, gate):
    bsz, t, d = q.shape
    tile = min(ATTN_TILE, t)
    nq = t // tile
    qspec = pl.BlockSpec((1, tile, LANES), lambda b, p, i: (b, i, p))
    kspec = pl.BlockSpec((1, t, LANES), lambda b, p, i: (b, 0, p))
    vspec = pl.BlockSpec((1, nq, LANES, tile), lambda b, p, i: (b, 0, p, 0))
    return pl.pallas_call(
        functools.partial(_fox_attn_kernel, tile),
        out_shape=jax.ShapeDtypeStruct((bsz, t, d), BF16),
        grid=(bsz, PAIRS, nq),
        in_specs=[qspec, kspec, kspec, vspec, qspec],
        out_specs=qspec,
        scratch_shapes=[pltpu.VMEM((2, tile), F32), pltpu.VMEM((2, tile), F32),
                        pltpu.VMEM((LANES, tile), F32)],
        compiler_params=_cparams(("parallel", "parallel", "arbitrary")),
        name="fox_attn",
    )(q, k, k_bias, v_t, gate)


def _fox_key_bias(c_new):
    bsz, t, h = c_new.shape
    rest = -c_new * math.log2(math.e)
    pieces = []
    for _ in range(BIAS_PIECES):
        top = lax.bitcast_convert_type(
            lax.bitcast_convert_type(rest, jnp.uint32) & jnp.uint32(0xFFFF0000), F32)
        pieces.append(top.astype(BF16))
        rest = rest - top
    packed = jnp.pad(jnp.stack(pieces, -1), ((0, 0), (0, 0), (0, 0), (0, B_HEAD - BIAS_PIECES)))
    packed = packed.reshape(bsz, t, h // 2, 2, B_HEAD)[:, :, :, ::-1, :]
    return packed.reshape(bsz, t, h * B_HEAD)


def _fox_cache_kernel(nq, q_ref, kn_ref, vn_ref, ck_ref, cv_ref, lf_ref, cn_ref, gate_ref, z_ref,
                      m_scr, l_scr, acc_scr, carry_scr):
    kb = pl.program_id(1)
    d = D_MODEL
    rows = B_HEADS * nq
    q = q_ref[0]
    row_head = lax.broadcasted_iota(jnp.int32, (rows, d), 0) // nq
    col_head = lax.broadcasted_iota(jnp.int32, (rows, d), 1) // B_HEAD
    q_rep = jnp.concatenate([q] * B_HEADS, axis=0)
    q_bd = jnp.where(row_head == col_head, q_rep, jnp.zeros_like(q_rep))

    def expand(bias):
        n = bias.shape[-1]
        return jnp.broadcast_to(bias[:, None, :], (B_HEADS, nq, n)).reshape(rows, n)

    @pl.when(kb == 0)
    def _():
        m_scr[...] = jnp.full_like(m_scr, -jnp.inf)
        l_scr[...] = jnp.zeros_like(l_scr)
        acc_scr[...] = jnp.zeros_like(acc_scr)
        carry_scr[...] = jnp.zeros_like(carry_scr)

    def update(s, v):
        m_prev = m_scr[...]
        m_new = jnp.maximum(m_prev, jnp.max(s, -1, keepdims=True))
        p = jnp.exp(s - m_new)
        alpha = jnp.exp(m_prev - m_new)
        l_scr[...] = alpha * l_scr[...] + jnp.sum(p, -1, keepdims=True)
        acc_scr[...] = alpha * acc_scr[...] + _dot(p, v)
        m_scr[...] = m_new

    cum = lf_ref[0]
    tk = cum.shape[-1]
    key = lax.broadcasted_iota(jnp.int32, cum.shape, 1)
    shift = 1
    while shift < tk:
        cum = cum + jnp.where(key >= shift, pltpu.roll(cum, shift, 1), 0.0)
        shift *= 2
    cum = cum + carry_scr[...]
    carry_scr[...] = cum[:, tk - 1:tk]
    update(_dot_nt(q_bd, ck_ref[0]) - expand(cum), cv_ref[0])

    @pl.when(kb == pl.num_programs(1) - 1)
    def _():
        s = _dot_nt(q_bd, kn_ref[0]) - expand(carry_scr[...] + cn_ref[0])
        qpos = lax.broadcasted_iota(jnp.int32, (rows, nq), 0) % nq
        kpos = lax.broadcasted_iota(jnp.int32, (rows, nq), 1)
        update(jnp.where(kpos <= qpos, s, -jnp.inf), vn_ref[0])
        o_all = acc_scr[...] / l_scr[...]
        head_of_col = lax.broadcasted_iota(jnp.int32, (nq, d), 1) // B_HEAD
        o = jnp.zeros((nq, d), F32)
        for h in range(B_HEADS):
            o = o + jnp.where(head_of_col == h, o_all[h * nq:(h + 1) * nq, :], 0.0)
        z_ref[0] = (_sigmoid(gate_ref[0]) * o).astype(BF16)


def _fox_cache_attn(q, kn, vn, ck, cv, logf_cache, c_new, gate):
    bsz, nq, d = q.shape
    plen = ck.shape[1]
    tk = min(CACHE_TILE, plen)
    rows = B_HEADS * nq
    new = pl.BlockSpec((1, nq, d), lambda b, j: (b, 0, 0))
    cache = pl.BlockSpec((1, tk, d), lambda b, j: (b, j, 0))
    return pl.pallas_call(
        functools.partial(_fox_cache_kernel, nq),
        out_shape=jax.ShapeDtypeStruct((bsz, nq, d), BF16),
        grid=(bsz, plen // tk),
        in_specs=[new, new, new, cache, cache,
                  pl.BlockSpec((1, B_HEADS, tk), lambda b, j: (b, 0, j)),
                  pl.BlockSpec((1, B_HEADS, nq), lambda b, j: (b, 0, 0)), new],
        out_specs=new,
        scratch_shapes=[pltpu.VMEM((rows, 1), F32), pltpu.VMEM((rows, 1), F32),
                        pltpu.VMEM((rows, d), F32), pltpu.VMEM((B_HEADS, 1), F32)],
        compiler_params=_cparams(("parallel", "arbitrary")),
        name="fox_cache_attn",
    )(q, kn, vn, ck, cv, logf_cache, c_new, gate)


def _ret_proj_kernel(x_ref, w_ref, cos_ref, sin_ref, q_o, k_o, v_o, g_o):
    d = D_MODEL
    xb = x_ref[...].astype(BF16)
    for h in range(C_HEADS):
        blk = slice(h * C_QK, (h + 1) * C_QK)
        cos = cos_ref[:, blk]
        sin = sin_ref[:, blk]
        q = _dot(xb, w_ref[:, h * C_QK:(h + 1) * C_QK])
        q_o[:, blk] = (q * cos + pltpu.roll(q, C_QK // 2, 1) * sin).astype(BF16)
        k = _dot(xb, w_ref[:, d + h * C_QK:d + (h + 1) * C_QK])
        k_o[:, blk] = ((k * cos + pltpu.roll(k, C_QK // 2, 1) * sin) * (C_QK ** -0.5)).astype(BF16)
    v_o[...] = _dot(xb, w_ref[:, 2 * d:4 * d]).astype(BF16)
    g_o[...] = _dot(xb, w_ref[:, 4 * d:6 * d])


def _ret_proj(x, w_in, cos, sin, table_blocks):
    m, d = x.shape
    tm = min(ROW_TILE, m)
    tab = pl.BlockSpec((tm, d), lambda i: (i % table_blocks, 0))
    w_spec, w_in = _weight(w_in)
    return pl.pallas_call(
        _ret_proj_kernel,
        out_shape=[jax.ShapeDtypeStruct((m, d), BF16), jax.ShapeDtypeStruct((m, d), BF16),
                   jax.ShapeDtypeStruct((m, 2 * d), BF16), jax.ShapeDtypeStruct((m, 2 * d), F32)],
        grid=(m // tm,),
        in_specs=[_rows(tm, d), w_spec, tab, tab],
        out_specs=[_rows(tm, d), _rows(tm, d), _rows(tm, 2 * d), _rows(tm, 2 * d)],
        compiler_params=_cparams(("parallel",)),
        name="ret_proj",
    )(x, w_in, cos, sin)


def _ret_chunk_kernel(q_ref, k_ref, v_ref, g_ref, intra_ref, qd_ref, kd_ref, cd_ref, gng_ref, gnb_ref,
                      r0_ref, z_ref, rout_ref, r_scr):
    c = pl.program_id(1)

    @pl.when(c == 0)
    def _():
        r_scr[...] = r0_ref[0]

    heads = range(C_HEADS)
    qk = [slice(h * C_QK, (h + 1) * C_QK) for h in heads]
    vg = [slice(h * C_V, (h + 1) * C_V) for h in heads]
    q = [q_ref[0, :, sl] for sl in qk]
    k = [k_ref[0, :, sl] for sl in qk]
    v = [v_ref[0, :, sl] for sl in vg]
    state = [r_scr[h] for h in heads]
    s = [_dot_nt(q[h], k[h]) * intra_ref[h] for h in heads]
    o = [_dot(s[h], v[h]) + _dot(q[h], state[h]) * qd_ref[h] for h in heads]
    for h in heads:
        r_scr[h] = state[h] * cd_ref[h] + _dot_tn(k[h].astype(F32) * kd_ref[h], v[h])
    for h in heads:
        mu = jnp.mean(o[h], -1, keepdims=True)
        oc = o[h] - mu
        var = jnp.mean(oc * oc, -1, keepdims=True)
        on = oc * lax.rsqrt(var + LN_EPS) * gng_ref[:, vg[h]] + gnb_ref[:, vg[h]]
        g = g_ref[0, :, vg[h]]
        z_ref[0, :, vg[h]] = (g * _sigmoid(g) * on).astype(BF16)

    @pl.when(c == pl.num_programs(1) - 1)
    def _():
        rout_ref[0] = r_scr[...]


def _ret_chunk(q, k, v, g, gn_g, gn_b, r0):
    bsz, t, _ = q.shape
    L = min(RET_CHUNK, t)
    log_g = jnp.log1p(-jnp.exp2(-5.0 - jnp.arange(C_HEADS, dtype=F32)))
    idx = jnp.arange(L, dtype=F32)
    rel = idx[:, None] - idx[None, :]
    intra = jnp.where(rel[None] >= 0, jnp.exp(rel[None] * log_g[:, None, None]), 0.0)
    q_decay = jnp.exp((idx[None, :, None] + 1.0) * log_g[:, None, None])
    k_decay = jnp.exp((L - 1.0 - idx[None, :, None]) * log_g[:, None, None])
    chunk_decay = jnp.exp(L * log_g)[:, None, None]
    qk = pl.BlockSpec((1, L, D_MODEL), lambda b, c: (b, c, 0))
    vg = pl.BlockSpec((1, L, 2 * D_MODEL), lambda b, c: (b, c, 0))
    st = pl.BlockSpec((1, C_HEADS, C_QK, C_V), lambda b, c: (b, 0, 0, 0))
    w_specs, w_args = _weights([intra, q_decay, k_decay, chunk_decay, gn_g, gn_b])
    return pl.pallas_call(
        _ret_chunk_kernel,
        out_shape=[jax.ShapeDtypeStruct((bsz, t, 2 * D_MODEL), BF16),
                   jax.ShapeDtypeStruct((bsz, C_HEADS, C_QK, C_V), F32)],
        grid=(bsz, t // L),
        in_specs=[qk, qk, vg, vg] + w_specs + [st],
        out_specs=[vg, st],
        scratch_shapes=[pltpu.VMEM((C_HEADS, C_QK, C_V), F32)],
        compiler_params=_cparams(("parallel", "arbitrary")),
        name="ret_chunk",
    )(q, k, v, g, *w_args, r0)


def _pad_last(w, n):
    return jnp.pad(w, [(0, 0)] * (w.ndim - 1) + [(0, n - w.shape[-1])])


def _pad_rows(w, n):
    return jnp.pad(w, [(0, 0)] * (w.ndim - 2) + [(0, n - w.shape[-2]), (0, 0)])


def _rows_of(vecs):
    return vecs.reshape(vecs.shape[0], 1, -1).astype(F32)


def _prep_rwkv(wt):
    def lora(w_in, w_out, width):
        return _pad_last(w_in, width).astype(BF16), _pad_rows(w_out, width).astype(BF16)

    w1, w2 = lora(wt["a_w1"], wt["a_w2"], 128)
    a1, a2 = lora(wt["a_a1"], wt["a_a2"], 128)
    g1, g2 = lora(wt["a_g1"], wt["a_g2"], 256)
    v1, v2 = lora(wt["a_v1"], wt["a_v2"], 128)
    stacks = dict(mu=wt["a_mu"], w_rkv=wt["a_w_rkv"].astype(BF16), w0=_rows_of(wt["a_w0"]), w1=w1, w2=w2,
                  a0=_rows_of(wt["a_a0"]), a1=a1, a2=a2, g1=g1, g2=g2,
                  k_k=_rows_of(wt["a_k_k"]), k_a=_rows_of(wt["a_k_a"]), r_k=_rows_of(wt["a_r_k"]),
                  gn_g=_rows_of(wt["a_gn_g"]), gn_b=_rows_of(wt["a_gn_b"]))
    vres = dict(v0=_rows_of(wt["a_v0"]), v1=v1, v2=v2)
    out = []
    for j in range(wt["a_mu"].shape[0]):
        w = {name: (arr, j) for name, arr in stacks.items()}
        if j > 0:
            w.update({name: (arr, j - 1) for name, arr in vres.items()})
        out.append(w)
    return out


def _prep_fox(wt):
    d = D_MODEL
    w_in = wt["b_w_in"]
    stacks = dict(w_in=w_in[:, :, :4 * d].astype(BF16), w_f=_pad_last(w_in[:, :, 4 * d:], LANES).astype(BF16),
                  b_f=_pad_last(_rows_of(wt["b_b_f"]), LANES),
                  q_gain=_rows_of(jnp.tile(wt["b_q_gain"], (1, B_HEADS))),
                  k_gain=_rows_of(jnp.tile(wt["b_k_gain"], (1, B_HEADS))))
    return [{name: (arr, j) for name, arr in stacks.items()} for j in range(w_in.shape[0])]


def _rope_tables(pos):
    inv = ROPE_BASE ** (-jnp.arange(0, C_QK, 2, dtype=F32) / C_QK)
    ang = pos[:, None].astype(F32) * inv[None]
    cos, sin = jnp.cos(ang), jnp.sin(ang)
    cos_t = jnp.tile(jnp.concatenate([cos, cos], -1), (1, C_HEADS))
    sin_t = jnp.tile(jnp.concatenate([-sin, sin], -1), (1, C_HEADS))
    return cos_t, sin_t


def _run_trunk(x, ple, pos0, a_shift, a_wkv, b_cache, c_state, wt, prep):
    bsz, t, d = x.shape
    m = bsz * t
    v_first = None
    na_shift, na_wkv, nb_k, nb_v, nb_logf, nc_state = [], [], [], [], [], []
    for i in range(DEPTH):
        kind, j = i % N_MIXERS, i // N_MIXERS
        x2 = x.reshape(m, d)
        if kind == 0:
            w = prep["rwkv"][j]
            r, k, v, wl, a, g = _rwkv_proj(x, a_shift[j], v_first, w)
            if j == 0:
                v_first = v
            seq = lambda arr: arr.reshape(bsz, t, d)
            z, s_pairs = _rwkv_chunk(seq(r), seq(wl), seq(k), seq(v), seq(a), seq(g), w,
                                     _state_to_pairs(a_wkv[j].astype(F32)))
            na_shift.append(x[:, -1])
            na_wkv.append(_pairs_to_state(s_pairs))
            z = z.reshape(m, d)
            wo = (prep["a_w_o"], j)
        elif kind == 1:
            w = prep["fox"][j]
            fresh = b_cache is None
            q_scale = B_HEAD ** -0.5 * (math.log2(math.e) if fresh else 1.0)
            qb, kf, kb, vf, vb, gate, logf_pad = _fox_proj(x2, w, q_scale)
            logf = logf_pad[:, :B_HEADS].reshape(bsz, t, B_HEADS)
            c_new = jnp.cumsum(logf, axis=1)
            seq = lambda arr: arr.reshape(bsz, t, d)
            if fresh:
                tile = min(ATTN_TILE, t)
                v_t = vb.reshape(bsz, t // tile, tile, d).transpose(0, 1, 3, 2)
                z = _fox_attn(seq(qb), seq(kb), _fox_key_bias(c_new), v_t, seq(gate))
            else:
                ck, cv, clogf = b_cache[0][j], b_cache[1][j], b_cache[2][j]
                plen = ck.shape[1]
                z = _fox_cache_attn(seq(qb), seq(kb), seq(vb), ck.reshape(bsz, plen, d),
                                    cv.reshape(bsz, plen, d), jnp.swapaxes(clogf.astype(F32), 1, 2),
                                    jnp.swapaxes(c_new, 1, 2), seq(gate))
            nb_k.append(kf.reshape(bsz, t, B_HEADS, B_HEAD))
            nb_v.append(vf.reshape(bsz, t, B_HEADS, B_HEAD))
            nb_logf.append(logf)
            z = z.reshape(m, d)
            wo = (prep["b_w_o"], j)
        else:
            cos, sin = _rope_tables(pos0 + jnp.arange(t))
            tm = min(ROW_TILE, m)
            if t >= tm:
                table_blocks = t // tm
            else:
                cos, sin = jnp.tile(cos, (tm // t, 1)), jnp.tile(sin, (tm // t, 1))
                table_blocks = 1
            q, k, v, g = _ret_proj(x2, (prep["c_w_in"], j), cos, sin, table_blocks)
            z, r_new = _ret_chunk(q.reshape(bsz, t, d), k.reshape(bsz, t, d), v.reshape(bsz, t, 2 * d),
                                  g.reshape(bsz, t, 2 * d), (prep["c_gn_g"], j), (prep["c_gn_b"], j),
                                  c_state[j].astype(F32))
            nc_state.append(r_new)
            z = z.reshape(m, 2 * d)
            wo = (prep["c_w_o"], j)
        layer = lambda name: (prep[name], i)
        x = _post_mixer(z, x2, ple[i].reshape(m, D_PLE),
                        [wo, layer("ln_mix_g"), layer("ln_mix_b"), layer("ffn_w1"), layer("ffn_w2"),
                         layer("ln_ffn_g"), layer("ln_ffn_b"), layer("ple_gate"), layer("ple_in")]
                        ).reshape(bsz, t, d)
    return (x, jnp.stack(na_shift), jnp.stack(na_wkv), jnp.stack(nb_k), jnp.stack(nb_v),
            jnp.stack(nb_logf), jnp.stack(nc_state))


def kernel(x_prompt, x_sample, p_prompt, p_sample, state_a_shift, state_a_wkv, cache_b_k, cache_b_v, cache_b_logf, state_c, ln_mix_g, ln_mix_b, ln_ffn_g, ln_ffn_b, ffn_w1, ffn_w2, ple_in, ple_gate, a_mu, a_w_rkv, a_w0, a_w1, a_w2, a_a0, a_a1, a_a2, a_v0, a_v1, a_v2, a_g1, a_g2, a_k_k, a_k_a, a_r_k, a_gn_g, a_gn_b, a_w_o, b_w_in, b_b_f, b_q_gain, b_k_gain, b_w_o, c_w_in, c_gn_g, c_gn_b, c_w_o):
    wt = dict(ln_mix_g=ln_mix_g, ln_mix_b=ln_mix_b, ln_ffn_g=ln_ffn_g, ln_ffn_b=ln_ffn_b,
              a_mu=a_mu, a_w_rkv=a_w_rkv, a_w0=a_w0, a_w1=a_w1, a_w2=a_w2, a_a0=a_a0, a_a1=a_a1, a_a2=a_a2,
              a_v0=a_v0, a_v1=a_v1, a_v2=a_v2, a_g1=a_g1, a_g2=a_g2, a_k_k=a_k_k, a_k_a=a_k_a, a_r_k=a_r_k,
              a_gn_g=a_gn_g, a_gn_b=a_gn_b, b_w_in=b_w_in, b_b_f=b_b_f, b_q_gain=b_q_gain,
              b_k_gain=b_k_gain, c_gn_g=c_gn_g, c_gn_b=c_gn_b)
    n_a, n_b, n_c = a_mu.shape[0], b_w_in.shape[0], c_w_in.shape[0]
    prep = dict(
        rwkv=_prep_rwkv(wt), fox=_prep_fox(wt),
        a_w_o=a_w_o.astype(BF16), b_w_o=b_w_o.astype(BF16), c_w_o=c_w_o.astype(BF16),
        c_w_in=c_w_in.astype(BF16), c_gn_g=_rows_of(c_gn_g), c_gn_b=_rows_of(c_gn_b),
        ffn_w1=ffn_w1.astype(BF16), ffn_w2=ffn_w2.astype(BF16),
        ple_gate=ple_gate.astype(BF16), ple_in=ple_in.astype(BF16),
        ln_mix_g=_rows_of(ln_mix_g), ln_mix_b=_rows_of(ln_mix_b),
        ln_ffn_g=_rows_of(ln_ffn_g), ln_ffn_b=_rows_of(ln_ffn_b))
    nbat = x_prompt.shape[0]
    zero_shift = jnp.zeros((n_a, nbat, D_MODEL), x_prompt.dtype)
    zero_wkv = jnp.zeros((n_a, nbat, A_HEADS, A_HEAD, A_HEAD), F32)
    zero_ret = jnp.zeros((n_c, nbat, C_HEADS, C_QK, C_V), F32)
    past_len = cache_b_k.shape[2]
    y_prompt, pa_shift, pa_wkv, pb_k, pb_v, pb_logf, pc_state = _run_trunk(
        x_prompt, p_prompt, 0, zero_shift, zero_wkv, None, zero_ret, wt, prep)
    y_sample, sa_shift, sa_wkv, sb_k, sb_v, sb_logf, sc_state = _run_trunk(
        x_sample, p_sample, past_len, state_a_shift, state_a_wkv, (cache_b_k, cache_b_v, cache_b_logf),
        state_c, wt, prep)
    return (y_prompt, y_sample, pa_shift, pa_wkv, pb_k, pb_v, pb_logf, pc_state,
            sa_shift, sa_wkv, sb_k, sb_v, sb_logf, sc_state)
```

```python
import functools
import math

import jax
import jax.numpy as jnp
from jax import lax
from jax.experimental import pallas as pl
from jax.experimental.pallas import tpu as pltpu

F32 = jnp.float32
BF16 = jnp.bfloat16

D_MODEL = 1024
DEPTH = 4
N_MIXERS = 3
D_PLE = 256
D_FF = 4 * D_MODEL
DN_ALPHA = (2.0 * DEPTH) ** 0.25
LN_EPS = 1e-5
A_HEAD = 64
A_HEADS = D_MODEL // A_HEAD
A_GN_EPS = 64e-5
B_HEAD = 64
B_HEADS = D_MODEL // B_HEAD
C_HEADS = 8
C_QK = D_MODEL // C_HEADS
C_V = 2 * D_MODEL // C_HEADS
ROPE_BASE = 10000.0
RMS_EPS = 1e-6

LANES = 128
PAIRS = D_MODEL // LANES
VMEM_LIMIT = 56 * 1024 * 1024
ROW_TILE = 512
FFN_CHUNK = 1024
RWKV_CHUNK = 64
RWKV_SEQS = 4
RET_CHUNK = 256
ATTN_TILE = 512
ATTN_GROUP = 4
CACHE_TILE = 1024


def _cparams(semantics):
    return pltpu.CompilerParams(dimension_semantics=semantics, vmem_limit_bytes=VMEM_LIMIT)


def _resident(shape):
    nd = len(shape)
    return pl.BlockSpec(shape, lambda *_: (0,) * nd, pipeline_mode=pl.Buffered(1))


def _weight(x):
    if isinstance(x, tuple):
        arr, layer = x
        nd = arr.ndim - 1
        spec = pl.BlockSpec((None,) + arr.shape[1:], lambda *_: (layer,) + (0,) * nd,
                            pipeline_mode=pl.Buffered(1))
        return spec, arr
    return _resident(x.shape), x


def _weights(xs):
    specs, args = zip(*[_weight(x) for x in xs])
    return list(specs), list(args)


def _rows(tm, width):
    return pl.BlockSpec((tm, width), lambda i: (i, 0))


def _dot(a, b):
    return jnp.dot(a.astype(BF16), b.astype(BF16), preferred_element_type=F32)


def _dot_nt(a, b):
    return lax.dot_general(a.astype(BF16), b.astype(BF16), (((1,), (1,)), ((), ())),
                           preferred_element_type=F32)


def _dot_tn(a, b):
    return lax.dot_general(a.astype(BF16), b.astype(BF16), (((0,), (0,)), ((), ())),
                           preferred_element_type=F32)


def _sigmoid(x):
    return 1.0 / (1.0 + jnp.exp(-x))


def _softplus(x):
    return jnp.maximum(x, 0.0) + jnp.log(1.0 + jnp.exp(-jnp.abs(x)))


def _layer_norm(x, g, b):
    mu = jnp.mean(x, -1, keepdims=True)
    xc = x - mu
    var = jnp.mean(xc * xc, -1, keepdims=True)
    return xc * lax.rsqrt(var + LN_EPS) * g + b


def _pair_sum(x, first):
    s0 = jnp.sum(jnp.where(first, x, 0.0), -1, keepdims=True)
    s1 = jnp.sum(jnp.where(first, 0.0, x), -1, keepdims=True)
    return jnp.where(first, s0, s1)


def _post_kernel(z_ref, x_ref, p_ref, wo_ref, g1_ref, b1_ref, w1_ref, w2_ref, g2_ref, b2_ref,
                 wg_ref, wp_ref, o_ref):
    x = x_ref[...]
    x1 = _layer_norm(DN_ALPHA * x + _dot(z_ref[...], wo_ref[...]), g1_ref[...], b1_ref[...])
    x1b = x1.astype(BF16)
    acc = jnp.zeros_like(x1)
    for f in range(D_FF // FFN_CHUNK):
        cols = slice(f * FFN_CHUNK, (f + 1) * FFN_CHUNK)
        h = jnp.maximum(_dot(x1b, w1_ref[:, cols]), 0.0)
        acc = acc + _dot(h * h, w2_ref[cols, :])
    x2 = _layer_norm(DN_ALPHA * x1 + acc, g2_ref[...], b2_ref[...])
    gate = _sigmoid(_dot(x2, wg_ref[...]))
    o_ref[...] = x2 + gate * _dot(p_ref[...], wp_ref[...])


def _post_mixer(z, x, p, params):
    m, dz = z.shape
    tm = min(ROW_TILE, m)
    d = D_MODEL
    w_specs, w_args = _weights(params)
    return pl.pallas_call(
        _post_kernel,
        out_shape=jax.ShapeDtypeStruct((m, d), F32),
        grid=(m // tm,),
        in_specs=[_rows(tm, dz), _rows(tm, d), _rows(tm, D_PLE)] + w_specs,
        out_specs=_rows(tm, d),
        compiler_params=_cparams(("parallel",)),
        name="post_mixer",
    )(z, x, p, *w_args)


def _rwkv_proj_kernel(has_vres, seg_rows, *refs):
    if has_vres:
        (x_ref, edge_ref, vf_ref, mu_ref, wrkv_ref, w0_ref, w1_ref, w2_ref, a0_ref, a1_ref, a2_ref,
         g1_ref, g2_ref, v0_ref, v1_ref, v2_ref, r_o, k_o, v_o, wl_o, a_o, g_o) = refs
    else:
        (x_ref, edge_ref, mu_ref, wrkv_ref, w0_ref, w1_ref, w2_ref, a0_ref, a1_ref, a2_ref,
         g1_ref, g2_ref, r_o, k_o, v_o, wl_o, a_o, g_o) = refs
    x = x_ref[...]
    row = lax.broadcasted_iota(jnp.int32, x.shape, 0)
    xprev = pltpu.roll(x, 1, 0)
    for s in range(x.shape[0] // seg_rows):
        xprev = jnp.where(row == s * seg_rows, edge_ref[0, s:s + 1, :], xprev)
    dx = xprev - x
    xr, xw, xk, xv, xa, xg = [(x + dx * mu_ref[i:i + 1, :]).astype(BF16) for i in range(6)]
    r_o[...] = _dot(xr, wrkv_ref[0])
    k_o[...] = _dot(xk, wrkv_ref[1])
    v = _dot(xv, wrkv_ref[2])
    if has_vres:
        mix = _sigmoid(v0_ref[...] + _dot(_dot(xv, v1_ref[...]), v2_ref[...]))
        v = v + (vf_ref[...] - v) * mix
    v_o[...] = v
    pre = w0_ref[...] + _dot(jnp.tanh(_dot(xw, w1_ref[...])), w2_ref[...])
    w_log = -_softplus(-pre) - 0.5
    wl_o[...] = -jnp.exp(w_log)
    a_o[...] = _sigmoid(a0_ref[...] + _dot(_dot(xa, a1_ref[...]), a2_ref[...]))
    g_o[...] = _dot(_sigmoid(_dot(xg, g1_ref[...])), g2_ref[...])


def _rwkv_proj(x, shift, vfirst, w):
    bsz, t, d = x.shape
    m = bsz * t
    tm = min(ROW_TILE, m)
    seg_rows = min(tm, t)
    seg_last = x.reshape(bsz, t // seg_rows, seg_rows, d)[:, :, -1]
    edge = jnp.concatenate([shift[:, None, :].astype(F32), seg_last[:, :-1]], axis=1)
    edge = edge.reshape(m // tm, tm // seg_rows, d)
    sublanes = 8
    if edge.shape[1] < sublanes:
        edge = jnp.pad(edge, ((0, 0), (0, sublanes - edge.shape[1]), (0, 0)))
    x = x.reshape(m, d)
    has_vres = vfirst is not None
    params = [w["mu"], w["w_rkv"], w["w0"], w["w1"], w["w2"], w["a0"], w["a1"], w["a2"], w["g1"], w["g2"]]
    if has_vres:
        params += [w["v0"], w["v1"], w["v2"]]
    w_specs, w_args = _weights(params)
    edge_spec = pl.BlockSpec((1,) + edge.shape[1:], lambda i: (i, 0, 0))
    return pl.pallas_call(
        functools.partial(_rwkv_proj_kernel, has_vres, seg_rows),
        out_shape=[jax.ShapeDtypeStruct((m, d), F32)] * 6,
        grid=(m // tm,),
        in_specs=[_rows(tm, d), edge_spec] + ([_rows(tm, d)] if has_vres else []) + w_specs,
        out_specs=[_rows(tm, d)] * 6,
        compiler_params=_cparams(("parallel",)),
        name="rwkv_proj",
    )(x, edge, *([vfirst] if has_vres else []), *w_args)


def _split3(x):
    hi = x.astype(BF16)
    r1 = x - hi.astype(F32)
    mid = r1.astype(BF16)
    lo = (r1 - mid.astype(F32)).astype(BF16)
    return hi, mid, lo


def _rwkv_chunk_kernel(L, nb, r_ref, wl_ref, k_ref, v_ref, a_ref, g_ref, kk_ref, ka_ref, rk_ref,
                       gng_ref, gnb_ref, s0_ref, z_ref, sout_ref, s_scr):
    c = pl.program_id(1)

    @pl.when(c == 0)
    def _():
        s_scr[...] = s0_ref[...]

    L2 = 2 * L
    first = lax.broadcasted_iota(jnp.int32, (L, LANES), 1) < A_HEAD
    row = lax.broadcasted_iota(jnp.int32, (L, L2), 0)
    col = lax.broadcasted_iota(jnp.int32, (L, L2), 1)
    first_sq = col < L
    col = jnp.where(first_sq, col, col - L)
    strict = row > col
    incl = row >= col
    eye = jnp.where(row == col, 1.0, 0.0).astype(F32)
    tri = jnp.where(lax.broadcasted_iota(jnp.int32, (L, L), 0) >=
                    lax.broadcasted_iota(jnp.int32, (L, L), 1), 1.0, 0.0).astype(BF16)
    vrow = lax.broadcasted_iota(jnp.int32, (LANES, LANES), 0) < A_HEAD
    vcol = lax.broadcasted_iota(jnp.int32, (LANES, LANES), 1) < A_HEAD
    same_head = vrow == vcol

    def stack(x, mask=first):
        return jnp.concatenate([jnp.where(mask, x, 0.0), jnp.where(mask, 0.0, x)], axis=0).astype(BF16)

    units = [(bi, p) for bi in range(nb) for p in range(PAIRS)]
    pairs = range(len(units))
    sls = [slice(p * LANES, (p + 1) * LANES) for _, p in units]
    r = [r_ref[bi, :, sls[i]] for i, (bi, _) in enumerate(units)]
    wl = [wl_ref[bi, :, sls[i]] for i, (bi, _) in enumerate(units)]
    v = [v_ref[bi, :, sls[i]] for i, (bi, _) in enumerate(units)]
    a = [a_ref[bi, :, sls[i]] for i, (bi, _) in enumerate(units)]
    k_raw = [k_ref[bi, :, sls[i]] for i, (bi, _) in enumerate(units)]
    kk = [k_raw[p] * kk_ref[:, sls[p]] for p in pairs]
    kk = [x / jnp.maximum(jnp.sqrt(_pair_sum(x * x, first)), 1e-12) for x in kk]
    k = [k_raw[p] * (1.0 + (a[p] - 1.0) * ka_ref[:, sls[p]]) for p in pairs]
    b = [kk[p] * a[p] for p in pairs]

    def cumsum(x):
        hi, mid, lo = _split3(x)
        return (jnp.dot(tri, hi, preferred_element_type=F32) + jnp.dot(tri, mid, preferred_element_type=F32)
                + jnp.dot(tri, lo, preferred_element_type=F32))

    cum = [cumsum(x) for x in wl]
    g_t = [jnp.exp(x) for x in cum]
    g_prev = [jnp.exp(cum[p] - wl[p]) for p in pairs]
    g_inv = [jnp.exp(-x) for x in cum]
    g_end = [x[L - 1:L, :] for x in g_t]
    kinv = [k[p] * g_inv[p] for p in pairs]
    binv = [b[p] * g_inv[p] for p in pairs]
    lhs = [jnp.concatenate([kk[p] * g_prev[p], r[p] * g_t[p]], axis=0).astype(BF16) for p in pairs]
    rhs = [jnp.concatenate([stack(kinv[p]), stack(binv[p])], axis=0) for p in pairs]
    s_v = [stack(x) for x in v]
    gram = [_dot_nt(lhs[p], rhs[p]) for p in pairs]
    m_k = [jnp.where(strict, x[:L, :L2], 0.0).astype(BF16) for x in gram]
    n_k = [jnp.where(incl, x[L:, :L2], 0.0).astype(BF16) for x in gram]
    n_b = [jnp.where(incl, -x[L:, L2:], 0.0).astype(BF16) for x in gram]
    x_pow = [jnp.where(strict, -x[:L, L2:], 0.0) for x in gram]
    t_inv = [eye + x for x in x_pow]
    x_pow = [_dot(x, stack(x, first_sq)) for x in x_pow]
    n = 2
    while n < L:
        x_bd = [stack(x, first_sq) for x in x_pow]
        if 2 * n < L:
            both = [_dot(jnp.concatenate([t_inv[p], x_pow[p]], axis=0), x_bd[p]) for p in pairs]
            t_inv = [t_inv[p] + both[p][:L] for p in pairs]
            x_pow = [x[L:] for x in both]
        else:
            t_inv = [t_inv[p] + _dot(t_inv[p], x_bd[p]) for p in pairs]
        n *= 2
    s_prev = [s_scr[bi, p] for bi, p in units]
    from_state = [_dot_nt(lhs[p], s_prev[p]) for p in pairs]
    y_part = [from_state[p][L:] + _dot(n_k[p], s_v[p]) for p in pairs]
    r_bd = [stack(from_state[p][:L] + _dot(m_k[p], s_v[p])) for p in pairs]
    nb_t = [_dot(n_b[p], stack(t_inv[p], first_sq)) for p in pairs]
    both = [_dot(jnp.concatenate([t_inv[p], nb_t[p]], axis=0), r_bd[p]) for p in pairs]
    u = [x[:L] for x in both]
    y = [y_part[p] + both[p][L:] for p in pairs]
    vu = [jnp.concatenate([v[p], -u[p]], axis=0).astype(BF16) for p in pairs]
    ends = [jnp.concatenate([kinv[p] * g_end[p], binv[p] * g_end[p]], axis=0).astype(BF16) for p in pairs]
    outer = [_dot_tn(vu[p], ends[p]) for p in pairs]
    s_new = [s_prev[p] * g_end[p] + jnp.where(same_head, outer[p], 0.0) for p in pairs]
    for i, (bi, p) in enumerate(units):
        s_scr[bi, p] = s_new[i]
    for i, (bi, _) in enumerate(units):
        sl = sls[i]
        mu = _pair_sum(y[i], first) * (1.0 / A_HEAD)
        yc = y[i] - mu
        var = _pair_sum(yc * yc, first) * (1.0 / A_HEAD)
        yn = yc * lax.rsqrt(var + A_GN_EPS) * gng_ref[:, sl] + gnb_ref[:, sl]
        bonus = _pair_sum(r[i] * k[i] * rk_ref[:, sl], first) * v[i]
        z_ref[bi, :, sl] = ((yn + bonus) * g_ref[bi, :, sl]).astype(BF16)

    @pl.when(c == pl.num_programs(1) - 1)
    def _():
        sout_ref[...] = s_scr[...]


def _rwkv_chunk(r, wl, k, v, a, g, w, s0_pairs):
    bsz, t, d = r.shape
    L = min(RWKV_CHUNK, t)
    nb = math.gcd(bsz, RWKV_SEQS)
    seq = pl.BlockSpec((nb, L, d), lambda b, c: (b, c, 0))
    st = pl.BlockSpec((nb, PAIRS, LANES, LANES), lambda b, c: (b, 0, 0, 0))
    w_specs, w_args = _weights([w["k_k"], w["k_a"], w["r_k"], w["gn_g"], w["gn_b"]])
    return pl.pallas_call(
        functools.partial(_rwkv_chunk_kernel, L, nb),
        out_shape=[jax.ShapeDtypeStruct((bsz, t, d), BF16),
                   jax.ShapeDtypeStruct((bsz, PAIRS, LANES, LANES), F32)],
        grid=(bsz // nb, t // L),
        in_specs=[seq] * 6 + w_specs + [st],
        out_specs=[seq, st],
        scratch_shapes=[pltpu.VMEM((nb, PAIRS, LANES, LANES), F32)],
        compiler_params=_cparams(("parallel", "arbitrary")),
        name="rwkv_chunk",
    )(r, wl, k, v, a, g, *w_args, s0_pairs)


def _state_to_pairs(s):
    bsz = s.shape[0]
    s = s.reshape(bsz, PAIRS, 2, A_HEAD, A_HEAD)
    z = jnp.zeros_like(s[:, :, 0])
    top = jnp.concatenate([s[:, :, 0], z], axis=-1)
    bot = jnp.concatenate([z, s[:, :, 1]], axis=-1)
    return jnp.concatenate([top, bot], axis=-2)


def _pairs_to_state(sp):
    bsz = sp.shape[0]
    s0 = sp[:, :, :A_HEAD, :A_HEAD]
    s1 = sp[:, :, A_HEAD:, A_HEAD:]
    return jnp.stack([s0, s1], axis=2).reshape(bsz, A_HEADS, A_HEAD, A_HEAD)


def _fox_proj_kernel(q_scale, x_ref, w_ref, wf_ref, bf_ref, qg_ref, kg_ref,
                     qb_o, kf_o, kb_o, vf_o, vb_o, gate_o, logf_o):
    d = D_MODEL
    xb = x_ref[...].astype(BF16)
    first = lax.broadcasted_iota(jnp.int32, (xb.shape[0], LANES), 1) < B_HEAD

    def rms(t, gain_ref, blk):
        ms = _pair_sum(t * t, first) * (1.0 / B_HEAD)
        return t * lax.rsqrt(ms + RMS_EPS) * gain_ref[:, blk]

    for j in range(PAIRS):
        blk = slice(j * LANES, (j + 1) * LANES)
        q = rms(_dot(xb, w_ref[:, j * LANES:(j + 1) * LANES]), qg_ref, blk)
        qb_o[:, blk] = (q * q_scale).astype(BF16)
        k = rms(_dot(xb, w_ref[:, d + j * LANES:d + (j + 1) * LANES]), kg_ref, blk)
        kf_o[:, blk] = k
        kb_o[:, blk] = k.astype(BF16)
    v = _dot(xb, w_ref[:, 2 * d:3 * d])
    vf_o[...] = v
    vb_o[...] = v.astype(BF16)
    gate_o[...] = _dot(xb, w_ref[:, 3 * d:4 * d])
    logf_o[...] = -_softplus(-(_dot(xb, wf_ref[...]) + bf_ref[...]))


def _fox_proj(x, w, q_scale):
    m, d = x.shape
    tm = min(ROW_TILE, m)
    outs = [jax.ShapeDtypeStruct((m, d), BF16), jax.ShapeDtypeStruct((m, d), F32),
            jax.ShapeDtypeStruct((m, d), BF16), jax.ShapeDtypeStruct((m, d), F32),
            jax.ShapeDtypeStruct((m, d), BF16), jax.ShapeDtypeStruct((m, d), F32),
            jax.ShapeDtypeStruct((m, LANES), F32)]
    w_specs, w_args = _weights([w["w_in"], w["w_f"], w["b_f"], w["q_gain"], w["k_gain"]])
    return pl.pallas_call(
        functools.partial(_fox_proj_kernel, q_scale),
        out_shape=outs,
        grid=(m // tm,),
        in_specs=[_rows(tm, d)] + w_specs,
        out_specs=[_rows(tm, d)] * 6 + [_rows(tm, LANES)],
        compiler_params=_cparams(("parallel",)),
        name="fox_proj",
    )(x, *w_args)


BIAS_PIECES = 3


def _fox_attn_kernel(tile, q_ref, k_ref, kb_ref, vt_ref, gate_ref, z_ref, m_scr, l_scr, acc_scr):
    qi = pl.program_id(2)
    lane = lax.broadcasted_iota(jnp.int32, (tile, LANES), 1)
    first = lane < B_HEAD
    ones = jnp.where(lane % B_HEAD < BIAS_PIECES, 1.0, 0.0).astype(BF16)
    q = q_ref[0]
    q_heads = (jnp.where(first, q, ones), jnp.where(first, ones, q))
    causal = (lax.broadcasted_iota(jnp.int32, (tile, tile), 0) <=
              lax.broadcasted_iota(jnp.int32, (tile, tile), 1))
    m_scr[...] = jnp.full_like(m_scr, -jnp.inf)
    l_scr[...] = jnp.zeros_like(l_scr)
    acc_scr[...] = jnp.zeros_like(acc_scr)

    def step(blocks):
        k_heads, vts = [], []
        for kb, _ in blocks:
            start = pl.multiple_of(kb * tile, tile)
            k = k_ref[0, pl.ds(start, tile), :]
            kbias = kb_ref[0, pl.ds(start, tile), :]
            k_heads.append((jnp.where(first, k, kbias), jnp.where(first, kbias, k)))
            vts.append(vt_ref[0, kb])
        m_all, l_all, acc_all = m_scr[...], l_scr[...], acc_scr[...]
        heads = range(2)
        rows = [slice(h * B_HEAD, (h + 1) * B_HEAD) for h in heads]
        s = [[_dot_nt(kh[h], q_heads[h]) for kh in k_heads] for h in heads]
        s = [[jnp.where(causal, x, -jnp.inf) if diag else x for x, (_, diag) in zip(s[h], blocks)]
             for h in heads]
        m_prev = [m_all[h:h + 1] for h in heads]
        m_new = list(m_prev)
        for h in heads:
            for x in s[h]:
                m_new[h] = jnp.maximum(m_new[h], jnp.max(x, 0, keepdims=True))
        p = [[jnp.exp2(x - m_new[h]) for x in s[h]] for h in heads]
        alpha = [jnp.exp2(m_prev[h] - m_new[h]) for h in heads]
        l_new = [alpha[h] * l_all[h:h + 1] for h in heads]
        acc_new = [alpha[h] * acc_all[rows[h]] for h in heads]
        for h in heads:
            for x, vt in zip(p[h], vts):
                l_new[h] = l_new[h] + jnp.sum(x, 0, keepdims=True)
                acc_new[h] = acc_new[h] + _dot(vt[rows[h]], x)
        m_scr[...] = jnp.concatenate(m_new, 0)
        l_scr[...] = jnp.concatenate(l_new, 0)
        acc_scr[...] = jnp.concatenate(acc_new, 0)

    @pl.loop(0, qi // ATTN_GROUP)
    def _(j):
        step([(ATTN_GROUP * j + i, False) for i in range(ATTN_GROUP)])

    for rem in range(ATTN_GROUP):
        @pl.when(qi % ATTN_GROUP == rem)
        def _(rem=rem):
            step([(qi - rem + i, False) for i in range(rem)] + [(qi, True)])

    inv_l = 1.0 / l_scr[...]
    o_t = jnp.concatenate([acc_scr[:B_HEAD, :] * inv_l[0:1], acc_scr[B_HEAD:, :] * inv_l[1:2]], 0)
    z_ref[0] = (_sigmoid(gate_ref[0]) * o_t.T).astype(BF16)


def _fox_attn(q, k, k_bias, v_t, gate):
    bsz, t, d = q.shape
    tile = min(ATTN_TILE, t)
    nq = t // tile
    qspec = pl.BlockSpec((1, tile, LANES), lambda b, p, i: (b, i, p))
    kspec = pl.BlockSpec((1, t, LANES), lambda b, p, i: (b, 0, p))
    vspec = pl.BlockSpec((1, nq, LANES, tile), lambda b, p, i: (b, 0, p, 0))
    return pl.pallas_call(
        functools.partial(_fox_attn_kernel, tile),
        out_shape=jax.ShapeDtypeStruct((bsz, t, d), BF16),
        grid=(bsz, PAIRS, nq),
        in_specs=[qspec, kspec, kspec, vspec, qspec],
        out_specs=qspec,
        scratch_shapes=[pltpu.VMEM((2, tile), F32), pltpu.VMEM((2, tile), F32),
                        pltpu.VMEM((LANES, tile), F32)],
        compiler_params=_cparams(("parallel", "parallel", "arbitrary")),
        name="fox_attn",
    )(q, k, k_bias, v_t, gate)


def _fox_key_bias(c_new):
    bsz, t, h = c_new.shape
    rest = -c_new * math.log2(math.e)
    pieces = []
    for _ in range(BIAS_PIECES):
        top = lax.bitcast_convert_type(
            lax.bitcast_convert_type(rest, jnp.uint32) & jnp.uint32(0xFFFF0000), F32)
        pieces.append(top.astype(BF16))
        rest = rest - top
    packed = jnp.pad(jnp.stack(pieces, -1), ((0, 0), (0, 0), (0, 0), (0, B_HEAD - BIAS_PIECES)))
    packed = packed.reshape(bsz, t, h // 2, 2, B_HEAD)[:, :, :, ::-1, :]
    return packed.reshape(bsz, t, h * B_HEAD)


def _fox_cache_kernel(nq, q_ref, kn_ref, vn_ref, ck_ref, cv_ref, lf_ref, cn_ref, gate_ref, z_ref,
                      m_scr, l_scr, acc_scr, carry_scr):
    kb = pl.program_id(1)
    d = D_MODEL
    rows = B_HEADS * nq
    q = q_ref[0]
    row_head = lax.broadcasted_iota(jnp.int32, (rows, d), 0) // nq
    col_head = lax.broadcasted_iota(jnp.int32, (rows, d), 1) // B_HEAD
    q_rep = jnp.concatenate([q] * B_HEADS, axis=0)
    q_bd = jnp.where(row_head == col_head, q_rep, jnp.zeros_like(q_rep))

    def expand(bias):
        n = bias.shape[-1]
        return jnp.broadcast_to(bias[:, None, :], (B_HEADS, nq, n)).reshape(rows, n)

    @pl.when(kb == 0)
    def _():
        m_scr[...] = jnp.full_like(m_scr, -jnp.inf)
        l_scr[...] = jnp.zeros_like(l_scr)
        acc_scr[...] = jnp.zeros_like(acc_scr)
        carry_scr[...] = jnp.zeros_like(carry_scr)

    def update(s, v):
        m_prev = m_scr[...]
        m_new = jnp.maximum(m_prev, jnp.max(s, -1, keepdims=True))
        p = jnp.exp(s - m_new)
        alpha = jnp.exp(m_prev - m_new)
        l_scr[...] = alpha * l_scr[...] + jnp.sum(p, -1, keepdims=True)
        acc_scr[...] = alpha * acc_scr[...] + _dot(p, v)
        m_scr[...] = m_new

    cum = lf_ref[0]
    tk = cum.shape[-1]
    key = lax.broadcasted_iota(jnp.int32, cum.shape, 1)
    shift = 1
    while shift < tk:
        cum = cum + jnp.where(key >= shift, pltpu.roll(cum, shift, 1), 0.0)
        shift *= 2
    cum = cum + carry_scr[...]
    carry_scr[...] = cum[:, tk - 1:tk]
    update(_dot_nt(q_bd, ck_ref[0]) - expand(cum), cv_ref[0])

    @pl.when(kb == pl.num_programs(1) - 1)
    def _():
        s = _dot_nt(q_bd, kn_ref[0]) - expand(carry_scr[...] + cn_ref[0])
        qpos = lax.broadcasted_iota(jnp.int32, (rows, nq), 0) % nq
        kpos = lax.broadcasted_iota(jnp.int32, (rows, nq), 1)
        update(jnp.where(kpos <= qpos, s, -jnp.inf), vn_ref[0])
        o_all = acc_scr[...] / l_scr[...]
        head_of_col = lax.broadcasted_iota(jnp.int32, (nq, d), 1) // B_HEAD
        o = jnp.zeros((nq, d), F32)
        for h in range(B_HEADS):
            o = o + jnp.where(head_of_col == h, o_all[h * nq:(h + 1) * nq, :], 0.0)
        z_ref[0] = (_sigmoid(gate_ref[0]) * o).astype(BF16)


def _fox_cache_attn(q, kn, vn, ck, cv, logf_cache, c_new, gate):
    bsz, nq, d = q.shape
    plen = ck.shape[1]
    tk = min(CACHE_TILE, plen)
    rows = B_HEADS * nq
    new = pl.BlockSpec((1, nq, d), lambda b, j: (b, 0, 0))
    cache = pl.BlockSpec((1, tk, d), lambda b, j: (b, j, 0))
    return pl.pallas_call(
        functools.partial(_fox_cache_kernel, nq),
        out_shape=jax.ShapeDtypeStruct((bsz, nq, d), BF16),
        grid=(bsz, plen // tk),
        in_specs=[new, new, new, cache, cache,
                  pl.BlockSpec((1, B_HEADS, tk), lambda b, j: (b, 0, j)),
                  pl.BlockSpec((1, B_HEADS, nq), lambda b, j: (b, 0, 0)), new],
        out_specs=new,
        scratch_shapes=[pltpu.VMEM((rows, 1), F32), pltpu.VMEM((rows, 1), F32),
                        pltpu.VMEM((rows, d), F32), pltpu.VMEM((B_HEADS, 1), F32)],
        compiler_params=_cparams(("parallel", "arbitrary")),
        name="fox_cache_attn",
    )(q, kn, vn, ck, cv, logf_cache, c_new, gate)


def _ret_proj_kernel(x_ref, w_ref, cos_ref, sin_ref, q_o, k_o, v_o, g_o):
    d = D_MODEL
    xb = x_ref[...].astype(BF16)
    for h in range(C_HEADS):
        blk = slice(h * C_QK, (h + 1) * C_QK)
        cos = cos_ref[:, blk]
        sin = sin_ref[:, blk]
        q = _dot(xb, w_ref[:, h * C_QK:(h + 1) * C_QK])
        q_o[:, blk] = (q * cos + pltpu.roll(q, C_QK // 2, 1) * sin).astype(BF16)
        k = _dot(xb, w_ref[:, d + h * C_QK:d + (h + 1) * C_QK])
        k_o[:, blk] = ((k * cos + pltpu.roll(k, C_QK // 2, 1) * sin) * (C_QK ** -0.5)).astype(BF16)
    v_o[...] = _dot(xb, w_ref[:, 2 * d:4 * d]).astype(BF16)
    g_o[...] = _dot(xb, w_ref[:, 4 * d:6 * d])


def _ret_proj(x, w_in, cos, sin, table_blocks):
    m, d = x.shape
    tm = min(ROW_TILE, m)
    tab = pl.BlockSpec((tm, d), lambda i: (i % table_blocks, 0))
    w_spec, w_in = _weight(w_in)
    return pl.pallas_call(
        _ret_proj_kernel,
        out_shape=[jax.ShapeDtypeStruct((m, d), BF16), jax.ShapeDtypeStruct((m, d), BF16),
                   jax.ShapeDtypeStruct((m, 2 * d), BF16), jax.ShapeDtypeStruct((m, 2 * d), F32)],
        grid=(m // tm,),
        in_specs=[_rows(tm, d), w_spec, tab, tab],
        out_specs=[_rows(tm, d), _rows(tm, d), _rows(tm, 2 * d), _rows(tm, 2 * d)],
        compiler_params=_cparams(("parallel",)),
        name="ret_proj",
    )(x, w_in, cos, sin)


def _ret_chunk_kernel(q_ref, k_ref, v_ref, g_ref, intra_ref, qd_ref, kd_ref, cd_ref, gng_ref, gnb_ref,
                      r0_ref, z_ref, rout_ref, r_scr):
    c = pl.program_id(1)

    @pl.when(c == 0)
    def _():
        r_scr[...] = r0_ref[0]

    heads = range(C_HEADS)
    qk = [slice(h * C_QK, (h + 1) * C_QK) for h in heads]
    vg = [slice(h * C_V, (h + 1) * C_V) for h in heads]
    q = [q_ref[0, :, sl] for sl in qk]
    k = [k_ref[0, :, sl] for sl in qk]
    v = [v_ref[0, :, sl] for sl in vg]
    state = [r_scr[h] for h in heads]
    s = [_dot_nt(q[h], k[h]) * intra_ref[h] for h in heads]
    o = [_dot(s[h], v[h]) + _dot(q[h], state[h]) * qd_ref[h] for h in heads]
    for h in heads:
        r_scr[h] = state[h] * cd_ref[h] + _dot_tn(k[h].astype(F32) * kd_ref[h], v[h])
    for h in heads:
        mu = jnp.mean(o[h], -1, keepdims=True)
        oc = o[h] - mu
        var = jnp.mean(oc * oc, -1, keepdims=True)
        on = oc * lax.rsqrt(var + LN_EPS) * gng_ref[:, vg[h]] + gnb_ref[:, vg[h]]
        g = g_ref[0, :, vg[h]]
        z_ref[0, :, vg[h]] = (g * _sigmoid(g) * on).astype(BF16)

    @pl.when(c == pl.num_programs(1) - 1)
    def _():
        rout_ref[0] = r_scr[...]


def _ret_chunk(q, k, v, g, gn_g, gn_b, r0):
    bsz, t, _ = q.shape
    L = min(RET_CHUNK, t)
    log_g = jnp.log1p(-jnp.exp2(-5.0 - jnp.arange(C_HEADS, dtype=F32)))
    idx = jnp.arange(L, dtype=F32)
    rel = idx[:, None] - idx[None, :]
    intra = jnp.where(rel[None] >= 0, jnp.exp(rel[None] * log_g[:, None, None]), 0.0)
    q_decay = jnp.exp((idx[None, :, None] + 1.0) * log_g[:, None, None])
    k_decay = jnp.exp((L - 1.0 - idx[None, :, None]) * log_g[:, None, None])
    chunk_decay = jnp.exp(L * log_g)[:, None, None]
    qk = pl.BlockSpec((1, L, D_MODEL), lambda b, c: (b, c, 0))
    vg = pl.BlockSpec((1, L, 2 * D_MODEL), lambda b, c: (b, c, 0))
    st = pl.BlockSpec((1, C_HEADS, C_QK, C_V), lambda b, c: (b, 0, 0, 0))
    w_specs, w_args = _weights([intra, q_decay, k_decay, chunk_decay, gn_g, gn_b])
    return pl.pallas_call(
        _ret_chunk_kernel,
        out_shape=[jax.ShapeDtypeStruct((bsz, t, 2 * D_MODEL), BF16),
                   jax.ShapeDtypeStruct((bsz, C_HEADS, C_QK, C_V), F32)],
        grid=(bsz, t // L),
        in_specs=[qk, qk, vg, vg] + w_specs + [st],
        out_specs=[vg, st],
        scratch_shapes=[pltpu.VMEM((C_HEADS, C_QK, C_V), F32)],
        compiler_params=_cparams(("parallel", "arbitrary")),
        name="ret_chunk",
    )(q, k, v, g, *w_args, r0)


def _pad_last(w, n):
    return jnp.pad(w, [(0, 0)] * (w.ndim - 1) + [(0, n - w.shape[-1])])


def _pad_rows(w, n):
    return jnp.pad(w, [(0, 0)] * (w.ndim - 2) + [(0, n - w.shape[-2]), (0, 0)])


def _rows_of(vecs):
    return vecs.reshape(vecs.shape[0], 1, -1).astype(F32)


def _prep_rwkv(wt):
    def lora(w_in, w_out, width):
        return _pad_last(w_in, width).astype(BF16), _pad_rows(w_out, width).astype(BF16)

    w1, w2 = lora(wt["a_w1"], wt["a_w2"], 128)
    a1, a2 = lora(wt["a_a1"], wt["a_a2"], 128)
    g1, g2 = lora(wt["a_g1"], wt["a_g2"], 256)
    v1, v2 = lora(wt["a_v1"], wt["a_v2"], 128)
    stacks = dict(mu=wt["a_mu"], w_rkv=wt["a_w_rkv"].astype(BF16), w0=_rows_of(wt["a_w0"]), w1=w1, w2=w2,
                  a0=_rows_of(wt["a_a0"]), a1=a1, a2=a2, g1=g1, g2=g2,
                  k_k=_rows_of(wt["a_k_k"]), k_a=_rows_of(wt["a_k_a"]), r_k=_rows_of(wt["a_r_k"]),
                  gn_g=_rows_of(wt["a_gn_g"]), gn_b=_rows_of(wt["a_gn_b"]))
    vres = dict(v0=_rows_of(wt["a_v0"]), v1=v1, v2=v2)
    out = []
    for j in range(wt["a_mu"].shape[0]):
        w = {name: (arr, j) for name, arr in stacks.items()}
        if j > 0:
            w.update({name: (arr, j - 1) for name, arr in vres.items()})
        out.append(w)
    return out


def _prep_fox(wt):
    d = D_MODEL
    w_in = wt["b_w_in"]
    stacks = dict(w_in=w_in[:, :, :4 * d].astype(BF16), w_f=_pad_last(w_in[:, :, 4 * d:], LANES).astype(BF16),
                  b_f=_pad_last(_rows_of(wt["b_b_f"]), LANES),
                  q_gain=_rows_of(jnp.tile(wt["b_q_gain"], (1, B_HEADS))),
                  k_gain=_rows_of(jnp.tile(wt["b_k_gain"], (1, B_HEADS))))
    return [{name: (arr, j) for name, arr in stacks.items()} for j in range(w_in.shape[0])]


def _rope_tables(pos):
    inv = ROPE_BASE ** (-jnp.arange(0, C_QK, 2, dtype=F32) / C_QK)
    ang = pos[:, None].astype(F32) * inv[None]
    cos, sin = jnp.cos(ang), jnp.sin(ang)
    cos_t = jnp.tile(jnp.concatenate([cos, cos], -1), (1, C_HEADS))
    sin_t = jnp.tile(jnp.concatenate([-sin, sin], -1), (1, C_HEADS))
    return cos_t, sin_t


def _run_trunk(x, ple, pos0, a_shift, a_wkv, b_cache, c_state, wt, prep):
    bsz, t, d = x.shape
    m = bsz * t
    v_first = None
    na_shift, na_wkv, nb_k, nb_v, nb_logf, nc_state = [], [], [], [], [], []
    for i in range(DEPTH):
        kind, j = i % N_MIXERS, i // N_MIXERS
        x2 = x.reshape(m, d)
        if kind == 0:
            w = prep["rwkv"][j]
            r, k, v, wl, a, g = _rwkv_proj(x, a_shift[j], v_first, w)
            if j == 0:
                v_first = v
            seq = lambda arr: arr.reshape(bsz, t, d)
            z, s_pairs = _rwkv_chunk(seq(r), seq(wl), seq(k), seq(v), seq(a), seq(g), w,
                                     _state_to_pairs(a_wkv[j].astype(F32)))
            na_shift.append(x[:, -1])
            na_wkv.append(_pairs_to_state(s_pairs))
            z = z.reshape(m, d)
            wo = (prep["a_w_o"], j)
        elif kind == 1:
            w = prep["fox"][j]
            fresh = b_cache is None
            q_scale = B_HEAD ** -0.5 * (math.log2(math.e) if fresh else 1.0)
            qb, kf, kb, vf, vb, gate, logf_pad = _fox_proj(x2, w, q_scale)
            logf = logf_pad[:, :B_HEADS].reshape(bsz, t, B_HEADS)
            c_new = jnp.cumsum(logf, axis=1)
            seq = lambda arr: arr.reshape(bsz, t, d)
            if fresh:
                tile = min(ATTN_TILE, t)
                v_t = vb.reshape(bsz, t // tile, tile, d).transpose(0, 1, 3, 2)
                z = _fox_attn(seq(qb), seq(kb), _fox_key_bias(c_new), v_t, seq(gate))
            else:
                ck, cv, clogf = b_cache[0][j], b_cache[1][j], b_cache[2][j]
                plen = ck.shape[1]
                z = _fox_cache_attn(seq(qb), seq(kb), seq(vb), ck.reshape(bsz, plen, d),
                                    cv.reshape(bsz, plen, d), jnp.swapaxes(clogf.astype(F32), 1, 2),
                                    jnp.swapaxes(c_new, 1, 2), seq(gate))
            nb_k.append(kf.reshape(bsz, t, B_HEADS, B_HEAD))
            nb_v.append(vf.reshape(bsz, t, B_HEADS, B_HEAD))
            nb_logf.append(logf)
            z = z.reshape(m, d)
            wo = (prep["b_w_o"], j)
        else:
            cos, sin = _rope_tables(pos0 + jnp.arange(t))
            tm = min(ROW_TILE, m)
            if t >= tm:
                table_blocks = t // tm
            else:
                cos, sin = jnp.tile(cos, (tm // t, 1)), jnp.tile(sin, (tm // t, 1))
                table_blocks = 1
            q, k, v, g = _ret_proj(x2, (prep["c_w_in"], j), cos, sin, table_blocks)
            z, r_new = _ret_chunk(q.reshape(bsz, t, d), k.reshape(bsz, t, d), v.reshape(bsz, t, 2 * d),
                                  g.reshape(bsz, t, 2 * d), (prep["c_gn_g"], j), (prep["c_gn_b"], j),
                                  c_state[j].astype(F32))
            nc_state.append(r_new)
            z = z.reshape(m, 2 * d)
            wo = (prep["c_w_o"], j)
        layer = lambda name: (prep[name], i)
        x = _post_mixer(z, x2, ple[i].reshape(m, D_PLE),
                        [wo, layer("ln_mix_g"), layer("ln_mix_b"), layer("ffn_w1"), layer("ffn_w2"),
                         layer("ln_ffn_g"), layer("ln_ffn_b"), layer("ple_gate"), layer("ple_in")]
                        ).reshape(bsz, t, d)
    return (x, jnp.stack(na_shift), jnp.stack(na_wkv), jnp.stack(nb_k), jnp.stack(nb_v),
            jnp.stack(nb_logf), jnp.stack(nc_state))


def kernel(x_prompt, x_sample, p_prompt, p_sample, state_a_shift, state_a_wkv, cache_b_k, cache_b_v, cache_b_logf, state_c, ln_mix_g, ln_mix_b, ln_ffn_g, ln_ffn_b, ffn_w1, ffn_w2, ple_in, ple_gate, a_mu, a_w_rkv, a_w0, a_w1, a_w2, a_a0, a_a1, a_a2, a_v0, a_v1, a_v2, a_g1, a_g2, a_k_k, a_k_a, a_r_k, a_gn_g, a_gn_b, a_w_o, b_w_in, b_b_f, b_q_gain, b_k_gain, b_w_o, c_w_in, c_gn_g, c_gn_b, c_w_o):
    wt = dict(ln_mix_g=ln_mix_g, ln_mix_b=ln_mix_b, ln_ffn_g=ln_ffn_g, ln_ffn_b=ln_ffn_b,
              a_mu=a_mu, a_w_rkv=a_w_rkv, a_w0=a_w0, a_w1=a_w1, a_w2=a_w2, a_a0=a_a0, a_a1=a_a1, a_a2=a_a2,
              a_v0=a_v0, a_v1=a_v1, a_v2=a_v2, a_g1=a_g1, a_g2=a_g2, a_k_k=a_k_k, a_k_a=a_k_a, a_r_k=a_r_k,
              a_gn_g=a_gn_g, a_gn_b=a_gn_b, b_w_in=b_w_in, b_b_f=b_b_f, b_q_gain=b_q_gain,
              b_k_gain=b_k_gain, c_gn_g=c_gn_g, c_gn_b=c_gn_b)
    n_a, n_b, n_c = a_mu.shape[0], b_w_in.shape[0], c_w_in.shape[0]
    prep = dict(
        rwkv=_prep_rwkv(wt), fox=_prep_fox(wt),
        a_w_o=a_w_o.astype(BF16), b_w_o=b_w_o.astype(BF16), c_w_o=c_w_o.astype(BF16),
        c_w_in=c_w_in.astype(BF16), c_gn_g=_rows_of(c_gn_g), c_gn_b=_rows_of(c_gn_b),
        ffn_w1=ffn_w1.astype(BF16), ffn_w2=ffn_w2.astype(BF16),
        ple_gate=ple_gate.astype(BF16), ple_in=ple_in.astype(BF16),
        ln_mix_g=_rows_of(ln_mix_g), ln_mix_b=_rows_of(ln_mix_b),
        ln_ffn_g=_rows_of(ln_ffn_g), ln_ffn_b=_rows_of(ln_ffn_b))
    nbat = x_prompt.shape[0]
    zero_shift = jnp.zeros((n_a, nbat, D_MODEL), x_prompt.dtype)
    zero_wkv = jnp.zeros((n_a, nbat, A_HEADS, A_HEAD, A_HEAD), F32)
    zero_ret = jnp.zeros((n_c, nbat, C_HEADS, C_QK, C_V), F32)
    past_len = cache_b_k.shape[2]
    y_prompt, pa_shift, pa_wkv, pb_k, pb_v, pb_logf, pc_state = _run_trunk(
        x_prompt, p_prompt, 0, zero_shift, zero_wkv, None, zero_ret, wt, prep)
    y_sample, sa_shift, sa_wkv, sb_k, sb_v, sb_logf, sc_state = _run_trunk(
        x_sample, p_sample, past_len, state_a_shift, state_a_wkv, (cache_b_k, cache_b_v, cache_b_logf),
        state_c, wt, prep)
    return (y_prompt, y_sample, pa_shift, pa_wkv, pb_k, pb_v, pb_logf, pc_state,
            sa_shift, sa_wkv, sb_k, sb_v, sb_logf, sc_state)
```

```python
import functools
import math

import jax
import jax.numpy as jnp
from jax import lax
from jax.experimental import pallas as pl
from jax.experimental.pallas import tpu as pltpu

F32 = jnp.float32
BF16 = jnp.bfloat16

D_MODEL = 1024
DEPTH = 4
N_MIXERS = 3
D_PLE = 256
D_FF = 4 * D_MODEL
DN_ALPHA = (2.0 * DEPTH) ** 0.25
LN_EPS = 1e-5
A_HEAD = 64
A_HEADS = D_MODEL // A_HEAD
A_GN_EPS = 64e-5
B_HEAD = 64
B_HEADS = D_MODEL // B_HEAD
C_HEADS = 8
C_QK = D_MODEL // C_HEADS
C_V = 2 * D_MODEL // C_HEADS
ROPE_BASE = 10000.0
RMS_EPS = 1e-6

LANES = 128
PAIRS = D_MODEL // LANES
VMEM_LIMIT = 56 * 1024 * 1024
ROW_TILE = 512
FFN_CHUNK = 1024
RWKV_CHUNK = 64
RWKV_SEQS = 4
RET_CHUNK = 256
ATTN_TILE = 512
ATTN_GROUP = 4
CACHE_TILE = 1024


def _cparams(semantics):
    return pltpu.CompilerParams(dimension_semantics=semantics, vmem_limit_bytes=VMEM_LIMIT)


def _resident(shape):
    nd = len(shape)
    return pl.BlockSpec(shape, lambda *_: (0,) * nd, pipeline_mode=pl.Buffered(1))


def _weight(x):
    if isinstance(x, tuple):
        arr, layer = x
        nd = arr.ndim - 1
        spec = pl.BlockSpec((None,) + arr.shape[1:], lambda *_: (layer,) + (0,) * nd,
                            pipeline_mode=pl.Buffered(1))
        return spec, arr
    return _resident(x.shape), x


def _weights(xs):
    specs, args = zip(*[_weight(x) for x in xs])
    return list(specs), list(args)


def _rows(tm, width):
    return pl.BlockSpec((tm, width), lambda i: (i, 0))


def _dot(a, b):
    return jnp.dot(a.astype(BF16), b.astype(BF16), preferred_element_type=F32)


def _dot_nt(a, b):
    return lax.dot_general(a.astype(BF16), b.astype(BF16), (((1,), (1,)), ((), ())),
                           preferred_element_type=F32)


def _dot_tn(a, b):
    return lax.dot_general(a.astype(BF16), b.astype(BF16), (((0,), (0,)), ((), ())),
                           preferred_element_type=F32)


def _sigmoid(x):
    return 1.0 / (1.0 + jnp.exp(-x))


def _softplus(x):
    return jnp.maximum(x, 0.0) + jnp.log(1.0 + jnp.exp(-jnp.abs(x)))


def _layer_norm(x, g, b):
    mu = jnp.mean(x, -1, keepdims=True)
    xc = x - mu
    var = jnp.mean(xc * xc, -1, keepdims=True)
    return xc * lax.rsqrt(var + LN_EPS) * g + b


def _pair_sum(x, first):
    s0 = jnp.sum(jnp.where(first, x, 0.0), -1, keepdims=True)
    s1 = jnp.sum(jnp.where(first, 0.0, x), -1, keepdims=True)
    return jnp.where(first, s0, s1)


def _post_kernel(z_ref, x_ref, p_ref, wo_ref, g1_ref, b1_ref, w1_ref, w2_ref, g2_ref, b2_ref,
                 wg_ref, wp_ref, o_ref):
    x = x_ref[...]
    x1 = _layer_norm(DN_ALPHA * x + _dot(z_ref[...], wo_ref[...]), g1_ref[...], b1_ref[...])
    x1b = x1.astype(BF16)
    acc = jnp.zeros_like(x1)
    for f in range(D_FF // FFN_CHUNK):
        cols = slice(f * FFN_CHUNK, (f + 1) * FFN_CHUNK)
        h = jnp.maximum(_dot(x1b, w1_ref[:, cols]), 0.0)
        acc = acc + _dot(h * h, w2_ref[cols, :])
    x2 = _layer_norm(DN_ALPHA * x1 + acc, g2_ref[...], b2_ref[...])
    gate = _sigmoid(_dot(x2, wg_ref[...]))
    o_ref[...] = x2 + gate * _dot(p_ref[...], wp_ref[...])


def _post_mixer(z, x, p, params):
    m, dz = z.shape
    tm = min(ROW_TILE, m)
    d = D_MODEL
    w_specs, w_args = _weights(params)
    return pl.pallas_call(
        _post_kernel,
        out_shape=jax.ShapeDtypeStruct((m, d), F32),
        grid=(m // tm,),
        in_specs=[_rows(tm, dz), _rows(tm, d), _rows(tm, D_PLE)] + w_specs,
        out_specs=_rows(tm, d),
        compiler_params=_cparams(("parallel",)),
        name="post_mixer",
    )(z, x, p, *w_args)


def _rwkv_proj_kernel(has_vres, seg_rows, *refs):
    if has_vres:
        (x_ref, edge_ref, vf_ref, mu_ref, wrkv_ref, w0_ref, w1_ref, w2_ref, a0_ref, a1_ref, a2_ref,
         g1_ref, g2_ref, v0_ref, v1_ref, v2_ref, r_o, k_o, v_o, wl_o, a_o, g_o) = refs
    else:
        (x_ref, edge_ref, mu_ref, wrkv_ref, w0_ref, w1_ref, w2_ref, a0_ref, a1_ref, a2_ref,
         g1_ref, g2_ref, r_o, k_o, v_o, wl_o, a_o, g_o) = refs
    x = x_ref[...]
    row = lax.broadcasted_iota(jnp.int32, x.shape, 0)
    xprev = pltpu.roll(x, 1, 0)
    for s in range(x.shape[0] // seg_rows):
        xprev = jnp.where(row == s * seg_rows, edge_ref[0, s:s + 1, :], xprev)
    dx = xprev - x
    xr, xw, xk, xv, xa, xg = [(x + dx * mu_ref[i:i + 1, :]).astype(BF16) for i in range(6)]
    r_o[...] = _dot(xr, wrkv_ref[0])
    k_o[...] = _dot(xk, wrkv_ref[1])
    v = _dot(xv, wrkv_ref[2])
    if has_vres:
        mix = _sigmoid(v0_ref[...] + _dot(_dot(xv, v1_ref[...]), v2_ref[...]))
        v = v + (vf_ref[...] - v) * mix
    v_o[...] = v
    pre = w0_ref[...] + _dot(jnp.tanh(_dot(xw, w1_ref[...])), w2_ref[...])
    w_log = -_softplus(-pre) - 0.5
    wl_o[...] = -jnp.exp(w_log)
    a_o[...] = _sigmoid(a0_ref[...] + _dot(_dot(xa, a1_ref[...]), a2_ref[...]))
    g_o[...] = _dot(_sigmoid(_dot(xg, g1_ref[...])), g2_ref[...])


def _rwkv_proj(x, shift, vfirst, w):
    bsz, t, d = x.shape
    m = bsz * t
    tm = min(ROW_TILE, m)
    seg_rows = min(tm, t)
    seg_last = x.reshape(bsz, t // seg_rows, seg_rows, d)[:, :, -1]
    edge = jnp.concatenate([shift[:, None, :].astype(F32), seg_last[:, :-1]], axis=1)
    edge = edge.reshape(m // tm, tm // seg_rows, d)
    sublanes = 8
    if edge.shape[1] < sublanes:
        edge = jnp.pad(edge, ((0, 0), (0, sublanes - edge.shape[1]), (0, 0)))
    x = x.reshape(m, d)
    has_vres = vfirst is not None
    params = [w["mu"], w["w_rkv"], w["w0"], w["w1"], w["w2"], w["a0"], w["a1"], w["a2"], w["g1"], w["g2"]]
    if has_vres:
        params += [w["v0"], w["v1"], w["v2"]]
    w_specs, w_args = _weights(params)
    edge_spec = pl.BlockSpec((1,) + edge.shape[1:], lambda i: (i, 0, 0))
    return pl.pallas_call(
        functools.partial(_rwkv_proj_kernel, has_vres, seg_rows),
        out_shape=[jax.ShapeDtypeStruct((m, d), F32)] * 6,
        grid=(m // tm,),
        in_specs=[_rows(tm, d), edge_spec] + ([_rows(tm, d)] if has_vres else []) + w_specs,
        out_specs=[_rows(tm, d)] * 6,
        compiler_params=_cparams(("parallel",)),
        name="rwkv_proj",
    )(x, edge, *([vfirst] if has_vres else []), *w_args)


def _rwkv_chunk_kernel(L, nb, r_ref, wl_ref, k_ref, v_ref, a_ref, g_ref, kk_ref, ka_ref, rk_ref,
                       gng_ref, gnb_ref, s0_ref, z_ref, sout_ref, s_scr):
    c = pl.program_id(1)

    @pl.when(c == 0)
    def _():
        s_scr[...] = s0_ref[...]

    L2 = 2 * L
    first = lax.broadcasted_iota(jnp.int32, (L, LANES), 1) < A_HEAD
    row = lax.broadcasted_iota(jnp.int32, (L, L2), 0)
    col = lax.broadcasted_iota(jnp.int32, (L, L2), 1)
    first_sq = col < L
    col = jnp.where(first_sq, col, col - L)
    strict = row > col
    incl = row >= col
    eye = jnp.where(row == col, 1.0, 0.0).astype(F32)
    vrow = lax.broadcasted_iota(jnp.int32, (LANES, LANES), 0) < A_HEAD
    vcol = lax.broadcasted_iota(jnp.int32, (LANES, LANES), 1) < A_HEAD
    same_head = vrow == vcol

    def stack(x, mask=first):
        return jnp.concatenate([jnp.where(mask, x, 0.0), jnp.where(mask, 0.0, x)], axis=0).astype(BF16)

    units = [(bi, p) for bi in range(nb) for p in range(PAIRS)]
    pairs = range(len(units))
    sls = [slice(p * LANES, (p + 1) * LANES) for _, p in units]
    r = [r_ref[bi, :, sls[i]] for i, (bi, _) in enumerate(units)]
    wl = [wl_ref[bi, :, sls[i]] for i, (bi, _) in enumerate(units)]
    v = [v_ref[bi, :, sls[i]] for i, (bi, _) in enumerate(units)]
    a = [a_ref[bi, :, sls[i]] for i, (bi, _) in enumerate(units)]
    k_raw = [k_ref[bi, :, sls[i]] for i, (bi, _) in enumerate(units)]
    kk = [k_raw[p] * kk_ref[:, sls[p]] for p in pairs]
    kk = [x * lax.rsqrt(jnp.maximum(_pair_sum(x * x, first), 1e-24)) for x in kk]
    k = [k_raw[p] * (1.0 + (a[p] - 1.0) * ka_ref[:, sls[p]]) for p in pairs]
    b = [kk[p] * a[p] for p in pairs]

    step_row = lax.broadcasted_iota(jnp.int32, (L, LANES), 0)

    def cumsum(x):
        shift = 1
        while shift < L:
            x = x + jnp.where(step_row >= shift, pltpu.roll(x, shift, 0), 0.0)
            shift *= 2
        return x

    cum = [cumsum(x) for x in wl]
    g_t = [jnp.exp(x) for x in cum]
    g_prev = [jnp.exp(cum[p] - wl[p]) for p in pairs]
    g_inv = [jnp.exp(-x) for x in cum]
    g_end = [x[L - 1:L, :] for x in g_t]
    kinv = [k[p] * g_inv[p] for p in pairs]
    binv = [b[p] * g_inv[p] for p in pairs]
    lhs = [jnp.concatenate([kk[p] * g_prev[p], r[p] * g_t[p]], axis=0).astype(BF16) for p in pairs]
    rhs = [jnp.concatenate([stack(kinv[p]), stack(binv[p])], axis=0) for p in pairs]
    s_v = [stack(x) for x in v]
    gram = [_dot_nt(lhs[p], rhs[p]) for p in pairs]
    m_k = [jnp.where(strict, x[:L, :L2], 0.0).astype(BF16) for x in gram]
    n_k = [jnp.where(incl, x[L:, :L2], 0.0).astype(BF16) for x in gram]
    n_b = [jnp.where(incl, -x[L:, L2:], 0.0).astype(BF16) for x in gram]
    x_pow = [jnp.where(strict, -x[:L, L2:], 0.0) for x in gram]
    t_inv = [eye + x for x in x_pow]
    x_pow = [_dot(x, stack(x, first_sq)) for x in x_pow]
    n = 2
    while n < L:
        x_bd = [stack(x, first_sq) for x in x_pow]
        if 2 * n < L:
            both = [_dot(jnp.concatenate([t_inv[p], x_pow[p]], axis=0), x_bd[p]) for p in pairs]
            t_inv = [t_inv[p] + both[p][:L] for p in pairs]
            x_pow = [x[L:] for x in both]
        else:
            t_inv = [t_inv[p] + _dot(t_inv[p], x_bd[p]) for p in pairs]
        n *= 2
    s_prev = [s_scr[bi, p] for bi, p in units]
    from_state = [_dot_nt(lhs[p], s_prev[p]) for p in pairs]
    y_part = [from_state[p][L:] + _dot(n_k[p], s_v[p]) for p in pairs]
    r_bd = [stack(from_state[p][:L] + _dot(m_k[p], s_v[p])) for p in pairs]
    nb_t = [_dot(n_b[p], stack(t_inv[p], first_sq)) for p in pairs]
    both = [_dot(jnp.concatenate([t_inv[p], nb_t[p]], axis=0), r_bd[p]) for p in pairs]
    u = [x[:L] for x in both]
    y = [y_part[p] + both[p][L:] for p in pairs]
    vu = [jnp.concatenate([v[p], -u[p]], axis=0).astype(BF16) for p in pairs]
    ends = [jnp.concatenate([kinv[p] * g_end[p], binv[p] * g_end[p]], axis=0).astype(BF16) for p in pairs]
    outer = [_dot_tn(vu[p], ends[p]) for p in pairs]
    s_new = [s_prev[p] * g_end[p] + jnp.where(same_head, outer[p], 0.0) for p in pairs]
    for i, (bi, p) in enumerate(units):
        s_scr[bi, p] = s_new[i]
    for i, (bi, _) in enumerate(units):
        sl = sls[i]
        mu = _pair_sum(y[i], first) * (1.0 / A_HEAD)
        yc = y[i] - mu
        var = _pair_sum(yc * yc, first) * (1.0 / A_HEAD)
        yn = yc * lax.rsqrt(var + A_GN_EPS) * gng_ref[:, sl] + gnb_ref[:, sl]
        bonus = _pair_sum(r[i] * k[i] * rk_ref[:, sl], first) * v[i]
        z_ref[bi, :, sl] = ((yn + bonus) * g_ref[bi, :, sl]).astype(BF16)

    @pl.when(c == pl.num_programs(1) - 1)
    def _():
        sout_ref[...] = s_scr[...]


def _rwkv_chunk(r, wl, k, v, a, g, w, s0_pairs):
    bsz, t, d = r.shape
    L = min(RWKV_CHUNK, t)
    nb = math.gcd(bsz, RWKV_SEQS)
    seq = pl.BlockSpec((nb, L, d), lambda b, c: (b, c, 0))
    st = pl.BlockSpec((nb, PAIRS, LANES, LANES), lambda b, c: (b, 0, 0, 0))
    w_specs, w_args = _weights([w["k_k"], w["k_a"], w["r_k"], w["gn_g"], w["gn_b"]])
    return pl.pallas_call(
        functools.partial(_rwkv_chunk_kernel, L, nb),
        out_shape=[jax.ShapeDtypeStruct((bsz, t, d), BF16),
                   jax.ShapeDtypeStruct((bsz, PAIRS, LANES, LANES), F32)],
        grid=(bsz // nb, t // L),
        in_specs=[seq] * 6 + w_specs + [st],
        out_specs=[seq, st],
        scratch_shapes=[pltpu.VMEM((nb, PAIRS, LANES, LANES), F32)],
        compiler_params=_cparams(("parallel", "arbitrary")),
        name="rwkv_chunk",
    )(r, wl, k, v, a, g, *w_args, s0_pairs)


def _state_to_pairs(s):
    bsz = s.shape[0]
    s = s.reshape(bsz, PAIRS, 2, A_HEAD, A_HEAD)
    z = jnp.zeros_like(s[:, :, 0])
    top = jnp.concatenate([s[:, :, 0], z], axis=-1)
    bot = jnp.concatenate([z, s[:, :, 1]], axis=-1)
    return jnp.concatenate([top, bot], axis=-2)


def _pairs_to_state(sp):
    bsz = sp.shape[0]
    s0 = sp[:, :, :A_HEAD, :A_HEAD]
    s1 = sp[:, :, A_HEAD:, A_HEAD:]
    return jnp.stack([s0, s1], axis=2).reshape(bsz, A_HEADS, A_HEAD, A_HEAD)


def _fox_proj_kernel(q_scale, x_ref, w_ref, wf_ref, bf_ref, qg_ref, kg_ref,
                     qb_o, kf_o, kb_o, vf_o, vb_o, gate_o, logf_o):
    d = D_MODEL
    xb = x_ref[...].astype(BF16)
    first = lax.broadcasted_iota(jnp.int32, (xb.shape[0], LANES), 1) < B_HEAD

    def rms(t, gain_ref, blk):
        ms = _pair_sum(t * t, first) * (1.0 / B_HEAD)
        return t * lax.rsqrt(ms + RMS_EPS) * gain_ref[:, blk]

    for j in range(PAIRS):
        blk = slice(j * LANES, (j + 1) * LANES)
        q = rms(_dot(xb, w_ref[:, j * LANES:(j + 1) * LANES]), qg_ref, blk)
        qb_o[:, blk] = (q * q_scale).astype(BF16)
        k = rms(_dot(xb, w_ref[:, d + j * LANES:d + (j + 1) * LANES]), kg_ref, blk)
        kf_o[:, blk] = k
        kb_o[:, blk] = k.astype(BF16)
    v = _dot(xb, w_ref[:, 2 * d:3 * d])
    vf_o[...] = v
    vb_o[...] = v.astype(BF16)
    gate_o[...] = _dot(xb, w_ref[:, 3 * d:4 * d])
    logf_o[...] = -_softplus(-(_dot(xb, wf_ref[...]) + bf_ref[...]))


def _fox_proj(x, w, q_scale):
    m, d = x.shape
    tm = min(ROW_TILE, m)
    outs = [jax.ShapeDtypeStruct((m, d), BF16), jax.ShapeDtypeStruct((m, d), F32),
            jax.ShapeDtypeStruct((m, d), BF16), jax.ShapeDtypeStruct((m, d), F32),
            jax.ShapeDtypeStruct((m, d), BF16), jax.ShapeDtypeStruct((m, d), F32),
            jax.ShapeDtypeStruct((m, LANES), F32)]
    w_specs, w_args = _weights([w["w_in"], w["w_f"], w["b_f"], w["q_gain"], w["k_gain"]])
    return pl.pallas_call(
        functools.partial(_fox_proj_kernel, q_scale),
        out_shape=outs,
        grid=(m // tm,),
        in_specs=[_rows(tm, d)] + w_specs,
        out_specs=[_rows(tm, d)] * 6 + [_rows(tm, LANES)],
        compiler_params=_cparams(("parallel",)),
        name="fox_proj",
    )(x, *w_args)


BIAS_PIECES = 3


def _fox_attn_kernel(tile, q_ref, k_ref, kb_ref, vt_ref, gate_ref, z_ref, m_scr, l_scr, acc_scr):
    qi = pl.program_id(2)
    lane = lax.broadcasted_iota(jnp.int32, (tile, LANES), 1)
    first = lane < B_HEAD
    ones = jnp.where(lane % B_HEAD < BIAS_PIECES, 1.0, 0.0).astype(BF16)
    q = q_ref[0]
    q_heads = (jnp.where(first, q, ones), jnp.where(first, ones, q))
    causal = (lax.broadcasted_iota(jnp.int32, (tile, tile), 0) <=
              lax.broadcasted_iota(jnp.int32, (tile, tile), 1))
    m_scr[...] = jnp.full_like(m_scr, -jnp.inf)
    l_scr[...] = jnp.zeros_like(l_scr)
    acc_scr[...] = jnp.zeros_like(acc_scr)

    def step(blocks):
        k_heads, vts = [], []
        for kb, _ in blocks:
            start = pl.multiple_of(kb * tile, tile)
            k = k_ref[0, pl.ds(start, tile), :]
            kbias = kb_ref[0, pl.ds(start, tile), :]
            k_heads.append((jnp.where(first, k, kbias), jnp.where(first, kbias, k)))
            vts.append(vt_ref[0, kb])
        m_all, l_all, acc_all = m_scr[...], l_scr[...], acc_scr[...]
        heads = range(2)
        rows = [slice(h * B_HEAD, (h + 1) * B_HEAD) for h in heads]
        s = [[_dot_nt(kh[h], q_heads[h]) for kh in k_heads] for h in heads]
        s = [[jnp.where(causal, x, -jnp.inf) if diag else x for x, (_, diag) in zip(s[h], blocks)]
             for h in heads]
        m_prev = [m_all[h:h + 1] for h in heads]
        m_new = list(m_prev)
        for h in heads:
            for x in s[h]:
                m_new[h] = jnp.maximum(m_new[h], jnp.max(x, 0, keepdims=True))
        p = [[jnp.exp2(x - m_new[h]) for x in s[h]] for h in heads]
        alpha = [jnp.exp2(m_prev[h] - m_new[h]) for h in heads]
        l_new = [alpha[h] * l_all[h:h + 1] for h in heads]
        acc_new = [alpha[h] * acc_all[rows[h]] for h in heads]
        for h in heads:
            for x, vt in zip(p[h], vts):
                l_new[h] = l_new[h] + jnp.sum(x, 0, keepdims=True)
                acc_new[h] = acc_new[h] + _dot(vt[rows[h]], x)
        m_scr[...] = jnp.concatenate(m_new, 0)
        l_scr[...] = jnp.concatenate(l_new, 0)
        acc_scr[...] = jnp.concatenate(acc_new, 0)

    @pl.loop(0, qi // ATTN_GROUP)
    def _(j):
        step([(ATTN_GROUP * j + i, False) for i in range(ATTN_GROUP)])

    for rem in range(ATTN_GROUP):
        @pl.when(qi % ATTN_GROUP == rem)
        def _(rem=rem):
            step([(qi - rem + i, False) for i in range(rem)] + [(qi, True)])

    inv_l = 1.0 / l_scr[...]
    o_t = jnp.concatenate([acc_scr[:B_HEAD, :] * inv_l[0:1], acc_scr[B_HEAD:, :] * inv_l[1:2]], 0)
    z_ref[0] = (_sigmoid(gate_ref[0]) * o_t.T).astype(BF16)


def _fox_attn(q, k, k_bias, v_t, gate):
    bsz, t, d = q.shape
    tile = min(ATTN_TILE, t)
    nq = t // tile
    qspec = pl.BlockSpec((1, tile, LANES), lambda b, p, i: (b, i, p))
    kspec = pl.BlockSpec((1, t, LANES), lambda b, p, i: (b, 0, p))
    vspec = pl.BlockSpec((1, nq, LANES, tile), lambda b, p, i: (b, 0, p, 0))
    return pl.pallas_call(
        functools.partial(_fox_attn_kernel, tile),
        out_shape=jax.ShapeDtypeStruct((bsz, t, d), BF16),
        grid=(bsz, PAIRS, nq),
        in_specs=[qspec, kspec, kspec, vspec, qspec],
        out_specs=qspec,
        scratch_shapes=[pltpu.VMEM((2, tile), F32), pltpu.VMEM((2, tile), F32),
                        pltpu.VMEM((LANES, tile), F32)],
        compiler_params=_cparams(("parallel", "parallel", "arbitrary")),
        name="fox_attn",
    )(q, k, k_bias, v_t, gate)


def _fox_key_bias(c_new):
    bsz, t, h = c_new.shape
    rest = -c_new * math.log2(math.e)
    pieces = []
    for _ in range(BIAS_PIECES):
        top = lax.bitcast_convert_type(
            lax.bitcast_convert_type(rest, jnp.uint32) & jnp.uint32(0xFFFF0000), F32)
        pieces.append(top.astype(BF16))
        rest = rest - top
    packed = jnp.pad(jnp.stack(pieces, -1), ((0, 0), (0, 0), (0, 0), (0, B_HEAD - BIAS_PIECES)))
    packed = packed.reshape(bsz, t, h // 2, 2, B_HEAD)[:, :, :, ::-1, :]
    return packed.reshape(bsz, t, h * B_HEAD)


def _fox_cache_kernel(nq, q_ref, kn_ref, vn_ref, ck_ref, cv_ref, lf_ref, cn_ref, gate_ref, z_ref,
                      m_scr, l_scr, acc_scr, carry_scr):
    kb = pl.program_id(1)
    d = D_MODEL
    rows = B_HEADS * nq
    q = q_ref[0]
    row_head = lax.broadcasted_iota(jnp.int32, (rows, d), 0) // nq
    col_head = lax.broadcasted_iota(jnp.int32, (rows, d), 1) // B_HEAD
    q_rep = jnp.concatenate([q] * B_HEADS, axis=0)
    q_bd = jnp.where(row_head == col_head, q_rep, jnp.zeros_like(q_rep))

    def expand(bias):
        n = bias.shape[-1]
        return jnp.broadcast_to(bias[:, None, :], (B_HEADS, nq, n)).reshape(rows, n)

    @pl.when(kb == 0)
    def _():
        m_scr[...] = jnp.full_like(m_scr, -jnp.inf)
        l_scr[...] = jnp.zeros_like(l_scr)
        acc_scr[...] = jnp.zeros_like(acc_scr)
        carry_scr[...] = jnp.zeros_like(carry_scr)

    def update(s, v):
        m_prev = m_scr[...]
        m_new = jnp.maximum(m_prev, jnp.max(s, -1, keepdims=True))
        p = jnp.exp(s - m_new)
        alpha = jnp.exp(m_prev - m_new)
        l_scr[...] = alpha * l_scr[...] + jnp.sum(p, -1, keepdims=True)
        acc_scr[...] = alpha * acc_scr[...] + _dot(p, v)
        m_scr[...] = m_new

    cum = lf_ref[0]
    tk = cum.shape[-1]
    key = lax.broadcasted_iota(jnp.int32, cum.shape, 1)
    shift = 1
    while shift < tk:
        cum = cum + jnp.where(key >= shift, pltpu.roll(cum, shift, 1), 0.0)
        shift *= 2
    cum = cum + carry_scr[...]
    carry_scr[...] = cum[:, tk - 1:tk]
    update(_dot_nt(q_bd, ck_ref[0]) - expand(cum), cv_ref[0])

    @pl.when(kb == pl.num_programs(1) - 1)
    def _():
        s = _dot_nt(q_bd, kn_ref[0]) - expand(carry_scr[...] + cn_ref[0])
        qpos = lax.broadcasted_iota(jnp.int32, (rows, nq), 0) % nq
        kpos = lax.broadcasted_iota(jnp.int32, (rows, nq), 1)
        update(jnp.where(kpos <= qpos, s, -jnp.inf), vn_ref[0])
        o_all = acc_scr[...] / l_scr[...]
        head_of_col = lax.broadcasted_iota(jnp.int32, (nq, d), 1) // B_HEAD
        o = jnp.zeros((nq, d), F32)
        for h in range(B_HEADS):
            o = o + jnp.where(head_of_col == h, o_all[h * nq:(h + 1) * nq, :], 0.0)
        z_ref[0] = (_sigmoid(gate_ref[0]) * o).astype(BF16)


def _fox_cache_attn(q, kn, vn, ck, cv, logf_cache, c_new, gate):
    bsz, nq, d = q.shape
    plen = ck.shape[1]
    tk = min(CACHE_TILE, plen)
    rows = B_HEADS * nq
    new = pl.BlockSpec((1, nq, d), lambda b, j: (b, 0, 0))
    cache = pl.BlockSpec((1, tk, d), lambda b, j: (b, j, 0))
    return pl.pallas_call(
        functools.partial(_fox_cache_kernel, nq),
        out_shape=jax.ShapeDtypeStruct((bsz, nq, d), BF16),
        grid=(bsz, plen // tk),
        in_specs=[new, new, new, cache, cache,
                  pl.BlockSpec((1, B_HEADS, tk), lambda b, j: (b, 0, j)),
                  pl.BlockSpec((1, B_HEADS, nq), lambda b, j: (b, 0, 0)), new],
        out_specs=new,
        scratch_shapes=[pltpu.VMEM((rows, 1), F32), pltpu.VMEM((rows, 1), F32),
                        pltpu.VMEM((rows, d), F32), pltpu.VMEM((B_HEADS, 1), F32)],
        compiler_params=_cparams(("parallel", "arbitrary")),
        name="fox_cache_attn",
    )(q, kn, vn, ck, cv, logf_cache, c_new, gate)


def _ret_proj_kernel(x_ref, w_ref, cos_ref, sin_ref, q_o, k_o, v_o, g_o):
    d = D_MODEL
    xb = x_ref[...].astype(BF16)
    for h in range(C_HEADS):
        blk = slice(h * C_QK, (h + 1) * C_QK)
        cos = cos_ref[:, blk]
        sin = sin_ref[:, blk]
        q = _dot(xb, w_ref[:, h * C_QK:(h + 1) * C_QK])
        q_o[:, blk] = (q * cos + pltpu.roll(q, C_QK // 2, 1) * sin).astype(BF16)
        k = _dot(xb, w_ref[:, d + h * C_QK:d + (h + 1) * C_QK])
        k_o[:, blk] = ((k * cos + pltpu.roll(k, C_QK // 2, 1) * sin) * (C_QK ** -0.5)).astype(BF16)
    v_o[...] = _dot(xb, w_ref[:, 2 * d:4 * d]).astype(BF16)
    g_o[...] = _dot(xb, w_ref[:, 4 * d:6 * d])


def _ret_proj(x, w_in, cos, sin, table_blocks):
    m, d = x.shape
    tm = min(ROW_TILE, m)
    tab = pl.BlockSpec((tm, d), lambda i: (i % table_blocks, 0))
    w_spec, w_in = _weight(w_in)
    return pl.pallas_call(
        _ret_proj_kernel,
        out_shape=[jax.ShapeDtypeStruct((m, d), BF16), jax.ShapeDtypeStruct((m, d), BF16),
                   jax.ShapeDtypeStruct((m, 2 * d), BF16), jax.ShapeDtypeStruct((m, 2 * d), F32)],
        grid=(m // tm,),
        in_specs=[_rows(tm, d), w_spec, tab, tab],
        out_specs=[_rows(tm, d), _rows(tm, d), _rows(tm, 2 * d), _rows(tm, 2 * d)],
        compiler_params=_cparams(("parallel",)),
        name="ret_proj",
    )(x, w_in, cos, sin)


def _ret_chunk_kernel(q_ref, k_ref, v_ref, g_ref, intra_ref, qd_ref, kd_ref, cd_ref, gng_ref, gnb_ref,
                      r0_ref, z_ref, rout_ref, r_scr):
    c = pl.program_id(1)

    @pl.when(c == 0)
    def _():
        r_scr[...] = r0_ref[0]

    heads = range(C_HEADS)
    qk = [slice(h * C_QK, (h + 1) * C_QK) for h in heads]
    vg = [slice(h * C_V, (h + 1) * C_V) for h in heads]
    q = [q_ref[0, :, sl] for sl in qk]
    k = [k_ref[0, :, sl] for sl in qk]
    v = [v_ref[0, :, sl] for sl in vg]
    state = [r_scr[h] for h in heads]
    s = [_dot_nt(q[h], k[h]) * intra_ref[h] for h in heads]
    o = [_dot(s[h], v[h]) + _dot(q[h], state[h]) * qd_ref[h] for h in heads]
    for h in heads:
        r_scr[h] = state[h] * cd_ref[h] + _dot_tn(k[h].astype(F32) * kd_ref[h], v[h])
    for h in heads:
        mu = jnp.mean(o[h], -1, keepdims=True)
        oc = o[h] - mu
        var = jnp.mean(oc * oc, -1, keepdims=True)
        on = oc * lax.rsqrt(var + LN_EPS) * gng_ref[:, vg[h]] + gnb_ref[:, vg[h]]
        g = g_ref[0, :, vg[h]]
        z_ref[0, :, vg[h]] = (g * _sigmoid(g) * on).astype(BF16)

    @pl.when(c == pl.num_programs(1) - 1)
    def _():
        rout_ref[0] = r_scr[...]


def _ret_chunk(q, k, v, g, gn_g, gn_b, r0):
    bsz, t, _ = q.shape
    L = min(RET_CHUNK, t)
    log_g = jnp.log1p(-jnp.exp2(-5.0 - jnp.arange(C_HEADS, dtype=F32)))
    idx = jnp.arange(L, dtype=F32)
    rel = idx[:, None] - idx[None, :]
    intra = jnp.where(rel[None] >= 0, jnp.exp(rel[None] * log_g[:, None, None]), 0.0)
    q_decay = jnp.exp((idx[None, :, None] + 1.0) * log_g[:, None, None])
    k_decay = jnp.exp((L - 1.0 - idx[None, :, None]) * log_g[:, None, None])
    chunk_decay = jnp.exp(L * log_g)[:, None, None]
    qk = pl.BlockSpec((1, L, D_MODEL), lambda b, c: (b, c, 0))
    vg = pl.BlockSpec((1, L, 2 * D_MODEL), lambda b, c: (b, c, 0))
    st = pl.BlockSpec((1, C_HEADS, C_QK, C_V), lambda b, c: (b, 0, 0, 0))
    w_specs, w_args = _weights([intra, q_decay, k_decay, chunk_decay, gn_g, gn_b])
    return pl.pallas_call(
        _ret_chunk_kernel,
        out_shape=[jax.ShapeDtypeStruct((bsz, t, 2 * D_MODEL), BF16),
                   jax.ShapeDtypeStruct((bsz, C_HEADS, C_QK, C_V), F32)],
        grid=(bsz, t // L),
        in_specs=[qk, qk, vg, vg] + w_specs + [st],
        out_specs=[vg, st],
        scratch_shapes=[pltpu.VMEM((C_HEADS, C_QK, C_V), F32)],
        compiler_params=_cparams(("parallel", "arbitrary")),
        name="ret_chunk",
    )(q, k, v, g, *w_args, r0)


def _pad_last(w, n):
    return jnp.pad(w, [(0, 0)] * (w.ndim - 1) + [(0, n - w.shape[-1])])


def _pad_rows(w, n):
    return jnp.pad(w, [(0, 0)] * (w.ndim - 2) + [(0, n - w.shape[-2]), (0, 0)])


def _rows_of(vecs):
    return vecs.reshape(vecs.shape[0], 1, -1).astype(F32)


def _prep_rwkv(wt):
    def lora(w_in, w_out, width):
        return _pad_last(w_in, width).astype(BF16), _pad_rows(w_out, width).astype(BF16)

    w1, w2 = lora(wt["a_w1"], wt["a_w2"], 128)
    a1, a2 = lora(wt["a_a1"], wt["a_a2"], 128)
    g1, g2 = lora(wt["a_g1"], wt["a_g2"], 256)
    v1, v2 = lora(wt["a_v1"], wt["a_v2"], 128)
    stacks = dict(mu=wt["a_mu"], w_rkv=wt["a_w_rkv"].astype(BF16), w0=_rows_of(wt["a_w0"]), w1=w1, w2=w2,
                  a0=_rows_of(wt["a_a0"]), a1=a1, a2=a2, g1=g1, g2=g2,
                  k_k=_rows_of(wt["a_k_k"]), k_a=_rows_of(wt["a_k_a"]), r_k=_rows_of(wt["a_r_k"]),
                  gn_g=_rows_of(wt["a_gn_g"]), gn_b=_rows_of(wt["a_gn_b"]))
    vres = dict(v0=_rows_of(wt["a_v0"]), v1=v1, v2=v2)
    out = []
    for j in range(wt["a_mu"].shape[0]):
        w = {name: (arr, j) for name, arr in stacks.items()}
        if j > 0:
            w.update({name: (arr, j - 1) for name, arr in vres.items()})
        out.append(w)
    return out


def _prep_fox(wt):
    d = D_MODEL
    w_in = wt["b_w_in"]
    stacks = dict(w_in=w_in[:, :, :4 * d].astype(BF16), w_f=_pad_last(w_in[:, :, 4 * d:], LANES).astype(BF16),
                  b_f=_pad_last(_rows_of(wt["b_b_f"]), LANES),
                  q_gain=_rows_of(jnp.tile(wt["b_q_gain"], (1, B_HEADS))),
                  k_gain=_rows_of(jnp.tile(wt["b_k_gain"], (1, B_HEADS))))
    return [{name: (arr, j) for name, arr in stacks.items()} for j in range(w_in.shape[0])]


def _rope_tables(pos):
    inv = ROPE_BASE ** (-jnp.arange(0, C_QK, 2, dtype=F32) / C_QK)
    ang = pos[:, None].astype(F32) * inv[None]
    cos, sin = jnp.cos(ang), jnp.sin(ang)
    cos_t = jnp.tile(jnp.concatenate([cos, cos], -1), (1, C_HEADS))
    sin_t = jnp.tile(jnp.concatenate([-sin, sin], -1), (1, C_HEADS))
    return cos_t, sin_t


def _run_trunk(x, ple, pos0, a_shift, a_wkv, b_cache, c_state, wt, prep):
    bsz, t, d = x.shape
    m = bsz * t
    v_first = None
    na_shift, na_wkv, nb_k, nb_v, nb_logf, nc_state = [], [], [], [], [], []
    for i in range(DEPTH):
        kind, j = i % N_MIXERS, i // N_MIXERS
        x2 = x.reshape(m, d)
        if kind == 0:
            w = prep["rwkv"][j]
            r, k, v, wl, a, g = _rwkv_proj(x, a_shift[j], v_first, w)
            if j == 0:
                v_first = v
            seq = lambda arr: arr.reshape(bsz, t, d)
            z, s_pairs = _rwkv_chunk(seq(r), seq(wl), seq(k), seq(v), seq(a), seq(g), w,
                                     _state_to_pairs(a_wkv[j].astype(F32)))
            na_shift.append(x[:, -1])
            na_wkv.append(_pairs_to_state(s_pairs))
            z = z.reshape(m, d)
            wo = (prep["a_w_o"], j)
        elif kind == 1:
            w = prep["fox"][j]
            fresh = b_cache is None
            q_scale = B_HEAD ** -0.5 * (math.log2(math.e) if fresh else 1.0)
            qb, kf, kb, vf, vb, gate, logf_pad = _fox_proj(x2, w, q_scale)
            logf = logf_pad[:, :B_HEADS].reshape(bsz, t, B_HEADS)
            c_new = jnp.cumsum(logf, axis=1)
            seq = lambda arr: arr.reshape(bsz, t, d)
            if fresh:
                tile = min(ATTN_TILE, t)
                v_t = vb.reshape(bsz, t // tile, tile, d).transpose(0, 1, 3, 2)
                z = _fox_attn(seq(qb), seq(kb), _fox_key_bias(c_new), v_t, seq(gate))
            else:
                ck, cv, clogf = b_cache[0][j], b_cache[1][j], b_cache[2][j]
                plen = ck.shape[1]
                z = _fox_cache_attn(seq(qb), seq(kb), seq(vb), ck.reshape(bsz, plen, d),
                                    cv.reshape(bsz, plen, d), jnp.swapaxes(clogf.astype(F32), 1, 2),
                                    jnp.swapaxes(c_new, 1, 2), seq(gate))
            nb_k.append(kf.reshape(bsz, t, B_HEADS, B_HEAD))
            nb_v.append(vf.reshape(bsz, t, B_HEADS, B_HEAD))
            nb_logf.append(logf)
            z = z.reshape(m, d)
            wo = (prep["b_w_o"], j)
        else:
            cos, sin = _rope_tables(pos0 + jnp.arange(t))
            tm = min(ROW_TILE, m)
            if t >= tm:
                table_blocks = t // tm
            else:
                cos, sin = jnp.tile(cos, (tm // t, 1)), jnp.tile(sin, (tm // t, 1))
                table_blocks = 1
            q, k, v, g = _ret_proj(x2, (prep["c_w_in"], j), cos, sin, table_blocks)
            z, r_new = _ret_chunk(q.reshape(bsz, t, d), k.reshape(bsz, t, d), v.reshape(bsz, t, 2 * d),
                                  g.reshape(bsz, t, 2 * d), (prep["c_gn_g"], j), (prep["c_gn_b"], j),
                                  c_state[j].astype(F32))
            nc_state.append(r_new)
            z = z.reshape(m, 2 * d)
            wo = (prep["c_w_o"], j)
        layer = lambda name: (prep[name], i)
        x = _post_mixer(z, x2, ple[i].reshape(m, D_PLE),
                        [wo, layer("ln_mix_g"), layer("ln_mix_b"), layer("ffn_w1"), layer("ffn_w2"),
                         layer("ln_ffn_g"), layer("ln_ffn_b"), layer("ple_gate"), layer("ple_in")]
                        ).reshape(bsz, t, d)
    return (x, jnp.stack(na_shift), jnp.stack(na_wkv), jnp.stack(nb_k), jnp.stack(nb_v),
            jnp.stack(nb_logf), jnp.stack(nc_state))


def kernel(x_prompt, x_sample, p_prompt, p_sample, state_a_shift, state_a_wkv, cache_b_k, cache_b_v, cache_b_logf, state_c, ln_mix_g, ln_mix_b, ln_ffn_g, ln_ffn_b, ffn_w1, ffn_w2, ple_in, ple_gate, a_mu, a_w_rkv, a_w0, a_w1, a_w2, a_a0, a_a1, a_a2, a_v0, a_v1, a_v2, a_g1, a_g2, a_k_k, a_k_a, a_r_k, a_gn_g, a_gn_b, a_w_o, b_w_in, b_b_f, b_q_gain, b_k_gain, b_w_o, c_w_in, c_gn_g, c_gn_b, c_w_o):
    wt = dict(ln_mix_g=ln_mix_g, ln_mix_b=ln_mix_b, ln_ffn_g=ln_ffn_g, ln_ffn_b=ln_ffn_b,
              a_mu=a_mu, a_w_rkv=a_w_rkv, a_w0=a_w0, a_w1=a_w1, a_w2=a_w2, a_a0=a_a0, a_a1=a_a1, a_a2=a_a2,
              a_v0=a_v0, a_v1=a_v1, a_v2=a_v2, a_g1=a_g1, a_g2=a_g2, a_k_k=a_k_k, a_k_a=a_k_a, a_r_k=a_r_k,
              a_gn_g=a_gn_g, a_gn_b=a_gn_b, b_w_in=b_w_in, b_b_f=b_b_f, b_q_gain=b_q_gain,
              b_k_gain=b_k_gain, c_gn_g=c_gn_g, c_gn_b=c_gn_b)
    n_a, n_b, n_c = a_mu.shape[0], b_w_in.shape[0], c_w_in.shape[0]
    prep = dict(
        rwkv=_prep_rwkv(wt), fox=_prep_fox(wt),
        a_w_o=a_w_o.astype(BF16), b_w_o=b_w_o.astype(BF16), c_w_o=c_w_o.astype(BF16),
        c_w_in=c_w_in.astype(BF16), c_gn_g=_rows_of(c_gn_g), c_gn_b=_rows_of(c_gn_b),
        ffn_w1=ffn_w1.astype(BF16), ffn_w2=ffn_w2.astype(BF16),
        ple_gate=ple_gate.astype(BF16), ple_in=ple_in.astype(BF16),
        ln_mix_g=_rows_of(ln_mix_g), ln_mix_b=_rows_of(ln_mix_b),
        ln_ffn_g=_rows_of(ln_ffn_g), ln_ffn_b=_rows_of(ln_ffn_b))
    nbat = x_prompt.shape[0]
    zero_shift = jnp.zeros((n_a, nbat, D_MODEL), x_prompt.dtype)
    zero_wkv = jnp.zeros((n_a, nbat, A_HEADS, A_HEAD, A_HEAD), F32)
    zero_ret = jnp.zeros((n_c, nbat, C_HEADS, C_QK, C_V), F32)
    past_len = cache_b_k.shape[2]
    y_prompt, pa_shift, pa_wkv, pb_k, pb_v, pb_logf, pc_state = _run_trunk(
        x_prompt, p_prompt, 0, zero_shift, zero_wkv, None, zero_ret, wt, prep)
    y_sample, sa_shift, sa_wkv, sb_k, sb_v, sb_logf, sc_state = _run_trunk(
        x_sample, p_sample, past_len, state_a_shift, state_a_wkv, (cache_b_k, cache_b_v, cache_b_logf),
        state_c, wt, prep)
    return (y_prompt, y_sample, pa_shift, pa_wkv, pb_k, pb_v, pb_logf, pc_state,
            sa_shift, sa_wkv, sb_k, sb_v, sb_logf, sc_state)
```

```python
import functools
import math

import jax
import jax.numpy as jnp
from jax import lax
from jax.experimental import pallas as pl
from jax.experimental.pallas import tpu as pltpu

F32 = jnp.float32
BF16 = jnp.bfloat16

D_MODEL = 1024
DEPTH = 4
N_MIXERS = 3
D_PLE = 256
D_FF = 4 * D_MODEL
DN_ALPHA = (2.0 * DEPTH) ** 0.25
LN_EPS = 1e-5
A_HEAD = 64
A_HEADS = D_MODEL // A_HEAD
A_GN_EPS = 64e-5
B_HEAD = 64
B_HEADS = D_MODEL // B_HEAD
C_HEADS = 8
C_QK = D_MODEL // C_HEADS
C_V = 2 * D_MODEL // C_HEADS
ROPE_BASE = 10000.0
RMS_EPS = 1e-6

LANES = 128
PAIRS = D_MODEL // LANES
VMEM_LIMIT = 56 * 1024 * 1024
ROW_TILE = 512
FFN_CHUNK = 1024
RWKV_CHUNK = 64
RWKV_SEQS = 4
RET_CHUNK = 256
ATTN_TILE = 512
ATTN_GROUP = 4
CACHE_TILE = 1024


def _cparams(semantics):
    return pltpu.CompilerParams(dimension_semantics=semantics, vmem_limit_bytes=VMEM_LIMIT)


def _resident(shape):
    nd = len(shape)
    return pl.BlockSpec(shape, lambda *_: (0,) * nd, pipeline_mode=pl.Buffered(1))


def _weight(x):
    if isinstance(x, tuple):
        arr, layer = x
        nd = arr.ndim - 1
        spec = pl.BlockSpec((None,) + arr.shape[1:], lambda *_: (layer,) + (0,) * nd,
                            pipeline_mode=pl.Buffered(1))
        return spec, arr
    return _resident(x.shape), x


def _weights(xs):
    specs, args = zip(*[_weight(x) for x in xs])
    return list(specs), list(args)


def _rows(tm, width):
    return pl.BlockSpec((tm, width), lambda i: (i, 0))


def _dot(a, b):
    return jnp.dot(a.astype(BF16), b.astype(BF16), preferred_element_type=F32)


def _dot_nt(a, b):
    return lax.dot_general(a.astype(BF16), b.astype(BF16), (((1,), (1,)), ((), ())),
                           preferred_element_type=F32)


def _dot_tn(a, b):
    return lax.dot_general(a.astype(BF16), b.astype(BF16), (((0,), (0,)), ((), ())),
                           preferred_element_type=F32)


def _sigmoid(x):
    return 1.0 / (1.0 + jnp.exp(-x))


def _softplus(x):
    return jnp.maximum(x, 0.0) + jnp.log(1.0 + jnp.exp(-jnp.abs(x)))


def _layer_norm(x, g, b):
    mu = jnp.mean(x, -1, keepdims=True)
    xc = x - mu
    var = jnp.mean(xc * xc, -1, keepdims=True)
    return xc * lax.rsqrt(var + LN_EPS) * g + b


def _pair_sum(x, first):
    s0 = jnp.sum(jnp.where(first, x, 0.0), -1, keepdims=True)
    s1 = jnp.sum(jnp.where(first, 0.0, x), -1, keepdims=True)
    return jnp.where(first, s0, s1)


def _post_kernel(z_ref, x_ref, p_ref, wo_ref, g1_ref, b1_ref, w1_ref, w2_ref, g2_ref, b2_ref,
                 wg_ref, wp_ref, o_ref):
    x = x_ref[...]
    x1 = _layer_norm(DN_ALPHA * x + _dot(z_ref[...], wo_ref[...]), g1_ref[...], b1_ref[...])
    x1b = x1.astype(BF16)
    acc = jnp.zeros_like(x1)
    for f in range(D_FF // FFN_CHUNK):
        cols = slice(f * FFN_CHUNK, (f + 1) * FFN_CHUNK)
        h = jnp.maximum(_dot(x1b, w1_ref[:, cols]), 0.0)
        acc = acc + _dot(h * h, w2_ref[cols, :])
    x2 = _layer_norm(DN_ALPHA * x1 + acc, g2_ref[...], b2_ref[...])
    gate = _sigmoid(_dot(x2, wg_ref[...]))
    o_ref[...] = x2 + gate * _dot(p_ref[...], wp_ref[...])


def _post_mixer(z, x, p, params):
    m, dz = z.shape
    tm = min(ROW_TILE, m)
    d = D_MODEL
    w_specs, w_args = _weights(params)
    return pl.pallas_call(
        _post_kernel,
        out_shape=jax.ShapeDtypeStruct((m, d), F32),
        grid=(m // tm,),
        in_specs=[_rows(tm, dz), _rows(tm, d), _rows(tm, D_PLE)] + w_specs,
        out_specs=_rows(tm, d),
        compiler_params=_cparams(("parallel",)),
        name="post_mixer",
    )(z, x, p, *w_args)


def _rwkv_proj_kernel(has_vres, seg_rows, *refs):
    if has_vres:
        (x_ref, edge_ref, vf_ref, mu_ref, wrkv_ref, w0_ref, w1_ref, w2_ref, a0_ref, a1_ref, a2_ref,
         g1_ref, g2_ref, v0_ref, v1_ref, v2_ref, r_o, k_o, v_o, wl_o, a_o, g_o) = refs
    else:
        (x_ref, edge_ref, mu_ref, wrkv_ref, w0_ref, w1_ref, w2_ref, a0_ref, a1_ref, a2_ref,
         g1_ref, g2_ref, r_o, k_o, v_o, wl_o, a_o, g_o) = refs
    x = x_ref[...]
    row = lax.broadcasted_iota(jnp.int32, x.shape, 0)
    xprev = pltpu.roll(x, 1, 0)
    for s in range(x.shape[0] // seg_rows):
        xprev = jnp.where(row == s * seg_rows, edge_ref[0, s:s + 1, :], xprev)
    dx = xprev - x
    xr, xw, xk, xv, xa, xg = [(x + dx * mu_ref[i:i + 1, :]).astype(BF16) for i in range(6)]
    r_o[...] = _dot(xr, wrkv_ref[0])
    k_o[...] = _dot(xk, wrkv_ref[1])
    v = _dot(xv, wrkv_ref[2])
    if has_vres:
        mix = _sigmoid(v0_ref[...] + _dot(_dot(xv, v1_ref[...]), v2_ref[...]))
        v = v + (vf_ref[...] - v) * mix
    v_o[...] = v
    pre = w0_ref[...] + _dot(jnp.tanh(_dot(xw, w1_ref[...])), w2_ref[...])
    w_log = -_softplus(-pre) - 0.5
    wl_o[...] = -jnp.exp(w_log)
    a_o[...] = _sigmoid(a0_ref[...] + _dot(_dot(xa, a1_ref[...]), a2_ref[...]))
    g_o[...] = _dot(_sigmoid(_dot(xg, g1_ref[...])), g2_ref[...])


def _rwkv_proj(x, shift, vfirst, w):
    bsz, t, d = x.shape
    m = bsz * t
    tm = min(ROW_TILE, m)
    seg_rows = min(tm, t)
    seg_last = x.reshape(bsz, t // seg_rows, seg_rows, d)[:, :, -1]
    edge = jnp.concatenate([shift[:, None, :].astype(F32), seg_last[:, :-1]], axis=1)
    edge = edge.reshape(m // tm, tm // seg_rows, d)
    sublanes = 8
    if edge.shape[1] < sublanes:
        edge = jnp.pad(edge, ((0, 0), (0, sublanes - edge.shape[1]), (0, 0)))
    x = x.reshape(m, d)
    has_vres = vfirst is not None
    params = [w["mu"], w["w_rkv"], w["w0"], w["w1"], w["w2"], w["a0"], w["a1"], w["a2"], w["g1"], w["g2"]]
    if has_vres:
        params += [w["v0"], w["v1"], w["v2"]]
    w_specs, w_args = _weights(params)
    edge_spec = pl.BlockSpec((1,) + edge.shape[1:], lambda i: (i, 0, 0))
    return pl.pallas_call(
        functools.partial(_rwkv_proj_kernel, has_vres, seg_rows),
        out_shape=[jax.ShapeDtypeStruct((m, d), F32)] * 6,
        grid=(m // tm,),
        in_specs=[_rows(tm, d), edge_spec] + ([_rows(tm, d)] if has_vres else []) + w_specs,
        out_specs=[_rows(tm, d)] * 6,
        compiler_params=_cparams(("parallel",)),
        name="rwkv_proj",
    )(x, edge, *([vfirst] if has_vres else []), *w_args)


def _rwkv_chunk_kernel(L, nb, r_ref, wl_ref, k_ref, v_ref, a_ref, g_ref, kk_ref, ka_ref, rk_ref,
                       gng_ref, gnb_ref, s0_ref, z_ref, sout_ref, s_scr):
    c = pl.program_id(1)

    @pl.when(c == 0)
    def _():
        s_scr[...] = s0_ref[...]

    L2 = 2 * L
    first = lax.broadcasted_iota(jnp.int32, (L, LANES), 1) < A_HEAD
    row = lax.broadcasted_iota(jnp.int32, (L, L2), 0)
    col = lax.broadcasted_iota(jnp.int32, (L, L2), 1)
    first_sq = col < L
    col = jnp.where(first_sq, col, col - L)
    strict = row > col
    incl = row >= col
    eye = jnp.where(row == col, 1.0, 0.0).astype(F32)
    vrow = lax.broadcasted_iota(jnp.int32, (LANES, LANES), 0) < A_HEAD
    vcol = lax.broadcasted_iota(jnp.int32, (LANES, LANES), 1) < A_HEAD
    same_head = vrow == vcol

    def stack(x, mask=first):
        return jnp.concatenate([jnp.where(mask, x, 0.0), jnp.where(mask, 0.0, x)], axis=0).astype(BF16)

    units = [(bi, p) for bi in range(nb) for p in range(PAIRS)]
    pairs = range(len(units))
    sls = [slice(p * LANES, (p + 1) * LANES) for _, p in units]
    r = [r_ref[bi, :, sls[i]] for i, (bi, _) in enumerate(units)]
    wl = [wl_ref[bi, :, sls[i]] for i, (bi, _) in enumerate(units)]
    v = [v_ref[bi, :, sls[i]] for i, (bi, _) in enumerate(units)]
    a = [a_ref[bi, :, sls[i]] for i, (bi, _) in enumerate(units)]
    k_raw = [k_ref[bi, :, sls[i]] for i, (bi, _) in enumerate(units)]
    kk = [k_raw[p] * kk_ref[:, sls[p]] for p in pairs]
    kk = [x * lax.rsqrt(jnp.maximum(_pair_sum(x * x, first), 1e-24)) for x in kk]
    k = [k_raw[p] * (1.0 + (a[p] - 1.0) * ka_ref[:, sls[p]]) for p in pairs]
    b = [kk[p] * a[p] for p in pairs]

    step_row = lax.broadcasted_iota(jnp.int32, (L, LANES), 0)

    def cumsum(x):
        shift = 1
        while shift < L:
            x = x + jnp.where(step_row >= shift, pltpu.roll(x, shift, 0), 0.0)
            shift *= 2
        return x

    cum = [cumsum(x) for x in wl]
    g_t = [jnp.exp(x) for x in cum]
    g_prev = [jnp.exp(cum[p] - wl[p]) for p in pairs]
    g_inv = [jnp.exp(-x) for x in cum]
    g_end = [x[L - 1:L, :] for x in g_t]
    kinv = [k[p] * g_inv[p] for p in pairs]
    binv = [b[p] * g_inv[p] for p in pairs]
    lhs = [jnp.concatenate([kk[p] * g_prev[p], r[p] * g_t[p]], axis=0).astype(BF16) for p in pairs]
    rhs = [jnp.concatenate([stack(kinv[p]), stack(binv[p])], axis=0) for p in pairs]
    s_v = [stack(x) for x in v]
    gram = [_dot_nt(lhs[p], rhs[p]) for p in pairs]
    m_k = [jnp.where(strict, x[:L, :L2], 0.0).astype(BF16) for x in gram]
    n_k = [jnp.where(incl, x[L:, :L2], 0.0).astype(BF16) for x in gram]
    n_b = [jnp.where(incl, -x[L:, L2:], 0.0).astype(BF16) for x in gram]
    x_pow = [jnp.where(strict, -x[:L, L2:], 0.0) for x in gram]
    t_inv = [eye + x for x in x_pow]
    x_pow = [_dot(x, stack(x, first_sq)) for x in x_pow]
    n = 2
    while n < L:
        x_bd = [stack(x, first_sq) for x in x_pow]
        if 2 * n < L:
            both = [_dot(jnp.concatenate([t_inv[p], x_pow[p]], axis=0), x_bd[p]) for p in pairs]
            t_inv = [t_inv[p] + both[p][:L] for p in pairs]
            x_pow = [x[L:] for x in both]
        else:
            t_inv = [t_inv[p] + _dot(t_inv[p], x_bd[p]) for p in pairs]
        n *= 2
    s_prev = [s_scr[bi, p] for bi, p in units]
    from_state = [_dot_nt(lhs[p], s_prev[p]) for p in pairs]
    y_part = [from_state[p][L:] + _dot(n_k[p], s_v[p]) for p in pairs]
    r_bd = [stack(from_state[p][:L] + _dot(m_k[p], s_v[p])) for p in pairs]
    nb_t = [_dot(n_b[p], stack(t_inv[p], first_sq)) for p in pairs]
    both = [_dot(jnp.concatenate([t_inv[p], nb_t[p]], axis=0), r_bd[p]) for p in pairs]
    u = [x[:L] for x in both]
    y = [y_part[p] + both[p][L:] for p in pairs]
    vu = [jnp.concatenate([v[p], -u[p]], axis=0).astype(BF16) for p in pairs]
    ends = [jnp.concatenate([kinv[p] * g_end[p], binv[p] * g_end[p]], axis=0).astype(BF16) for p in pairs]
    outer = [_dot_tn(vu[p], ends[p]) for p in pairs]
    s_new = [s_prev[p] * g_end[p] + jnp.where(same_head, outer[p], 0.0) for p in pairs]
    for i, (bi, p) in enumerate(units):
        s_scr[bi, p] = s_new[i]
    for i, (bi, _) in enumerate(units):
        sl = sls[i]
        mu = _pair_sum(y[i], first) * (1.0 / A_HEAD)
        yc = y[i] - mu
        var = _pair_sum(yc * yc, first) * (1.0 / A_HEAD)
        yn = yc * lax.rsqrt(var + A_GN_EPS) * gng_ref[:, sl] + gnb_ref[:, sl]
        bonus = _pair_sum(r[i] * k[i] * rk_ref[:, sl], first) * v[i]
        z_ref[bi, :, sl] = ((yn + bonus) * g_ref[bi, :, sl]).astype(BF16)

    @pl.when(c == pl.num_programs(1) - 1)
    def _():
        sout_ref[...] = s_scr[...]


def _rwkv_chunk(r, wl, k, v, a, g, w, s0_pairs):
    bsz, t, d = r.shape
    L = min(RWKV_CHUNK, t)
    nb = math.gcd(bsz, RWKV_SEQS)
    seq = pl.BlockSpec((nb, L, d), lambda b, c: (b, c, 0))
    st = pl.BlockSpec((nb, PAIRS, LANES, LANES), lambda b, c: (b, 0, 0, 0))
    w_specs, w_args = _weights([w["k_k"], w["k_a"], w["r_k"], w["gn_g"], w["gn_b"]])
    return pl.pallas_call(
        functools.partial(_rwkv_chunk_kernel, L, nb),
        out_shape=[jax.ShapeDtypeStruct((bsz, t, d), BF16),
                   jax.ShapeDtypeStruct((bsz, PAIRS, LANES, LANES), F32)],
        grid=(bsz // nb, t // L),
        in_specs=[seq] * 6 + w_specs + [st],
        out_specs=[seq, st],
        scratch_shapes=[pltpu.VMEM((nb, PAIRS, LANES, LANES), F32)],
        compiler_params=_cparams(("parallel", "arbitrary")),
        name="rwkv_chunk",
    )(r, wl, k, v, a, g, *w_args, s0_pairs)


def _state_to_pairs(s):
    bsz = s.shape[0]
    s = s.reshape(bsz, PAIRS, 2, A_HEAD, A_HEAD)
    z = jnp.zeros_like(s[:, :, 0])
    top = jnp.concatenate([s[:, :, 0], z], axis=-1)
    bot = jnp.concatenate([z, s[:, :, 1]], axis=-1)
    return jnp.concatenate([top, bot], axis=-2)


def _pairs_to_state(sp):
    bsz = sp.shape[0]
    s0 = sp[:, :, :A_HEAD, :A_HEAD]
    s1 = sp[:, :, A_HEAD:, A_HEAD:]
    return jnp.stack([s0, s1], axis=2).reshape(bsz, A_HEADS, A_HEAD, A_HEAD)


def _fox_proj_kernel(q_scale, x_ref, w_ref, wf_ref, bf_ref, qg_ref, kg_ref,
                     qb_o, kf_o, kb_o, vf_o, vb_o, gate_o, logf_o):
    d = D_MODEL
    xb = x_ref[...].astype(BF16)
    first = lax.broadcasted_iota(jnp.int32, (xb.shape[0], LANES), 1) < B_HEAD

    def rms(t, gain_ref, blk):
        ms = _pair_sum(t * t, first) * (1.0 / B_HEAD)
        return t * lax.rsqrt(ms + RMS_EPS) * gain_ref[:, blk]

    for j in range(PAIRS):
        blk = slice(j * LANES, (j + 1) * LANES)
        q = rms(_dot(xb, w_ref[:, j * LANES:(j + 1) * LANES]), qg_ref, blk)
        qb_o[:, blk] = (q * q_scale).astype(BF16)
        k = rms(_dot(xb, w_ref[:, d + j * LANES:d + (j + 1) * LANES]), kg_ref, blk)
        kf_o[:, blk] = k
        kb_o[:, blk] = k.astype(BF16)
    v = _dot(xb, w_ref[:, 2 * d:3 * d])
    vf_o[...] = v
    vb_o[...] = v.astype(BF16)
    gate_o[...] = _dot(xb, w_ref[:, 3 * d:4 * d])
    logf_o[...] = -_softplus(-(_dot(xb, wf_ref[...]) + bf_ref[...]))


def _fox_proj(x, w, q_scale):
    m, d = x.shape
    tm = min(ROW_TILE, m)
    outs = [jax.ShapeDtypeStruct((m, d), BF16), jax.ShapeDtypeStruct((m, d), F32),
            jax.ShapeDtypeStruct((m, d), BF16), jax.ShapeDtypeStruct((m, d), F32),
            jax.ShapeDtypeStruct((m, d), BF16), jax.ShapeDtypeStruct((m, d), F32),
            jax.ShapeDtypeStruct((m, LANES), F32)]
    w_specs, w_args = _weights([w["w_in"], w["w_f"], w["b_f"], w["q_gain"], w["k_gain"]])
    return pl.pallas_call(
        functools.partial(_fox_proj_kernel, q_scale),
        out_shape=outs,
        grid=(m // tm,),
        in_specs=[_rows(tm, d)] + w_specs,
        out_specs=[_rows(tm, d)] * 6 + [_rows(tm, LANES)],
        compiler_params=_cparams(("parallel",)),
        name="fox_proj",
    )(x, *w_args)


BIAS_PIECES = 3


def _fox_attn_kernel(tile, q_ref, k_ref, kb_ref, vt_ref, gate_ref, z_ref, m_scr, l_scr, acc_scr):
    qi = pl.program_id(2)
    lane = lax.broadcasted_iota(jnp.int32, (tile, LANES), 1)
    first = lane < B_HEAD
    ones = jnp.where(lane % B_HEAD < BIAS_PIECES, 1.0, 0.0).astype(BF16)
    q = q_ref[0]
    q_heads = (jnp.where(first, q, ones), jnp.where(first, ones, q))
    causal = (lax.broadcasted_iota(jnp.int32, (tile, tile), 0) <=
              lax.broadcasted_iota(jnp.int32, (tile, tile), 1))
    m_scr[...] = jnp.full_like(m_scr, -jnp.inf)
    l_scr[...] = jnp.zeros_like(l_scr)
    acc_scr[...] = jnp.zeros_like(acc_scr)

    def step(blocks):
        k_heads, vts = [], []
        for kb, _ in blocks:
            start = pl.multiple_of(kb * tile, tile)
            k = k_ref[0, pl.ds(start, tile), :]
            kbias = kb_ref[0, pl.ds(start, tile), :]
            k_heads.append((jnp.where(first, k, kbias), jnp.where(first, kbias, k)))
            vts.append(vt_ref[0, kb])
        m_all, l_all, acc_all = m_scr[...], l_scr[...], acc_scr[...]
        heads = range(2)
        rows = [slice(h * B_HEAD, (h + 1) * B_HEAD) for h in heads]
        s = [[_dot_nt(kh[h], q_heads[h]) for kh in k_heads] for h in heads]
        s = [[jnp.where(causal, x, -jnp.inf) if diag else x for x, (_, diag) in zip(s[h], blocks)]
             for h in heads]
        m_prev = [m_all[h:h + 1] for h in heads]
        m_new = list(m_prev)
        for h in heads:
            for x in s[h]:
                m_new[h] = jnp.maximum(m_new[h], jnp.max(x, 0, keepdims=True))
        p = [[jnp.exp2(x - m_new[h]) for x in s[h]] for h in heads]
        alpha = [jnp.exp2(m_prev[h] - m_new[h]) for h in heads]
        l_new = [alpha[h] * l_all[h:h + 1] for h in heads]
        acc_new = [alpha[h] * acc_all[rows[h]] for h in heads]
        for h in heads:
            for x, vt in zip(p[h], vts):
                l_new[h] = l_new[h] + jnp.sum(x, 0, keepdims=True)
                acc_new[h] = acc_new[h] + _dot(vt[rows[h]], x)
        m_scr[...] = jnp.concatenate(m_new, 0)
        l_scr[...] = jnp.concatenate(l_new, 0)
        acc_scr[...] = jnp.concatenate(acc_new, 0)

    @pl.loop(0, qi // ATTN_GROUP)
    def _(j):
        step([(ATTN_GROUP * j + i, False) for i in range(ATTN_GROUP)])

    for rem in range(ATTN_GROUP):
        @pl.when(qi % ATTN_GROUP == rem)
        def _(rem=rem):
            step([(qi - rem + i, False) for i in range(rem)] + [(qi, True)])

    inv_l = 1.0 / l_scr[...]
    o_t = jnp.concatenate([acc_scr[:B_HEAD, :] * inv_l[0:1], acc_scr[B_HEAD:, :] * inv_l[1:2]], 0)
    z_ref[0] = (_sigmoid(gate_ref[0]) * o_t.T).astype(BF16)


def _fox_attn(q, k, k_bias, v_t, gate):
    bsz, t, d = q.shape
    tile = min(ATTN_TILE, t)
    nq = t // tile
    qspec = pl.BlockSpec((1, tile, LANES), lambda b, p, i: (b, i, p))
    kspec = pl.BlockSpec((1, t, LANES), lambda b, p, i: (b, 0, p))
    vspec = pl.BlockSpec((1, nq, LANES, tile), lambda b, p, i: (b, 0, p, 0))
    return pl.pallas_call(
        functools.partial(_fox_attn_kernel, tile),
        out_shape=jax.ShapeDtypeStruct((bsz, t, d), BF16),
        grid=(bsz, PAIRS, nq),
        in_specs=[qspec, kspec, kspec, vspec, qspec],
        out_specs=qspec,
        scratch_shapes=[pltpu.VMEM((2, tile), F32), pltpu.VMEM((2, tile), F32),
                        pltpu.VMEM((LANES, tile), F32)],
        compiler_params=_cparams(("parallel", "parallel", "arbitrary")),
        name="fox_attn",
    )(q, k, k_bias, v_t, gate)


def _fox_key_bias(c_new):
    bsz, t, h = c_new.shape
    rest = -c_new * math.log2(math.e)
    pieces = []
    for _ in range(BIAS_PIECES):
        top = lax.bitcast_convert_type(
            lax.bitcast_convert_type(rest, jnp.uint32) & jnp.uint32(0xFFFF0000), F32)
        pieces.append(top.astype(BF16))
        rest = rest - top
    packed = jnp.pad(jnp.stack(pieces, -1), ((0, 0), (0, 0), (0, 0), (0, B_HEAD - BIAS_PIECES)))
    packed = packed.reshape(bsz, t, h // 2, 2, B_HEAD)[:, :, :, ::-1, :]
    return packed.reshape(bsz, t, h * B_HEAD)


def _fox_cache_kernel(nq, q_ref, kn_ref, vn_ref, ck_ref, cv_ref, lf_ref, cn_ref, gate_ref, z_ref,
                      m_scr, l_scr, acc_scr, carry_scr):
    kb = pl.program_id(1)
    d = D_MODEL
    rows = B_HEADS * nq
    q = q_ref[0]
    row_head = lax.broadcasted_iota(jnp.int32, (rows, d), 0) // nq
    col_head = lax.broadcasted_iota(jnp.int32, (rows, d), 1) // B_HEAD
    q_rep = jnp.concatenate([q] * B_HEADS, axis=0)
    q_bd = jnp.where(row_head == col_head, q_rep, jnp.zeros_like(q_rep))

    def expand(bias):
        n = bias.shape[-1]
        return jnp.broadcast_to(bias[:, None, :], (B_HEADS, nq, n)).reshape(rows, n)

    @pl.when(kb == 0)
    def _():
        m_scr[...] = jnp.full_like(m_scr, -jnp.inf)
        l_scr[...] = jnp.zeros_like(l_scr)
        acc_scr[...] = jnp.zeros_like(acc_scr)
        carry_scr[...] = jnp.zeros_like(carry_scr)

    def update(s, v):
        m_prev = m_scr[...]
        m_new = jnp.maximum(m_prev, jnp.max(s, -1, keepdims=True))
        p = jnp.exp(s - m_new)
        alpha = jnp.exp(m_prev - m_new)
        l_scr[...] = alpha * l_scr[...] + jnp.sum(p, -1, keepdims=True)
        acc_scr[...] = alpha * acc_scr[...] + _dot(p, v)
        m_scr[...] = m_new

    cum = lf_ref[0]
    tk = cum.shape[-1]
    key = lax.broadcasted_iota(jnp.int32, cum.shape, 1)
    shift = 1
    while shift < tk:
        cum = cum + jnp.where(key >= shift, pltpu.roll(cum, shift, 1), 0.0)
        shift *= 2
    cum = cum + carry_scr[...]
    carry_scr[...] = cum[:, tk - 1:tk]
    update(_dot_nt(q_bd, ck_ref[0]) - expand(cum), cv_ref[0])

    @pl.when(kb == pl.num_programs(1) - 1)
    def _():
        s = _dot_nt(q_bd, kn_ref[0]) - expand(carry_scr[...] + cn_ref[0])
        qpos = lax.broadcasted_iota(jnp.int32, (rows, nq), 0) % nq
        kpos = lax.broadcasted_iota(jnp.int32, (rows, nq), 1)
        update(jnp.where(kpos <= qpos, s, -jnp.inf), vn_ref[0])
        o_all = acc_scr[...] / l_scr[...]
        head_of_col = lax.broadcasted_iota(jnp.int32, (nq, d), 1) // B_HEAD
        o = jnp.zeros((nq, d), F32)
        for h in range(B_HEADS):
            o = o + jnp.where(head_of_col == h, o_all[h * nq:(h + 1) * nq, :], 0.0)
        z_ref[0] = (_sigmoid(gate_ref[0]) * o).astype(BF16)


def _fox_cache_attn(q, kn, vn, ck, cv, logf_cache, c_new, gate):
    bsz, nq, d = q.shape
    plen = ck.shape[1]
    tk = min(CACHE_TILE, plen)
    rows = B_HEADS * nq
    new = pl.BlockSpec((1, nq, d), lambda b, j: (b, 0, 0))
    cache = pl.BlockSpec((1, tk, d), lambda b, j: (b, j, 0))
    return pl.pallas_call(
        functools.partial(_fox_cache_kernel, nq),
        out_shape=jax.ShapeDtypeStruct((bsz, nq, d), BF16),
        grid=(bsz, plen // tk),
        in_specs=[new, new, new, cache, cache,
                  pl.BlockSpec((1, B_HEADS, tk), lambda b, j: (b, 0, j)),
                  pl.BlockSpec((1, B_HEADS, nq), lambda b, j: (b, 0, 0)), new],
        out_specs=new,
        scratch_shapes=[pltpu.VMEM((rows, 1), F32), pltpu.VMEM((rows, 1), F32),
                        pltpu.VMEM((rows, d), F32), pltpu.VMEM((B_HEADS, 1), F32)],
        compiler_params=_cparams(("parallel", "arbitrary")),
        name="fox_cache_attn",
    )(q, kn, vn, ck, cv, logf_cache, c_new, gate)


def _ret_proj_kernel(x_ref, w_ref, cos_ref, sin_ref, q_o, k_o, v_o, g_o):
    d = D_MODEL
    xb = x_ref[...].astype(BF16)
    for h in range(C_HEADS):
        blk = slice(h * C_QK, (h + 1) * C_QK)
        cos = cos_ref[:, blk]
        sin = sin_ref[:, blk]
        q = _dot(xb, w_ref[:, h * C_QK:(h + 1) * C_QK])
        q_o[:, blk] = (q * cos + pltpu.roll(q, C_QK // 2, 1) * sin).astype(BF16)
        k = _dot(xb, w_ref[:, d + h * C_QK:d + (h + 1) * C_QK])
        k_o[:, blk] = ((k * cos + pltpu.roll(k, C_QK // 2, 1) * sin) * (C_QK ** -0.5)).astype(BF16)
    v_o[...] = _dot(xb, w_ref[:, 2 * d:4 * d]).astype(BF16)
    g_o[...] = _dot(xb, w_ref[:, 4 * d:6 * d])


def _ret_proj(x, w_in, cos, sin, table_blocks):
    m, d = x.shape
    tm = min(ROW_TILE, m)
    tab = pl.BlockSpec((tm, d), lambda i: (i % table_blocks, 0))
    w_spec, w_in = _weight(w_in)
    return pl.pallas_call(
        _ret_proj_kernel,
        out_shape=[jax.ShapeDtypeStruct((m, d), BF16), jax.ShapeDtypeStruct((m, d), BF16),
                   jax.ShapeDtypeStruct((m, 2 * d), BF16), jax.ShapeDtypeStruct((m, 2 * d), F32)],
        grid=(m // tm,),
        in_specs=[_rows(tm, d), w_spec, tab, tab],
        out_specs=[_rows(tm, d), _rows(tm, d), _rows(tm, 2 * d), _rows(tm, 2 * d)],
        compiler_params=_cparams(("parallel",)),
        name="ret_proj",
    )(x, w_in, cos, sin)


def _ret_chunk_kernel(q_ref, k_ref, v_ref, g_ref, intra_ref, qd_ref, kd_ref, cd_ref, gng_ref, gnb_ref,
                      r0_ref, z_ref, rout_ref, r_scr):
    c = pl.program_id(1)

    @pl.when(c == 0)
    def _():
        r_scr[...] = r0_ref[0]

    heads = range(C_HEADS)
    qk = [slice(h * C_QK, (h + 1) * C_QK) for h in heads]
    vg = [slice(h * C_V, (h + 1) * C_V) for h in heads]
    q = [q_ref[0, :, sl] for sl in qk]
    k = [k_ref[0, :, sl] for sl in qk]
    v = [v_ref[0, :, sl] for sl in vg]
    state = [r_scr[h] for h in heads]
    s = [_dot_nt(q[h], k[h]) * intra_ref[h] for h in heads]
    o = [_dot(s[h], v[h]) + _dot(q[h], state[h]) * qd_ref[h] for h in heads]
    for h in heads:
        r_scr[h] = state[h] * cd_ref[h] + _dot_tn(k[h].astype(F32) * kd_ref[h], v[h])
    for h in heads:
        mu = jnp.mean(o[h], -1, keepdims=True)
        oc = o[h] - mu
        var = jnp.mean(oc * oc, -1, keepdims=True)
        on = oc * lax.rsqrt(var + LN_EPS) * gng_ref[:, vg[h]] + gnb_ref[:, vg[h]]
        g = g_ref[0, :, vg[h]]
        z_ref[0, :, vg[h]] = (g * _sigmoid(g) * on).astype(BF16)

    @pl.when(c == pl.num_programs(1) - 1)
    def _():
        rout_ref[0] = r_scr[...]


def _ret_chunk(q, k, v, g, gn_g, gn_b, r0):
    bsz, t, _ = q.shape
    L = min(RET_CHUNK, t)
    log_g = jnp.log1p(-jnp.exp2(-5.0 - jnp.arange(C_HEADS, dtype=F32)))
    idx = jnp.arange(L, dtype=F32)
    rel = idx[:, None] - idx[None, :]
    intra = jnp.where(rel[None] >= 0, jnp.exp(rel[None] * log_g[:, None, None]), 0.0)
    q_decay = jnp.exp((idx[None, :, None] + 1.0) * log_g[:, None, None])
    k_decay = jnp.exp((L - 1.0 - idx[None, :, None]) * log_g[:, None, None])
    chunk_decay = jnp.exp(L * log_g)[:, None, None]
    qk = pl.BlockSpec((1, L, D_MODEL), lambda b, c: (b, c, 0))
    vg = pl.BlockSpec((1, L, 2 * D_MODEL), lambda b, c: (b, c, 0))
    st = pl.BlockSpec((1, C_HEADS, C_QK, C_V), lambda b, c: (b, 0, 0, 0))
    w_specs, w_args = _weights([intra, q_decay, k_decay, chunk_decay, gn_g, gn_b])
    return pl.pallas_call(
        _ret_chunk_kernel,
        out_shape=[jax.ShapeDtypeStruct((bsz, t, 2 * D_MODEL), BF16),
                   jax.ShapeDtypeStruct((bsz, C_HEADS, C_QK, C_V), F32)],
        grid=(bsz, t // L),
        in_specs=[qk, qk, vg, vg] + w_specs + [st],
        out_specs=[vg, st],
        scratch_shapes=[pltpu.VMEM((C_HEADS, C_QK, C_V), F32)],
        compiler_params=_cparams(("parallel", "arbitrary")),
        name="ret_chunk",
    )(q, k, v, g, *w_args, r0)


def _pad_last(w, n):
    return jnp.pad(w, [(0, 0)] * (w.ndim - 1) + [(0, n - w.shape[-1])])


def _pad_rows(w, n):
    return jnp.pad(w, [(0, 0)] * (w.ndim - 2) + [(0, n - w.shape[-2]), (0, 0)])


def _rows_of(vecs):
    return vecs.reshape(vecs.shape[0], 1, -1).astype(F32)


def _prep_rwkv(wt):
    def lora(w_in, w_out, width):
        return _pad_last(w_in, width).astype(BF16), _pad_rows(w_out, width).astype(BF16)

    w1, w2 = lora(wt["a_w1"], wt["a_w2"], 128)
    a1, a2 = lora(wt["a_a1"], wt["a_a2"], 128)
    g1, g2 = lora(wt["a_g1"], wt["a_g2"], 256)
    v1, v2 = lora(wt["a_v1"], wt["a_v2"], 128)
    stacks = dict(mu=wt["a_mu"], w_rkv=wt["a_w_rkv"].astype(BF16), w0=_rows_of(wt["a_w0"]), w1=w1, w2=w2,
                  a0=_rows_of(wt["a_a0"]), a1=a1, a2=a2, g1=g1, g2=g2,
                  k_k=_rows_of(wt["a_k_k"]), k_a=_rows_of(wt["a_k_a"]), r_k=_rows_of(wt["a_r_k"]),
                  gn_g=_rows_of(wt["a_gn_g"]), gn_b=_rows_of(wt["a_gn_b"]))
    vres = dict(v0=_rows_of(wt["a_v0"]), v1=v1, v2=v2)
    out = []
    for j in range(wt["a_mu"].shape[0]):
        w = {name: (arr, j) for name, arr in stacks.items()}
        if j > 0:
            w.update({name: (arr, j - 1) for name, arr in vres.items()})
        out.append(w)
    return out


def _prep_fox(wt):
    d = D_MODEL
    w_in = wt["b_w_in"]
    stacks = dict(w_in=w_in[:, :, :4 * d].astype(BF16), w_f=_pad_last(w_in[:, :, 4 * d:], LANES).astype(BF16),
                  b_f=_pad_last(_rows_of(wt["b_b_f"]), LANES),
                  q_gain=_rows_of(jnp.tile(wt["b_q_gain"], (1, B_HEADS))),
                  k_gain=_rows_of(jnp.tile(wt["b_k_gain"], (1, B_HEADS))))
    return [{name: (arr, j) for name, arr in stacks.items()} for j in range(w_in.shape[0])]


def _rope_tables(pos):
    inv = ROPE_BASE ** (-jnp.arange(0, C_QK, 2, dtype=F32) / C_QK)
    ang = pos[:, None].astype(F32) * inv[None]
    cos, sin = jnp.cos(ang), jnp.sin(ang)
    cos_t = jnp.tile(jnp.concatenate([cos, cos], -1), (1, C_HEADS))
    sin_t = jnp.tile(jnp.concatenate([-sin, sin], -1), (1, C_HEADS))
    return cos_t, sin_t


def _run_trunk(x, ple, pos0, a_shift, a_wkv, b_cache, c_state, prep):
    bsz, t, d = x.shape
    m = bsz * t
    v_first = None
    na_shift, na_wkv, nb_k, nb_v, nb_logf, nc_state = [], [], [], [], [], []
    for i in range(DEPTH):
        kind, j = i % N_MIXERS, i // N_MIXERS
        x2 = x.reshape(m, d)
        if kind == 0:
            w = prep["rwkv"][j]
            r, k, v, wl, a, g = _rwkv_proj(x, a_shift[j], v_first, w)
            if j == 0:
                v_first = v
            seq = lambda arr: arr.reshape(bsz, t, d)
            z, s_pairs = _rwkv_chunk(seq(r), seq(wl), seq(k), seq(v), seq(a), seq(g), w,
                                     _state_to_pairs(a_wkv[j].astype(F32)))
            na_shift.append(x[:, -1])
            na_wkv.append(_pairs_to_state(s_pairs))
            z = z.reshape(m, d)
            wo = (prep["a_w_o"], j)
        elif kind == 1:
            w = prep["fox"][j]
            fresh = b_cache is None
            q_scale = B_HEAD ** -0.5 * (math.log2(math.e) if fresh else 1.0)
            qb, kf, kb, vf, vb, gate, logf_pad = _fox_proj(x2, w, q_scale)
            logf = logf_pad[:, :B_HEADS].reshape(bsz, t, B_HEADS)
            c_new = jnp.cumsum(logf, axis=1)
            seq = lambda arr: arr.reshape(bsz, t, d)
            if fresh:
                tile = min(ATTN_TILE, t)
                v_t = vb.reshape(bsz, t // tile, tile, d).transpose(0, 1, 3, 2)
                z = _fox_attn(seq(qb), seq(kb), _fox_key_bias(c_new), v_t, seq(gate))
            else:
                ck, cv, clogf = b_cache[0][j], b_cache[1][j], b_cache[2][j]
                plen = ck.shape[1]
                z = _fox_cache_attn(seq(qb), seq(kb), seq(vb), ck.reshape(bsz, plen, d),
                                    cv.reshape(bsz, plen, d), jnp.swapaxes(clogf.astype(F32), 1, 2),
                                    jnp.swapaxes(c_new, 1, 2), seq(gate))
            nb_k.append(kf.reshape(bsz, t, B_HEADS, B_HEAD))
            nb_v.append(vf.reshape(bsz, t, B_HEADS, B_HEAD))
            nb_logf.append(logf)
            z = z.reshape(m, d)
            wo = (prep["b_w_o"], j)
        else:
            cos, sin = _rope_tables(pos0 + jnp.arange(t))
            tm = min(ROW_TILE, m)
            if t >= tm:
                table_blocks = t // tm
            else:
                cos, sin = jnp.tile(cos, (tm // t, 1)), jnp.tile(sin, (tm // t, 1))
                table_blocks = 1
            q, k, v, g = _ret_proj(x2, (prep["c_w_in"], j), cos, sin, table_blocks)
            z, r_new = _ret_chunk(q.reshape(bsz, t, d), k.reshape(bsz, t, d), v.reshape(bsz, t, 2 * d),
                                  g.reshape(bsz, t, 2 * d), (prep["c_gn_g"], j), (prep["c_gn_b"], j),
                                  c_state[j].astype(F32))
            nc_state.append(r_new)
            z = z.reshape(m, 2 * d)
            wo = (prep["c_w_o"], j)
        layer = lambda name: (prep[name], i)
        x = _post_mixer(z, x2, ple[i].reshape(m, D_PLE),
                        [wo, layer("ln_mix_g"), layer("ln_mix_b"), layer("ffn_w1"), layer("ffn_w2"),
                         layer("ln_ffn_g"), layer("ln_ffn_b"), layer("ple_gate"), layer("ple_in")]
                        ).reshape(bsz, t, d)
    return (x, jnp.stack(na_shift), jnp.stack(na_wkv), jnp.stack(nb_k), jnp.stack(nb_v),
            jnp.stack(nb_logf), jnp.stack(nc_state))


def kernel(x_prompt, x_sample, p_prompt, p_sample, state_a_shift, state_a_wkv, cache_b_k, cache_b_v, cache_b_logf, state_c, ln_mix_g, ln_mix_b, ln_ffn_g, ln_ffn_b, ffn_w1, ffn_w2, ple_in, ple_gate, a_mu, a_w_rkv, a_w0, a_w1, a_w2, a_a0, a_a1, a_a2, a_v0, a_v1, a_v2, a_g1, a_g2, a_k_k, a_k_a, a_r_k, a_gn_g, a_gn_b, a_w_o, b_w_in, b_b_f, b_q_gain, b_k_gain, b_w_o, c_w_in, c_gn_g, c_gn_b, c_w_o):
    wt = dict(a_mu=a_mu, a_w_rkv=a_w_rkv, a_w0=a_w0, a_w1=a_w1, a_w2=a_w2, a_a0=a_a0, a_a1=a_a1, a_a2=a_a2,
              a_v0=a_v0, a_v1=a_v1, a_v2=a_v2, a_g1=a_g1, a_g2=a_g2, a_k_k=a_k_k, a_k_a=a_k_a, a_r_k=a_r_k,
              a_gn_g=a_gn_g, a_gn_b=a_gn_b, b_w_in=b_w_in, b_b_f=b_b_f, b_q_gain=b_q_gain,
              b_k_gain=b_k_gain)
    n_a, n_c = a_mu.shape[0], c_w_in.shape[0]
    prep = dict(
        rwkv=_prep_rwkv(wt), fox=_prep_fox(wt),
        a_w_o=a_w_o.astype(BF16), b_w_o=b_w_o.astype(BF16), c_w_o=c_w_o.astype(BF16),
        c_w_in=c_w_in.astype(BF16), c_gn_g=_rows_of(c_gn_g), c_gn_b=_rows_of(c_gn_b),
        ffn_w1=ffn_w1.astype(BF16), ffn_w2=ffn_w2.astype(BF16),
        ple_gate=ple_gate.astype(BF16), ple_in=ple_in.astype(BF16),
        ln_mix_g=_rows_of(ln_mix_g), ln_mix_b=_rows_of(ln_mix_b),
        ln_ffn_g=_rows_of(ln_ffn_g), ln_ffn_b=_rows_of(ln_ffn_b))
    nbat = x_prompt.shape[0]
    zero_shift = jnp.zeros((n_a, nbat, D_MODEL), x_prompt.dtype)
    zero_wkv = jnp.zeros((n_a, nbat, A_HEADS, A_HEAD, A_HEAD), F32)
    zero_ret = jnp.zeros((n_c, nbat, C_HEADS, C_QK, C_V), F32)
    past_len = cache_b_k.shape[2]
    y_prompt, pa_shift, pa_wkv, pb_k, pb_v, pb_logf, pc_state = _run_trunk(
        x_prompt, p_prompt, 0, zero_shift, zero_wkv, None, zero_ret, prep)
    y_sample, sa_shift, sa_wkv, sb_k, sb_v, sb_logf, sc_state = _run_trunk(
        x_sample, p_sample, past_len, state_a_shift, state_a_wkv, (cache_b_k, cache_b_v, cache_b_logf),
        state_c, prep)
    return (y_prompt, y_sample, pa_shift, pa_wkv, pb_k, pb_v, pb_logf, pc_state,
            sa_shift, sa_wkv, sb_k, sb_v, sb_logf, sc_state)
```
